```python
import math
import jax, jax.numpy as jnp
from jax import lax
import numpy as np

D_MODEL = 1024
BATCH = 2
SEQ = 8192
DEPTH = 2
DEC_BATCH = 128
DEC_SEQ = 4
PAST_LEN = 2048
PAGE_SIZE = 128

HD = 64
ROPE_DIM = HD // 4
ROPE_THETA = 500000.0
NSA_H = D_MODEL // (2 * HD)
NSA_KVH = max(1, NSA_H // 4)
NSA_GROUP = NSA_H // NSA_KVH
CMP_BLOCK = 64
N_SEL = 16
WINDOW = 512
FOX_H = D_MODEL // (2 * HD)
DIFF_H = D_MODEL // (2 * HD)
D_FF = 4 * D_MODEL
Q_BLOCK = 128
EPS = 1e-6
FORGET_BIAS = 3.0
N_EVEN = (DEPTH + 1) // 2
N_ODD = DEPTH // 2
EVEN_SIZES = [NSA_H * HD, 2 * NSA_KVH * HD, 2 * NSA_KVH * HD, 2 * NSA_KVH * HD, 3 * NSA_H,
              FOX_H * HD, 2 * FOX_H * HD, FOX_H]
EVEN_COLS = sum(EVEN_SIZES)
EVEN_OUT = (NSA_H + FOX_H) * HD
ODD_SIZES = [DIFF_H * 2 * HD] * 3
ODD_COLS = sum(ODD_SIZES)
ODD_OUT = DIFF_H * 2 * HD

kernel_name = 'nsa_fox_diff_adaln_decoder_step'

f32 = jnp.float32


def _split(x, sizes):
    idx = np.cumsum(sizes)[:-1].tolist()
    return jnp.split(x, idx, axis=-1)


def rms_norm(x, g):
    xf = x.astype(f32)
    y = xf * lax.rsqrt(jnp.mean(xf * xf, axis=-1, keepdims=True) + EPS)
    return (y * g.astype(f32)).astype(x.dtype)


def modulate(x, g, shift, scale):
    return rms_norm(x, g) * (1.0 + scale[:, None, :]) + shift[:, None, :]


def rope(x, pos):
    half = ROPE_DIM // 2
    inv = ROPE_THETA ** (-jnp.arange(half, dtype=f32) / half)
    ang = pos.astype(f32)[:, None] * inv[None, :]
    cos = jnp.cos(ang)[:, None, :]
    sin = jnp.sin(ang)[:, None, :]
    xr = x[..., :ROPE_DIM].astype(f32)
    x1, x2 = xr[..., :half], xr[..., half:]
    rot = jnp.concatenate([x1 * cos - x2 * sin, x2 * cos + x1 * sin], axis=-1).astype(x.dtype)
    return jnp.concatenate([rot, x[..., ROPE_DIM:]], axis=-1)


def rope_kv(kv, pos):
    return jnp.stack([rope(kv[:, :, 0], pos), kv[:, :, 1]], axis=2)


def masked_softmax(s, mask):
    s = jnp.where(mask, s.astype(f32), -1e30)
    m = jnp.max(s, axis=-1, keepdims=True)
    p = jnp.exp(s - m) * mask
    return p / jnp.maximum(jnp.sum(p, axis=-1, keepdims=True), 1e-30)


def gather_pages(pool, page_table):
    g = pool[page_table]
    return g.reshape((page_table.shape[0], page_table.shape[1] * PAGE_SIZE) + pool.shape[2:])


def unchunk(o, B, T):
    o = jnp.moveaxis(o, 0, 1)
    return o.reshape((B, T) + o.shape[3:])


def compress(kv, w_pool):
    B, L = kv.shape[:2]
    nb = L // CMP_BLOCK
    blocks = kv[:, :nb * CMP_BLOCK].reshape(B, nb, CMP_BLOCK, 2, NSA_KVH, HD)
    return jnp.einsum('bncegd,ecg->bnegd', blocks, w_pool)


def nsa_attend(q, qpos0, kv_cmp, kv_slc, kv_win, win_pos0, gates, w_pool):
    B, T = q.shape[:2]
    scale = HD ** -0.5
    qpos = qpos0 + jnp.arange(T, dtype=jnp.int32)
    qg = q.reshape(B, T, NSA_KVH, NSA_GROUP, HD)
    cmp = compress(kv_cmp, w_pool)
    nb = cmp.shape[1]
    s_c = jnp.einsum('btgjd,bngd->bgjtn', qg, cmp[:, :, 0]) * scale
    blk_end = (jnp.arange(nb, dtype=jnp.int32) + 1) * CMP_BLOCK - 1
    p_cmp = masked_softmax(s_c, blk_end[None, :] <= qpos[:, None])
    o_cmp = jnp.einsum('bgjtn,bngd->btgjd', p_cmp.astype(q.dtype), cmp[:, :, 1])
    cur = qpos // CMP_BLOCK
    imp = jnp.sum(p_cmp, axis=2)
    imp = jnp.where(jnp.arange(nb, dtype=jnp.int32)[None, :] < cur[:, None], imp, -1.0)
    n_top = min(N_SEL - 1, nb)
    top_v, top_i = lax.top_k(imp, n_top)
    sel = jnp.concatenate([jnp.broadcast_to(cur[None, None, :, None], (B, NSA_KVH, T, 1)),
                           top_i.astype(jnp.int32)], axis=-1)
    sel_ok = jnp.concatenate([jnp.ones((B, NSA_KVH, T, 1), dtype=bool), top_v >= 0.0], axis=-1)
    ns = sel.shape[-1]
    L = kv_slc.shape[1]
    nb_sel = -(-L // CMP_BLOCK)
    slc = jnp.pad(kv_slc, ((0, 0), (0, nb_sel * CMP_BLOCK - L), (0, 0), (0, 0), (0, 0)))
    slc = slc.reshape(B, nb_sel, CMP_BLOCK, 2, NSA_KVH, HD).transpose(0, 4, 1, 2, 3, 5)
    win = jnp.pad(kv_win, ((0, 0), (WINDOW, 0), (0, 0), (0, 0), (0, 0)))
    qb = min(Q_BLOCK, T)
    n_chunks = T // qb
    b_ix = jnp.arange(B)[:, None, None, None]
    g_ix = jnp.arange(NSA_KVH)[None, :, None, None]

    def chunk(ci):
        t0 = ci * qb
        tq = qpos0 + t0 + jnp.arange(qb, dtype=jnp.int32)
        qc = lax.dynamic_slice_in_dim(qg, t0, qb, axis=1)
        sc = lax.dynamic_slice_in_dim(sel, t0, qb, axis=2)
        okc = lax.dynamic_slice_in_dim(sel_ok, t0, qb, axis=2)
        kvs = slc[b_ix, g_ix, sc]
        spos = sc[..., None] * CMP_BLOCK + jnp.arange(CMP_BLOCK, dtype=jnp.int32)
        m_s = okc[..., None] & (spos <= tq[None, None, :, None, None])
        s_s = jnp.einsum('btgjd,bgtnkd->bgjtnk', qc, kvs[..., 0, :]) * scale
        p_s = masked_softmax(s_s.reshape(B, NSA_KVH, NSA_GROUP, qb, ns * CMP_BLOCK),
                             m_s.reshape(B, NSA_KVH, 1, qb, ns * CMP_BLOCK))
        o_s = jnp.einsum('bgjtm,bgtmd->btgjd', p_s.astype(q.dtype),
                         kvs[..., 1, :].reshape(B, NSA_KVH, qb, ns * CMP_BLOCK, HD))
        kw = lax.dynamic_slice_in_dim(win, qpos0 + t0 - win_pos0, WINDOW + qb, axis=1)
        kabs = qpos0 + t0 - WINDOW + jnp.arange(WINDOW + qb, dtype=jnp.int32)
        dist = tq[:, None] - kabs[None, :]
        m_w = (kabs[None, :] >= win_pos0) & (dist >= 0) & (dist < WINDOW)
        s_w = jnp.einsum('btgjd,bsgd->bgjts', qc, kw[:, :, 0]) * scale
        p_w = masked_softmax(s_w, m_w)
        o_w = jnp.einsum('bgjts,bsgd->btgjd', p_w.astype(q.dtype), kw[:, :, 1])
        return o_s, o_w

    o_s, o_w = lax.map(chunk, jnp.arange(n_chunks))
    g = jax.nn.sigmoid(gates.astype(f32)).astype(q.dtype)
    return (g[..., 0:1] * o_cmp.reshape(B, T, NSA_H, HD)
            + g[..., 1:2] * unchunk(o_s, B, T).reshape(B, T, NSA_H, HD)
            + g[..., 2:3] * unchunk(o_w, B, T).reshape(B, T, NSA_H, HD))


def fox_attend(q, qpos0, kv, logf):
    B, T = q.shape[:2]
    L = kv.shape[1]
    scale = HD ** -0.5
    F = jnp.cumsum(logf.astype(f32), axis=1)
    FT = jnp.transpose(F, (0, 2, 1))
    Fq = F[:, qpos0:]
    kpos = jnp.arange(L, dtype=jnp.int32)
    qb = min(Q_BLOCK, T)

    def chunk(ci):
        t0 = ci * qb
        tq = qpos0 + t0 + jnp.arange(qb, dtype=jnp.int32)
        qc = lax.dynamic_slice_in_dim(q, t0, qb, axis=1)
        fc = lax.dynamic_slice_in_dim(Fq, t0, qb, axis=1)
        s = jnp.einsum('bthd,bshd->bhts', qc, kv[:, :, 0]).astype(f32) * scale
        s = s + jnp.transpose(fc, (0, 2, 1))[..., None] - FT[:, :, None, :]
        p = masked_softmax(s, kpos[None, :] <= tq[:, None])
        return jnp.einsum('bhts,bshd->bthd', p.astype(q.dtype), kv[:, :, 1])

    return unchunk(lax.map(chunk, jnp.arange(T // qb)), B, T)


def diff_attend(q, qpos0, kv, lam, lam_init, g):
    B, T = q.shape[:2]
    L = kv.shape[1]
    scale = HD ** -0.5
    k = kv[:, :, 0].reshape(B, L, DIFF_H, 2, HD)
    v = kv[:, :, 1]
    kpos = jnp.arange(L, dtype=jnp.int32)
    qb = min(Q_BLOCK, T)

    def chunk(ci):
        t0 = ci * qb
        tq = qpos0 + t0 + jnp.arange(qb, dtype=jnp.int32)
        qc = lax.dynamic_slice_in_dim(q, t0, qb, axis=1)
        s = jnp.einsum('bthcd,bshcd->bhcts', qc, k) * scale
        p = masked_softmax(s, kpos[None, :] <= tq[:, None])
        a = p[:, :, 0] - lam * p[:, :, 1]
        return jnp.einsum('bhts,bshe->bthe', a.astype(q.dtype), v)

    o = unchunk(lax.map(chunk, jnp.arange(T // qb)), B, T)
    return rms_norm(o, g) * (1.0 - lam_init)


def trunk(x, c, qpos0, page_table, caches, params):
    (w_in_even, b_f, w_pool, w_out_even, w_in_odd, lam_qk, subln_g, w_out_odd,
     norm_mix_g, norm_ffn_g, w_ada, b_ada, w_ff_in, w_ff_out, final_g) = params
    B, T, _ = x.shape
    pos = qpos0 + jnp.arange(T, dtype=jnp.int32)
    new_cmp, new_slc, new_win, new_fox, new_logf, new_diff = [], [], [], [], [], []
    for li in range(DEPTH):
        mod = jax.nn.silu(c) @ w_ada[li] + b_ada[li]
        sh1, sc1, gt1, sh2, sc2, gt2 = jnp.split(mod, 6, axis=-1)
        h = modulate(x, norm_mix_g[li], sh1, sc1)
        if li % 2 == 0:
            e = li // 2
            q_n, kv_c, kv_s, kv_w, gates, q_f, kv_f, f_f = _split(h @ w_in_even[e], EVEN_SIZES)
            q_n = rope(q_n.reshape(B, T, NSA_H, HD), pos)
            kv_c = rope_kv(kv_c.reshape(B, T, 2, NSA_KVH, HD), pos)
            kv_s = rope_kv(kv_s.reshape(B, T, 2, NSA_KVH, HD), pos)
            kv_w = rope_kv(kv_w.reshape(B, T, 2, NSA_KVH, HD), pos)
            gates = gates.reshape(B, T, NSA_H, 3)
            q_f = q_f.reshape(B, T, FOX_H, HD)
            kv_f = kv_f.reshape(B, T, 2, FOX_H, HD)
            logf = jax.nn.log_sigmoid(f_f.astype(f32) + b_f[e].astype(f32))
            new_cmp.append(kv_c)
            new_slc.append(kv_s)
            new_fox.append(kv_f)
            new_logf.append(logf)
            if caches is None:
                kvc_all, kvs_all, kvf_all, logf_all = kv_c, kv_s, kv_f, logf
                kvw_all, win_pos0 = kv_w, 0
                new_win.append(kv_w[:, T - min(WINDOW, T):])
            else:
                c_cmp, c_slc, c_win, c_fox, c_logf, c_diff = caches
                kvc_all = jnp.concatenate([gather_pages(c_cmp[e], page_table), kv_c], axis=1)
                kvs_all = jnp.concatenate([gather_pages(c_slc[e], page_table), kv_s], axis=1)
                kvf_all = jnp.concatenate([gather_pages(c_fox[e], page_table), kv_f], axis=1)
                logf_all = jnp.concatenate([gather_pages(c_logf[e], page_table).astype(f32), logf], axis=1)
                w_buf = c_win.shape[2]
                kvw_all = jnp.concatenate([c_win[e], kv_w], axis=1)
                win_pos0 = qpos0 - w_buf
                new_win.append(kvw_all[:, T:])
            o_nsa = nsa_attend(q_n, qpos0, kvc_all, kvs_all, kvw_all, win_pos0, gates, w_pool[e])
            o_fox = fox_attend(q_f, qpos0, kvf_all, logf_all)
            mix = jnp.concatenate([o_nsa.reshape(B, T, NSA_H * HD), o_fox.reshape(B, T, FOX_H * HD)],
                                  axis=-1) @ w_out_even[e]
        else:
            o = li // 2
            q_d, k_d, v_d = _split(h @ w_in_odd[o], ODD_SIZES)
            q_d = rope(q_d.reshape(B, T, 2 * DIFF_H, HD), pos).reshape(B, T, DIFF_H, 2, HD)
            k_d = rope(k_d.reshape(B, T, 2 * DIFF_H, HD), pos).reshape(B, T, DIFF_H, 2 * HD)
            kv_d = jnp.stack([k_d, v_d.reshape(B, T, DIFF_H, 2 * HD)], axis=2)
            new_diff.append(kv_d)
            if caches is None:
                kvd_all = kv_d
            else:
                kvd_all = jnp.concatenate([gather_pages(caches[5][o], page_table), kv_d], axis=1)
            lp = lam_qk[o].astype(f32)
            lam_init = 0.8 - 0.6 * math.exp(-0.3 * li)
            lam = jnp.exp(jnp.sum(lp[0] * lp[1])) - jnp.exp(jnp.sum(lp[2] * lp[3])) + lam_init
            mix = diff_attend(q_d, qpos0, kvd_all, lam, lam_init, subln_g[o]).reshape(B, T, ODD_OUT) @ w_out_odd[o]
        x = x + gt1[:, None, :] * mix
        h2 = modulate(x, norm_ffn_g[li], sh2, sc2)
        ff = jnp.square(jax.nn.relu(h2 @ w_ff_in[li])) @ w_ff_out[li]
        x = x + gt2[:, None, :] * ff
    y = rms_norm(x, final_g)
    return (y, jnp.stack(new_cmp, 0), jnp.stack(new_slc, 0), jnp.stack(new_win, 0),
            jnp.stack(new_fox, 0), jnp.stack(new_logf, 0), jnp.stack(new_diff, 0))


def setup_inputs(seed: int = 0) -> dict:
    key = jax.random.key(seed)
    ks = jax.random.split(key, 32)
    n_pages = PAST_LEN // PAGE_SIZE
    n_used = DEC_BATCH * n_pages
    n_phys = (n_used * 5) // 4
    w_buf = min(WINDOW, PAST_LEN)

    def nrm(k, shape, s=1.0):
        return s * jax.random.normal(k, shape, f32)

    page_table = jax.random.permutation(ks[0], n_phys)[:n_used].reshape(DEC_BATCH, n_pages).astype(jnp.int32)
    return {
        'x_prompt': nrm(ks[1], (BATCH, SEQ, D_MODEL)),
        'x_sample': nrm(ks[2], (DEC_BATCH, DEC_SEQ, D_MODEL)),
        'cache_nsa_cmp_kv': nrm(ks[3], (N_EVEN, n_phys, PAGE_SIZE, 2, NSA_KVH, HD)),
        'cache_nsa_slc_kv': nrm(ks[4], (N_EVEN, n_phys, PAGE_SIZE, 2, NSA_KVH, HD)),
        'cache_nsa_win_kv': nrm(ks[5], (N_EVEN, DEC_BATCH, w_buf, 2, NSA_KVH, HD)),
        'cache_fox_kv': nrm(ks[6], (N_EVEN, n_phys, PAGE_SIZE, 2, FOX_H, HD)),
        'cache_fox_logf': jax.nn.log_sigmoid(nrm(ks[7], (N_EVEN, n_phys, PAGE_SIZE, FOX_H)) + FORGET_BIAS),
        'cache_diff_kv': nrm(ks[8], (N_ODD, n_phys, PAGE_SIZE, 2, DIFF_H, 2 * HD)),
        'page_table': page_table,
        'c_prompt': nrm(ks[9], (BATCH, D_MODEL)),
        'c_sample': nrm(ks[10], (DEC_BATCH, D_MODEL)),
        'w_in_even': nrm(ks[11], (N_EVEN, D_MODEL, EVEN_COLS), D_MODEL ** -0.5),
        'b_f': FORGET_BIAS + nrm(ks[12], (N_EVEN, FOX_H), 0.5),
        'w_pool': (1.0 + nrm(ks[13], (N_EVEN, 2, CMP_BLOCK, NSA_KVH), 0.1)) / CMP_BLOCK,
        'w_out_even': nrm(ks[14], (N_EVEN, EVEN_OUT, D_MODEL), EVEN_OUT ** -0.5),
        'w_in_odd': nrm(ks[15], (N_ODD, D_MODEL, ODD_COLS), D_MODEL ** -0.5),
        'lam_qk': nrm(ks[16], (N_ODD, 4, HD), 0.1),
        'subln_g': 1.0 + nrm(ks[17], (N_ODD, 2 * HD), 0.05),
        'w_out_odd': nrm(ks[18], (N_ODD, ODD_OUT, D_MODEL), ODD_OUT ** -0.5),
        'norm_mix_g': 1.0 + nrm(ks[19], (DEPTH, D_MODEL), 0.05),
        'norm_ffn_g': 1.0 + nrm(ks[20], (DEPTH, D_MODEL), 0.05),
        'w_ada': nrm(ks[21], (DEPTH, D_MODEL, 6 * D_MODEL), 0.2 * D_MODEL ** -0.5),
        'b_ada': nrm(ks[22], (DEPTH, 6 * D_MODEL), 0.02),
        'w_ff_in': nrm(ks[23], (DEPTH, D_MODEL, D_FF), D_MODEL ** -0.5),
        'w_ff_out': nrm(ks[24], (DEPTH, D_FF, D_MODEL), D_FF ** -0.5),
        'final_g': 1.0 + nrm(ks[25], (D_MODEL,), 0.05),
    }


def reference(x_prompt, x_sample, cache_nsa_cmp_kv, cache_nsa_slc_kv, cache_nsa_win_kv, cache_fox_kv,
              cache_fox_logf, cache_diff_kv, page_table, c_prompt, c_sample, w_in_even, b_f, w_pool,
              w_out_even, w_in_odd, lam_qk, subln_g, w_out_odd, norm_mix_g, norm_ffn_g, w_ada, b_ada,
              w_ff_in, w_ff_out, final_g):
    params = (w_in_even, b_f, w_pool, w_out_even, w_in_odd, lam_qk, subln_g, w_out_odd,
              norm_mix_g, norm_ffn_g, w_ada, b_ada, w_ff_in, w_ff_out, final_g)
    caches = (cache_nsa_cmp_kv, cache_nsa_slc_kv, cache_nsa_win_kv, cache_fox_kv, cache_fox_logf, cache_diff_kv)
    y_prompt, p_cmp, p_slc, p_win, p_fox, p_logf, p_diff = trunk(x_prompt, c_prompt, 0, None, None, params)
    y_sample, s_cmp, s_slc, s_win, s_fox, s_logf, s_diff = trunk(x_sample, c_sample, PAST_LEN, page_table,
                                                                  caches, params)
    return (y_prompt, y_sample, p_cmp, s_cmp, p_slc, s_slc, p_win, s_win, p_fox, s_fox, p_logf, s_logf,
            p_diff, s_diff)
```

```python
import functools
import math

import numpy as np
import jax
import jax.numpy as jnp
from jax import lax
from jax.experimental import pallas as pl
from jax.experimental.pallas import tpu as pltpu

HD = 64
ROPE_DIM = HD // 4
ROPE_HALF = ROPE_DIM // 2
ROPE_THETA = 500000.0
NSA_H = 8
NSA_KVH = 2
CMP_BLOCK = 64
CMP_SHIFT = 6
N_SEL = 16
WINDOW = 512
FOX_H = 8
DIFF_H = 8
EPS = 1e-6
PAGE_SIZE = 128

LANES = 128
VMEM_LIMIT_BYTES = 48 * 1024 * 1024

NEG_MASK = -1e30
NEG_BLOCK = -1e9
PAGES_PER_STEP = 4
KV_CHUNK = PAGES_PER_STEP * PAGE_SIZE

NSA_PERM = (0, 4, 1, 5, 2, 6, 3, 7)

f32 = jnp.float32
bf16 = jnp.bfloat16


def _cparams(*sem):
    return pltpu.CompilerParams(dimension_semantics=sem, vmem_limit_bytes=VMEM_LIMIT_BYTES)


def _dot(a, b):
    return jnp.dot(a, b, preferred_element_type=f32)


def _dot_nt(a, b):
    return lax.dot_general(a, b, (((1,), (1,)), ((), ())), preferred_element_type=f32)


def _rms_mod(x, g, sc, sh):
    y = x * lax.rsqrt(jnp.mean(x * x, axis=-1, keepdims=True) + EPS)
    return (y * g) * (1.0 + sc) + sh


def _ada_body(c_ref, w_ref, b_ref, o_ref):
    c = c_ref[...]
    a = (c * jax.nn.sigmoid(c)).astype(bf16)
    o_ref[0] = _dot(a, w_ref[0]) + b_ref[0]


def _ada(c_all, w_ada_b, b_ada):
    depth, d, n = w_ada_b.shape
    rows = c_all.shape[0]
    tn = n // 4
    return pl.pallas_call(
        _ada_body,
        out_shape=jax.ShapeDtypeStruct((depth, rows, n), f32),
        grid=(depth, n // tn),
        in_specs=[pl.BlockSpec((rows, d), lambda l, j: (0, 0)),
                  pl.BlockSpec((1, d, tn), lambda l, j: (l, 0, j)),
                  pl.BlockSpec((1, 1, tn), lambda l, j: (l, 0, j))],
        out_specs=pl.BlockSpec((1, rows, tn), lambda l, j: (l, 0, j)),
        compiler_params=_cparams("parallel", "parallel"),
        name="ada_mod",
    )(c_all, w_ada_b, b_ada.reshape(depth, 1, n))


def _rope_group(a, cos, sa, sb):
    return a * cos + pltpu.roll(a, LANES - ROPE_HALF, 1) * sa + pltpu.roll(a, ROPE_HALF, 1) * sb


def _proj_body(plan, n_out, x_ref, g_ref, sc_ref, sh_ref, w_ref, cos_ref, sa_ref, sb_ref, bias_ref, *out_refs):
    h = _rms_mod(x_ref[...], g_ref[...], sc_ref[0], sh_ref[0]).astype(bf16)
    cos, sa, sb = cos_ref[...], sa_ref[...], sb_ref[...]
    for src, groups in plan:
        width = LANES * len(groups)
        acc = _dot(h, w_ref[:, src:src + width])
        for gi, (kind, scale, dests) in enumerate(groups):
            a = acc[:, gi * LANES:(gi + 1) * LANES]
            if kind == "rope":
                a = _rope_group(a, cos, sa, sb)
            elif kind == "sigmoid":
                a = jax.nn.sigmoid(a)
            elif kind == "logsig":
                z = a + bias_ref[...]
                a = jnp.minimum(z, 0.0) - jnp.log1p(jnp.exp(-jnp.abs(z)))
            if scale != 1.0:
                a = a * scale
            for oi, col in dests:
                out_refs[oi][:, col:col + LANES] = a.astype(out_refs[oi].dtype)


def _proj(x2d, g, sc, sh, tiles_per_mod, w_b, tables, bias, plan, out_defs, tm):
    m, d = x2d.shape
    n = w_b.shape[1]
    cos, sa, sb = tables
    mod_rows = sc.shape[1]
    body = functools.partial(_proj_body, plan, len(out_defs))
    row = lambda i: (i, 0)
    const = lambda i: (0, 0)
    mod = lambda i: (i // tiles_per_mod, 0, 0)
    return pl.pallas_call(
        body,
        out_shape=[jax.ShapeDtypeStruct((m, w), dt) for w, dt in out_defs],
        grid=(m // tm,),
        in_specs=[pl.BlockSpec((tm, d), row),
                  pl.BlockSpec((1, d), const),
                  pl.BlockSpec((1, mod_rows, d), mod),
                  pl.BlockSpec((1, mod_rows, d), mod),
                  pl.BlockSpec((d, n), const, pipeline_mode=pl.Buffered(1)),
                  pl.BlockSpec((tm, LANES), row),
                  pl.BlockSpec((tm, LANES), row),
                  pl.BlockSpec((tm, LANES), row),
                  pl.BlockSpec((1, LANES), const)],
        out_specs=[pl.BlockSpec((tm, w), row) for w, _ in out_defs],
        compiler_params=_cparams("parallel"),
        name="norm_mod_proj",
    )(x2d, g.reshape(1, d), sc, sh, w_b, cos, sa, sb, bias)


def _rope_tables(pos):
    inv = np.float32(ROPE_THETA) ** (-np.arange(ROPE_HALF, dtype=np.float32) / np.float32(ROPE_HALF))
    ang = (pos.astype(np.float32)[:, None] * inv[None, :]).astype(np.float64)
    n = pos.shape[0]
    cos = np.ones((n, HD), np.float64)
    sa = np.zeros((n, HD), np.float64)
    sb = np.zeros((n, HD), np.float64)
    cos[:, :ROPE_HALF] = np.cos(ang)
    cos[:, ROPE_HALF:ROPE_DIM] = np.cos(ang)
    sa[:, :ROPE_HALF] = -np.sin(ang)
    sb[:, ROPE_HALF:ROPE_DIM] = np.sin(ang)
    tile = lambda t: jnp.asarray(np.tile(t, (1, LANES // HD)).astype(np.float32))
    return tile(cos), tile(sa), tile(sb)


EV_QN, EV_KVC, EV_KVS, EV_KVW, EV_QF, EV_KVF, EV_GATE, EV_FF = 0, 512, 768, 1024, 1280, 1792, 2816, 4352
EV_COLS = 4480


def _even_weight(w):
    d = w.shape[0]
    sizes = [NSA_H * HD, 2 * NSA_KVH * HD, 2 * NSA_KVH * HD, 2 * NSA_KVH * HD, 3 * NSA_H, FOX_H * HD,
             2 * FOX_H * HD, FOX_H]
    offs = np.concatenate([[0], np.cumsum(sizes)])
    q_n, kv_c, kv_s, kv_w, gates, q_f, kv_f, f_f = [w[:, offs[i]:offs[i + 1]] for i in range(8)]
    q_n = q_n.reshape(d, NSA_H, HD)[:, NSA_PERM, :].reshape(d, NSA_H * HD)
    gates = gates.reshape(d, NSA_H, 3)[:, NSA_PERM, :]
    gates = jnp.repeat(jnp.transpose(gates, (0, 2, 1))[..., None], HD, axis=-1).reshape(d, 3 * NSA_H * HD)
    f_f = jnp.pad(f_f, ((0, 0), (0, LANES - FOX_H)))
    return jnp.concatenate([q_n, kv_c, kv_s, kv_w, q_f, kv_f, gates, f_f], axis=1).astype(bf16)


(O_QN, O_KVC, O_KVS, O_KS, O_VS, O_KVW, O_KW, O_VW, O_QF, O_KVF, O_KF, O_VF, O_GATE, O_LOGF) = range(14)
EVEN_OUT_DEFS = [(512, bf16), (256, f32), (256, f32), (128, bf16), (128, bf16), (256, f32), (128, bf16),
                 (128, bf16), (512, bf16), (1024, f32), (512, bf16), (512, bf16), (1536, f32), (128, f32)]


def _even_plan():
    scale = HD ** -0.5
    plan = [(EV_QN, [("rope", scale, [(O_QN, LANES * i)]) for i in range(4)]),
            (EV_KVC, [("rope", 1.0, [(O_KVC, 0)]), ("plain", 1.0, [(O_KVC, LANES)])]),
            (EV_KVS, [("rope", 1.0, [(O_KVS, 0), (O_KS, 0)]), ("plain", 1.0, [(O_KVS, LANES), (O_VS, 0)])]),
            (EV_KVW, [("rope", 1.0, [(O_KVW, 0), (O_KW, 0)]), ("plain", 1.0, [(O_KVW, LANES), (O_VW, 0)])]),
            (EV_QF, [("plain", scale, [(O_QF, LANES * i)]) for i in range(4)]),
            (EV_KVF, [("plain", 1.0, [(O_KVF, LANES * i), (O_KF, LANES * i)]) for i in range(4)]),
            (EV_KVF + 512, [("plain", 1.0, [(O_KVF, 512 + LANES * i), (O_VF, LANES * i)]) for i in range(4)])]
    for j in range(3):
        plan.append((EV_GATE + 512 * j, [("sigmoid", 1.0, [(O_GATE, 512 * j + LANES * i)]) for i in range(4)]))
    plan.append((EV_FF, [("logsig", 1.0, [(O_LOGF, 0)])]))
    return plan


(O_QD, O_KVD, O_KD, O_VD) = range(4)
ODD_OUT_DEFS = [(1024, bf16), (2048, f32), (1024, bf16), (1024, bf16)]


def _odd_plan():
    scale = HD ** -0.5
    plan = []
    for c in range(2):
        plan.append((512 * c, [("rope", scale, [(O_QD, 512 * c + LANES * i)]) for i in range(4)]))
    for c in range(2):
        plan.append((1024 + 512 * c,
                     [("rope", 1.0, [(O_KVD, 512 * c + LANES * i), (O_KD, 512 * c + LANES * i)]) for i in range(4)]))
    for c in range(2):
        plan.append((2048 + 512 * c,
                     [("plain", 1.0, [(O_KVD, 1024 + 512 * c + LANES * i), (O_VD, 512 * c + LANES * i)])
                      for i in range(4)]))
    return plan


def _outproj_even_body(x_ref, gt_ref, oc_ref, os_ref, ow_ref, gates_ref, of_ref, w_ref, o_ref):
    nw = NSA_H * HD
    o_nsa = (gates_ref[:, 0:nw] * oc_ref[...] + gates_ref[:, nw:2 * nw] * os_ref[...]
             + gates_ref[:, 2 * nw:3 * nw] * ow_ref[...])
    a = jnp.concatenate([o_nsa.astype(bf16), of_ref[...]], axis=1)
    o_ref[...] = x_ref[...] + gt_ref[0] * _dot(a, w_ref[...])


def _outproj_odd_body(x_ref, gt_ref, od_ref, w_ref, o_ref):
    o_ref[...] = x_ref[...] + gt_ref[0] * _dot(od_ref[...], w_ref[...])


def _outproj(body, x2d, gt, tiles_per_mod, acts, w_b, tm, name):
    m, d = x2d.shape
    row = lambda i: (i, 0)
    mod = lambda i: (i // tiles_per_mod, 0, 0)
    return pl.pallas_call(
        body,
        out_shape=jax.ShapeDtypeStruct((m, d), f32),
        grid=(m // tm,),
        in_specs=[pl.BlockSpec((tm, d), row), pl.BlockSpec((1, gt.shape[1], d), mod)]
                 + [pl.BlockSpec((tm, a.shape[1]), row) for a in acts]
                 + [pl.BlockSpec(w_b.shape, lambda i: (0, 0), pipeline_mode=pl.Buffered(1))],
        out_specs=pl.BlockSpec((tm, d), row),
        compiler_params=_cparams("parallel"),
        name=name,
    )(x2d, gt, *acts, w_b)


def _mlp_body(final, x_ref, g_ref, sc_ref, sh_ref, gt_ref, w1_ref, w2_ref, fg_ref, o_ref, h_ref, acc_ref):
    f = pl.program_id(1)

    @pl.when(f == 0)
    def _():
        h_ref[...] = _rms_mod(x_ref[...], g_ref[...], sc_ref[0], sh_ref[0]).astype(bf16)
        acc_ref[...] = jnp.zeros_like(acc_ref)

    a = jnp.maximum(_dot(h_ref[...], w1_ref[...]), 0.0)
    acc_ref[...] += _dot((a * a).astype(bf16), w2_ref[...])

    @pl.when(f == pl.num_programs(1) - 1)
    def _():
        y = x_ref[...] + gt_ref[0] * acc_ref[...]
        if final:
            y = y * lax.rsqrt(jnp.mean(y * y, axis=-1, keepdims=True) + EPS) * fg_ref[...]
        o_ref[...] = y


def _mlp(x2d, g, sc, sh, gt, tiles_per_mod, w1_b, w2_b, final_g, final, tm, tf):
    m, d = x2d.shape
    dff = w1_b.shape[1]
    mod_rows = sc.shape[1]
    row = lambda i, f: (i, 0)
    const = lambda i, f: (0, 0)
    mod = lambda i, f: (i // tiles_per_mod, 0, 0)
    return pl.pallas_call(
        functools.partial(_mlp_body, final),
        out_shape=jax.ShapeDtypeStruct((m, d), f32),
        grid=(m // tm, dff // tf),
        in_specs=[pl.BlockSpec((tm, d), row), pl.BlockSpec((1, d), const),
                  pl.BlockSpec((1, mod_rows, d), mod), pl.BlockSpec((1, mod_rows, d), mod),
                  pl.BlockSpec((1, mod_rows, d), mod),
                  pl.BlockSpec((d, tf), lambda i, f: (0, f)), pl.BlockSpec((tf, d), lambda i, f: (f, 0)),
                  pl.BlockSpec((1, d), const)],
        out_specs=pl.BlockSpec((tm, d), row),
        scratch_shapes=[pltpu.VMEM((tm, d), bf16), pltpu.VMEM((tm, d), f32)],
        compiler_params=_cparams("parallel", "arbitrary"),
        name="relu2_mlp",
    )(x2d, g.reshape(1, d), sc, sh, gt, w1_b, w2_b, final_g.reshape(1, d))


def _compress_body(pt_ref, *refs):
    pages, wp_ref, o_ref = refs[:PAGES_PER_STEP], refs[PAGES_PER_STEP], refs[PAGES_PER_STEP + 1]
    wp = wp_ref[...]
    rows = []
    for p in pages:
        for half in range(PAGE_SIZE // CMP_BLOCK):
            blk = p[0, half * CMP_BLOCK:(half + 1) * CMP_BLOCK, :] * wp
            rows.append(jnp.sum(blk, axis=0, keepdims=True))
    o_ref[0] = jnp.concatenate(rows, axis=0)


def _compress(pool, table, w_pool_e):
    b, n_pages = table.shape
    c = pool.shape[2]
    wp = jnp.repeat(jnp.transpose(w_pool_e, (1, 0, 2))[..., None], HD, axis=-1).reshape(CMP_BLOCK, c)
    rows_per_step = PAGES_PER_STEP * PAGE_SIZE // CMP_BLOCK
    page_spec = lambda i: pl.BlockSpec((1, PAGE_SIZE, c), lambda bi, g, pt: (pt[bi, PAGES_PER_STEP * g + i], 0, 0))
    grid_spec = pltpu.PrefetchScalarGridSpec(
        num_scalar_prefetch=1, grid=(b, n_pages // PAGES_PER_STEP),
        in_specs=[page_spec(i) for i in range(PAGES_PER_STEP)] + [pl.BlockSpec((CMP_BLOCK, c), lambda bi, g, pt: (0, 0))],
        out_specs=pl.BlockSpec((1, rows_per_step, c), lambda bi, g, pt: (bi, g, 0)))
    return pl.pallas_call(
        _compress_body, grid_spec=grid_spec,
        out_shape=jax.ShapeDtypeStruct((b, n_pages * PAGE_SIZE // CMP_BLOCK, c), f32),
        compiler_params=_cparams("parallel", "parallel"),
        name="nsa_compress",
    )(table, *([pool] * PAGES_PER_STEP), wp)


def _gather_body(slabs, n_groups, pt_ref, *refs):
    pages, new_ref, outs = refs[:PAGES_PER_STEP], refs[PAGES_PER_STEP], refs[PAGES_PER_STEP + 1:]
    g = pl.program_id(1)

    @pl.when(g < n_groups)
    def _():
        for i, p in enumerate(pages):
            for (a, w), o in zip(slabs, outs):
                o[0, i * PAGE_SIZE:(i + 1) * PAGE_SIZE, :] = p[0, :, a:a + w].astype(o.dtype)

    @pl.when(g == n_groups)
    def _():
        new = new_ref[0]
        pad = jnp.zeros((KV_CHUNK - new.shape[0], new.shape[1]), f32)
        tail = jnp.concatenate([new, pad], axis=0)
        for (a, w), o in zip(slabs, outs):
            o[0] = tail[:, a:a + w].astype(o.dtype)


def _gather(pool, table, new, slabs, dtype):
    b, n_pages = table.shape
    c = pool.shape[2]
    n_groups = n_pages // PAGES_PER_STEP
    new_rows = new.shape[1]
    page_spec = lambda i: pl.BlockSpec(
        (1, PAGE_SIZE, c), lambda bi, g, pt: (pt[bi, jnp.minimum(PAGES_PER_STEP * g + i, n_pages - 1)], 0, 0))
    grid_spec = pltpu.PrefetchScalarGridSpec(
        num_scalar_prefetch=1, grid=(b, n_groups + 1),
        in_specs=[page_spec(i) for i in range(PAGES_PER_STEP)]
                 + [pl.BlockSpec((1, new_rows, c), lambda bi, g, pt: (bi, 0, 0))],
        out_specs=[pl.BlockSpec((1, KV_CHUNK, w), lambda bi, g, pt: (bi, g, 0)) for _, w in slabs])
    return pl.pallas_call(
        functools.partial(_gather_body, slabs, n_groups), grid_spec=grid_spec,
        out_shape=[jax.ShapeDtypeStruct((b, (n_groups + 1) * KV_CHUNK, w), dtype) for _, w in slabs],
        compiler_params=_cparams("parallel", "arbitrary"),
        name="page_gather",
    )(table, *([pool] * PAGES_PER_STEP), new)


def _cumsum_body(x_ref, u_ref, o_ref, carry_ref):
    @pl.when(pl.program_id(1) == 0)
    def _():
        carry_ref[...] = jnp.zeros_like(carry_ref)

    x = x_ref[0]
    hi = x.astype(bf16)
    r1 = x - hi.astype(f32)
    mid = r1.astype(bf16)
    lo = (r1 - mid.astype(f32)).astype(bf16)
    u = u_ref[...]
    cs = (_dot(hi, u) + _dot(mid, u)) + _dot(lo, u) + carry_ref[...]
    o_ref[0] = cs
    carry_ref[...] = cs[:, cs.shape[1] - 1:]


def _cumsum_lanes(x_t, tc):
    b, h, l = x_t.shape
    u = jnp.asarray(np.triu(np.ones((tc, tc), np.float32)), dtype=bf16)
    return pl.pallas_call(
        _cumsum_body,
        out_shape=jax.ShapeDtypeStruct((b, h, l), f32),
        grid=(b, l // tc),
        in_specs=[pl.BlockSpec((1, h, tc), lambda bi, c: (bi, 0, c)),
                  pl.BlockSpec((tc, tc), lambda bi, c: (0, 0))],
        out_specs=pl.BlockSpec((1, h, tc), lambda bi, c: (bi, 0, c)),
        scratch_shapes=[pltpu.VMEM((h, 1), f32)],
        compiler_params=_cparams("parallel", "arbitrary"),
        name="logf_cumsum",
    )(x_t, u)


def _stack_heads(q, qs_ref, n_slab, tq):
    lane = lax.broadcasted_iota(jnp.int32, (tq, LANES), 1)
    low = lane < HD
    for s in range(n_slab):
        slab = q[:, s * LANES:(s + 1) * LANES].astype(f32)
        qs_ref[(2 * s) * tq:(2 * s + 1) * tq, 0:LANES] = jnp.where(low, slab, 0.0).astype(bf16)
        qs_ref[(2 * s + 1) * tq:(2 * s + 2) * tq, 0:LANES] = jnp.where(low, 0.0, slab).astype(bf16)


def _unstack_heads(o, n_slab, tq):
    lane = lax.broadcasted_iota(jnp.int32, (tq, LANES), 1)
    low = lane < HD
    slabs = [jnp.where(low, o[(2 * s) * tq:(2 * s + 1) * tq], o[(2 * s + 1) * tq:(2 * s + 2) * tq])
             for s in range(n_slab)]
    return slabs[0] if n_slab == 1 else jnp.concatenate(slabs, axis=1)


def _flash_body(cfg, *refs):
    mode, n_slab, tq, tk, n_chunks = cfg["mode"], cfg["n_slab"], cfg["tq"], cfg["tk"], cfg["n_chunks"]
    qpos0, kpos0, window = cfg["qpos0"], cfg["kpos0"], cfg["window"]
    rows = 2 * n_slab * tq
    q_ref, k_ref, v_ref = refs[:3]
    extra = refs[3:-5]
    o_ref, qs_ref, m_ref, l_ref, acc_ref = refs[-5:]
    qi = pl.program_id(2)

    _stack_heads(q_ref[0], qs_ref, n_slab, tq)
    if mode == "slc":
        mb_ref = extra[0]
        for s in range(n_slab):
            for g in range(2):
                qs_ref[(2 * s + g) * tq:(2 * s + g + 1) * tq, LANES:2 * LANES] = mb_ref[0, g]
    m_ref[...] = jnp.full_like(m_ref, NEG_MASK)
    l_ref[...] = jnp.zeros_like(l_ref)
    acc_ref[...] = jnp.zeros_like(acc_ref)

    q_lo = qpos0 + qi * tq
    q_hi = q_lo + tq - 1
    c_hi = jnp.minimum(n_chunks, (q_hi - kpos0) // tk + 1)
    if window is None:
        c_lo = 0
        c_full = jnp.minimum(c_hi, (q_lo - kpos0 + 1) // tk)
    else:
        c_lo = jnp.maximum(q_lo - window + 1 - kpos0, 0) // tk
        c_full = c_lo
    qpos = q_lo + (lax.broadcasted_iota(jnp.int32, (rows, 1), 0) & (tq - 1))

    def chunk(c, masked):
        ks = pl.multiple_of(c * tk, tk)
        k = k_ref[0, pl.ds(ks, tk), :]
        if mode == "slc":
            blk = jnp.right_shift(kpos0 + ks + lax.broadcasted_iota(jnp.int32, (tk, LANES), 0), CMP_SHIFT)
            hot = jnp.where(blk == lax.broadcasted_iota(jnp.int32, (tk, LANES), 1), 1.0, 0.0).astype(bf16)
            k = jnp.concatenate([k, hot], axis=1)
        s = _dot_nt(qs_ref[...], k)
        if mode == "fox":
            fk_ref, fq_ref = extra
            fk = fk_ref[0, 0, c]
            s = s + jnp.concatenate([fq_ref[0, 0, 0] - fk[0:1], fq_ref[0, 0, 1] - fk[1:2]], axis=0)
        if masked:
            kpos = kpos0 + ks + lax.broadcasted_iota(jnp.int32, (1, tk), 1)
            mask = kpos <= qpos
            if window is not None:
                mask = jnp.logical_and(mask, (qpos - kpos) < window)
            s = jnp.where(mask, s, NEG_MASK)
        m_old = m_ref[...]
        m_new = jnp.maximum(m_old, jnp.max(s, axis=1, keepdims=True))
        alpha = jnp.exp(m_old - m_new)
        p = jnp.exp(s - m_new)
        if masked:
            p = jnp.where(mask, p, 0.0)
        l_ref[...] = alpha * l_ref[...] + jnp.sum(p, axis=1, keepdims=True)
        acc_ref[...] = alpha * acc_ref[...] + _dot(p.astype(bf16), v_ref[0, pl.ds(ks, tk), :])
        m_ref[...] = m_new

    def full_step(c, carry):
        chunk(c, False)
        return carry

    def masked_step(c, carry):
        chunk(c, True)
        return carry

    lax.fori_loop(c_lo, c_full, full_step, 0)
    lax.fori_loop(c_full, c_hi, masked_step, 0)

    o = acc_ref[...] / jnp.maximum(l_ref[...], 1e-30)
    if mode == "diff":
        lq_ref, sg_ref = extra
        lp = lq_ref[...]
        lam = (jnp.exp(jnp.sum(lp[0:1] * lp[1:2], axis=1, keepdims=True))
               - jnp.exp(jnp.sum(lp[2:3] * lp[3:4], axis=1, keepdims=True)) + cfg["lam_init"])
        d = o[0:tq] - lam * o[tq:2 * tq]
        d = d * lax.rsqrt(jnp.mean(d * d, axis=-1, keepdims=True) + EPS) * sg_ref[...]
        o_ref[0] = (d * (1.0 - cfg["lam_init"])).astype(o_ref.dtype)
    else:
        o_ref[0] = _unstack_heads(o, n_slab, tq).astype(o_ref.dtype)


def _flash(mode, q, k, v, extra, *, tq, tk, qpos0, kpos0, out_dtype, lam_init=0.0):
    b, t_q, cq = q.shape
    t_k = k.shape[1]
    n_slab = 4 if mode in ("slc", "win") else 1
    n_grid_slab = cq // (LANES * n_slab)
    n_chunks = t_k // tk
    rows = 2 * n_slab * tq
    ck = 2 * LANES if mode == "slc" else LANES
    cfg = dict(mode=mode, n_slab=n_slab, tq=tq, tk=tk, n_chunks=n_chunks, qpos0=qpos0, kpos0=kpos0,
               window=WINDOW if mode == "win" else None, lam_init=lam_init)
    qw = LANES * n_slab
    in_specs = [pl.BlockSpec((1, tq, qw), lambda bi, s, i: (bi, i, s)),
                pl.BlockSpec((1, t_k, LANES), lambda bi, s, i: (bi, 0, s)),
                pl.BlockSpec((1, t_k, LANES), lambda bi, s, i: (bi, 0, s))]
    if mode == "fox":
        in_specs += [pl.BlockSpec((1, 1, n_chunks, 2, tk), lambda bi, s, i: (bi, s, 0, 0, 0)),
                     pl.BlockSpec((1, 1, 2, tq, 1), lambda bi, s, i: (bi, s, 0, i, 0))]
    elif mode == "slc":
        in_specs += [pl.BlockSpec((1, 2, tq, LANES), lambda bi, s, i: (bi, 0, i, 0))]
    elif mode == "diff":
        in_specs += [pl.BlockSpec((4, HD), lambda bi, s, i: (0, 0)),
                     pl.BlockSpec((1, LANES), lambda bi, s, i: (0, 0))]
    return pl.pallas_call(
        functools.partial(_flash_body, cfg),
        out_shape=jax.ShapeDtypeStruct((b, t_q, cq), out_dtype),
        grid=(b, n_grid_slab, t_q // tq),
        in_specs=in_specs,
        out_specs=pl.BlockSpec((1, tq, qw), lambda bi, s, i: (bi, i, s)),
        scratch_shapes=[pltpu.VMEM((rows, ck), bf16), pltpu.VMEM((rows, 1), f32), pltpu.VMEM((rows, 1), f32),
                        pltpu.VMEM((rows, LANES), f32)],
        compiler_params=_cparams("parallel", "parallel", "parallel"),
        name="flash_" + mode,
    )(q, k, v, *extra)


def _cmp_body(tq, qpos0, nb_valid, q_ref, ck_ref, cv_ref, o_ref, mb_ref, qs_ref):
    n_slab = NSA_H // 2
    rows = 2 * n_slab * tq
    qi = pl.program_id(1)
    _stack_heads(q_ref[0], qs_ref, n_slab, tq)
    s = _dot_nt(qs_ref[...], ck_ref[0])
    t = qpos0 + qi * tq + (lax.broadcasted_iota(jnp.int32, (rows, 1), 0) & (tq - 1))
    n = lax.broadcasted_iota(jnp.int32, (1, LANES), 1)
    valid = jnp.logical_and((n + 1) * CMP_BLOCK - 1 <= t, n < nb_valid)
    sm = jnp.where(valid, s, NEG_MASK)
    m = jnp.max(sm, axis=1, keepdims=True)
    p = jnp.where(valid, jnp.exp(sm - m), 0.0)
    p = p / jnp.maximum(jnp.sum(p, axis=1, keepdims=True), 1e-30)
    o = _dot(p.astype(bf16), cv_ref[0])
    o_ref[0] = _unstack_heads(o, n_slab, tq)

    tg = t[0:tq]
    cur = tg // CMP_BLOCK
    lane_f = n.astype(f32)
    for g in range(NSA_KVH):
        imp = p[g * tq:(g + 1) * tq]
        for s_i in range(1, n_slab):
            imp = imp + p[(2 * s_i + g) * tq:(2 * s_i + g + 1) * tq]
        work = jnp.where(jnp.logical_and(n < cur, n < nb_valid), imp, -1.0)
        sel = n == cur
        for _ in range(N_SEL - 1):
            mx = jnp.max(work, axis=1, keepdims=True)
            first = jnp.min(jnp.where(work == mx, lane_f, float(LANES)), axis=1, keepdims=True)
            hit = lane_f == first
            sel = jnp.logical_or(sel, jnp.logical_and(hit, mx >= 0.0))
            work = jnp.where(hit, -2.0, work)
        mb_ref[0, g] = jnp.where(sel, 0.0, NEG_BLOCK).astype(bf16)


def _cmp_topk(q, ck, cv, *, tq, qpos0, nb_valid):
    b, t_q, cq = q.shape
    rows = NSA_H * tq
    return pl.pallas_call(
        functools.partial(_cmp_body, tq, qpos0, nb_valid),
        out_shape=[jax.ShapeDtypeStruct((b, t_q, cq), f32), jax.ShapeDtypeStruct((b, NSA_KVH, t_q, LANES), bf16)],
        grid=(b, t_q // tq),
        in_specs=[pl.BlockSpec((1, tq, cq), lambda bi, i: (bi, i, 0)),
                  pl.BlockSpec((1, LANES, LANES), lambda bi, i: (bi, 0, 0)),
                  pl.BlockSpec((1, LANES, LANES), lambda bi, i: (bi, 0, 0))],
        out_specs=[pl.BlockSpec((1, tq, cq), lambda bi, i: (bi, i, 0)),
                   pl.BlockSpec((1, NSA_KVH, tq, LANES), lambda bi, i: (bi, 0, i, 0))],
        scratch_shapes=[pltpu.VMEM((rows, LANES), bf16)],
        compiler_params=_cparams("parallel", "parallel"),
        name="nsa_cmp_topk",
    )(q, ck, cv)


SAMPLE_TQ = 16


def _pad_rows(a, n):
    return jnp.pad(a, ((0, 0), (0, n - a.shape[1])) + ((0, 0),) * (a.ndim - 2))


def _trunk(x, mods, qpos0, page_table, caches, wts, params):
    (b_f, w_pool, lam_qk, subln_g, norm_mix_g, norm_ffn_g, final_g) = params
    b, t, d = x.shape
    m = b * t
    decode = caches is not None
    depth = len(mods)
    tm = min(256, m)
    x2d = x.reshape(m, d)
    pos = qpos0 + np.tile(np.arange(t), b)
    tables = _rope_tables(pos)

    if decode:
        tiles_per_mod = 1
        tm_mlp, mlp_tiles_per_mod = tm, 1
        expand = lambda v: jnp.repeat(v, t, axis=0).reshape(m // tm, tm, d)
        tq_att = SAMPLE_TQ
        t_att = SAMPLE_TQ
    else:
        tiles_per_mod = t // tm
        tm_mlp = min(512, t)
        mlp_tiles_per_mod = t // tm_mlp
        expand = lambda v: v.reshape(b, 1, d)
        t_att = t

    as_att = lambda a: _pad_rows(a.reshape(b, t, a.shape[-1]), t_att)
    outs = {}
    for li in range(depth):
        sh1, sc1, gt1, sh2, sc2, gt2 = [expand(v) for v in jnp.split(mods[li], 6, axis=-1)]
        if li % 2 == 0:
            e = li // 2
            bias = jnp.pad(b_f[e].reshape(1, FOX_H), ((0, 0), (0, LANES - FOX_H)))
            po = _proj(x2d, norm_mix_g[li], sc1, sh1, tiles_per_mod, wts["in_even"][e], tables, bias,
                       _even_plan(), EVEN_OUT_DEFS, tm)
            logf = po[O_LOGF][:, :FOX_H]
            outs.setdefault("cmp", []).append(po[O_KVC].reshape(b, t, 2, NSA_KVH, HD))
            outs.setdefault("slc", []).append(po[O_KVS].reshape(b, t, 2, NSA_KVH, HD))
            outs.setdefault("fox", []).append(po[O_KVF].reshape(b, t, 2, FOX_H, HD))
            outs.setdefault("logf", []).append(logf.reshape(b, t, FOX_H))
            kvw_new = po[O_KVW].reshape(b, t, 2, NSA_KVH, HD)
            qn = as_att(po[O_QN])
            qf = as_att(po[O_QF])
            if not decode:
                outs.setdefault("win", []).append(kvw_new[:, t - min(WINDOW, t):])
                ident = jnp.arange(b * t // PAGE_SIZE, dtype=jnp.int32).reshape(b, t // PAGE_SIZE)
                cmp = _compress(po[O_KVC].reshape(b * t // PAGE_SIZE, PAGE_SIZE, 4 * HD), ident, w_pool[e])
                nb_valid = t // CMP_BLOCK
                ks, vs = po[O_KS].reshape(b, t, LANES), po[O_VS].reshape(b, t, LANES)
                kw, vw = po[O_KW].reshape(b, t, LANES), po[O_VW].reshape(b, t, LANES)
                kf, vf = po[O_KF].reshape(b, t, 4 * LANES), po[O_VF].reshape(b, t, 4 * LANES)
                logf_t = jnp.transpose(logf.reshape(b, t, FOX_H), (0, 2, 1))
                kpos0_w = 0
                tq_cmp, tq_nsa, tq_fox, tk = 128, 128, 512, 512
                tk_w = 256
            else:
                c_cmp, c_slc, c_win, c_fox, c_logf, _ = caches
                past = page_table.shape[1] * PAGE_SIZE
                w_buf = c_win.shape[2]
                outs.setdefault("win", []).append(
                    jnp.concatenate([c_win[e], kvw_new], axis=1)[:, t:])
                n_phys = c_cmp.shape[1]
                cmp = _compress(c_cmp[e].reshape(n_phys, PAGE_SIZE, 4 * HD), page_table, w_pool[e])
                nb_valid = past // CMP_BLOCK
                new8 = lambda a: _pad_rows(a.reshape(b, t, a.shape[-1]), 8)
                ks, vs = _gather(c_slc[e].reshape(n_phys, PAGE_SIZE, 4 * HD), page_table, new8(po[O_KVS]),
                                 [(0, LANES), (LANES, LANES)], bf16)
                win_table = jnp.arange(b * w_buf // PAGE_SIZE, dtype=jnp.int32).reshape(b, w_buf // PAGE_SIZE)
                kw, vw = _gather(c_win[e].reshape(b * w_buf // PAGE_SIZE, PAGE_SIZE, 4 * HD), win_table,
                                 new8(po[O_KVW]), [(0, LANES), (LANES, LANES)], bf16)
                kf, vf = _gather(c_fox[e].reshape(n_phys, PAGE_SIZE, 2 * FOX_H * HD), page_table,
                                 new8(po[O_KVF]), [(0, FOX_H * HD), (FOX_H * HD, FOX_H * HD)], bf16)
                (logf_all,) = _gather(c_logf[e], page_table, new8(logf), [(0, FOX_H)], f32)
                logf_t = jnp.transpose(logf_all, (0, 2, 1))
                kpos0_w = qpos0 - w_buf
                tq_cmp = tq_nsa = tq_fox = tq_att
                tk = tk_w = KV_CHUNK
            nb_pad = LANES
            cmp = _pad_rows(cmp, nb_pad)
            ck, cv = cmp[:, :, :LANES].astype(bf16), cmp[:, :, LANES:].astype(bf16)
            o_cmp, mb = _cmp_topk(qn, ck, cv, tq=tq_cmp, qpos0=qpos0, nb_valid=nb_valid)
            o_s = _flash("slc", qn, ks, vs, [mb], tq=tq_nsa, tk=tk, qpos0=qpos0, kpos0=0, out_dtype=f32)
            o_w = _flash("win", qn, kw, vw, [], tq=tq_nsa, tk=tk_w, qpos0=qpos0, kpos0=kpos0_w, out_dtype=f32)
            f_t = _cumsum_lanes(logf_t, tk)
            l_k = f_t.shape[2]
            fk = jnp.transpose(f_t.reshape(b, FOX_H // 2, 2, l_k // tk, tk), (0, 1, 3, 2, 4))
            fq = f_t[:, :, qpos0:qpos0 + t_att] if decode else f_t
            fq = fq.reshape(b, FOX_H // 2, 2, t_att, 1)
            o_f = _flash("fox", qf, kf, vf, [fk, fq], tq=tq_fox, tk=tk, qpos0=qpos0, kpos0=0, out_dtype=bf16)
            unpad = lambda a: a[:, :t].reshape(m, a.shape[-1])
            x2d = _outproj(_outproj_even_body, x2d, gt1, tiles_per_mod,
                           [unpad(o_cmp), unpad(o_s), unpad(o_w), po[O_GATE], unpad(o_f)],
                           wts["out_even"][e], tm, "outproj_even")
        else:
            o = li // 2
            po = _proj(x2d, norm_mix_g[li], sc1, sh1, tiles_per_mod, wts["in_odd"][o], tables,
                       jnp.zeros((1, LANES), f32), _odd_plan(), ODD_OUT_DEFS, tm)
            outs.setdefault("diff", []).append(po[O_KVD].reshape(b, t, 2, DIFF_H, 2 * HD))
            qd = as_att(po[O_QD])
            if not decode:
                kd, vd = po[O_KD].reshape(b, t, 8 * LANES), po[O_VD].reshape(b, t, 8 * LANES)
                tq_d, tk = 512, 512
            else:
                c_diff = caches[5]
                n_phys = c_diff.shape[1]
                new8 = lambda a: _pad_rows(a.reshape(b, t, a.shape[-1]), 8)
                kd, vd = _gather(c_diff[o].reshape(n_phys, PAGE_SIZE, 4 * DIFF_H * HD), page_table,
                                 new8(po[O_KVD]), [(0, 2 * DIFF_H * HD), (2 * DIFF_H * HD, 2 * DIFF_H * HD)], bf16)
                tq_d, tk = tq_att, KV_CHUNK
            lam_init = 0.8 - 0.6 * math.exp(-0.3 * li)
            o_d = _flash("diff", qd, kd, vd, [lam_qk[o], subln_g[o].reshape(1, 2 * HD)], tq=tq_d, tk=tk,
                         qpos0=qpos0, kpos0=0, out_dtype=bf16, lam_init=lam_init)
            x2d = _outproj(_outproj_odd_body, x2d, gt1, tiles_per_mod, [o_d[:, :t].reshape(m, d)],
                           wts["out_odd"][o], tm, "outproj_odd")
        x2d = _mlp(x2d, norm_ffn_g[li], sc2, sh2, gt2, mlp_tiles_per_mod, wts["ff_in"][li], wts["ff_out"][li],
                   final_g, li == depth - 1, tm_mlp, 1024)
    stack = lambda name: jnp.stack(outs[name], 0)
    return (x2d.reshape(b, t, d), stack("cmp"), stack("slc"), stack("win"), stack("fox"), stack("logf"),
            stack("diff"))


def kernel(x_prompt, x_sample, cache_nsa_cmp_kv, cache_nsa_slc_kv, cache_nsa_win_kv, cache_fox_kv,
           cache_fox_logf, cache_diff_kv, page_table, c_prompt, c_sample, w_in_even, b_f, w_pool,
           w_out_even, w_in_odd, lam_qk, subln_g, w_out_odd, norm_mix_g, norm_ffn_g, w_ada, b_ada,
           w_ff_in, w_ff_out, final_g):
    depth = w_ada.shape[0]
    nb_p, nb_s = c_prompt.shape[0], c_sample.shape[0]
    past_len = page_table.shape[1] * PAGE_SIZE

    perm_rows = lambda w: jnp.concatenate(
        [w[:NSA_H * HD].reshape(NSA_H, HD, -1)[NSA_PERM, :, :].reshape(NSA_H * HD, -1), w[NSA_H * HD:]], axis=0)
    wts = dict(
        in_even=[_even_weight(w_in_even[e]) for e in range(w_in_even.shape[0])],
        out_even=[perm_rows(w_out_even[e]).astype(bf16) for e in range(w_out_even.shape[0])],
        in_odd=[w_in_odd[o].astype(bf16) for o in range(w_in_odd.shape[0])],
        out_odd=[w_out_odd[o].astype(bf16) for o in range(w_out_odd.shape[0])],
        ff_in=[w_ff_in[l].astype(bf16) for l in range(depth)],
        ff_out=[w_ff_out[l].astype(bf16) for l in range(depth)],
    )
    rows = -(-(nb_p + nb_s) // 8) * 8
    c_all = jnp.pad(jnp.concatenate([c_prompt, c_sample], axis=0), ((0, rows - nb_p - nb_s), (0, 0)))
    mod_all = _ada(c_all, w_ada.astype(bf16), b_ada)
    mods_p = [mod_all[l, :nb_p] for l in range(depth)]
    mods_s = [mod_all[l, nb_p:nb_p + nb_s] for l in range(depth)]

    params = (b_f, w_pool, lam_qk, subln_g, norm_mix_g, norm_ffn_g, final_g)
    caches = (cache_nsa_cmp_kv, cache_nsa_slc_kv, cache_nsa_win_kv, cache_fox_kv, cache_fox_logf, cache_diff_kv)
    y_p, p_cmp, p_slc, p_win, p_fox, p_logf, p_diff = _trunk(x_prompt, mods_p, 0, None, None, wts, params)
    y_s, s_cmp, s_slc, s_win, s_fox, s_logf, s_diff = _trunk(x_sample, mods_s, past_len, page_table, caches,
                                                             wts, params)
    return (y_p, y_s, p_cmp, s_cmp, p_slc, s_slc, p_win, s_win, p_fox, s_fox, p_logf, s_logf, p_diff, s_diff)
```

```python
import functools
import math

import numpy as np
import jax
import jax.numpy as jnp
from jax import lax
from jax.experimental import pallas as pl
from jax.experimental.pallas import tpu as pltpu

HD = 64
ROPE_DIM = HD // 4
ROPE_HALF = ROPE_DIM // 2
ROPE_THETA = 500000.0
NSA_H = 8
NSA_KVH = 2
CMP_BLOCK = 64
CMP_SHIFT = 6
N_SEL = 16
WINDOW = 512
FOX_H = 8
DIFF_H = 8
EPS = 1e-6
PAGE_SIZE = 128

LANES = 128
SUBLANES = 8
VMEM_LIMIT_BYTES = 48 * 1024 * 1024

NEG_MASK = -1e30
NEG_BLOCK = -1e9
PAGES_PER_STEP = 4

NSA_PERM = (0, 4, 1, 5, 2, 6, 3, 7)

f32 = jnp.float32
bf16 = jnp.bfloat16


def _cparams(*sem):
    return pltpu.CompilerParams(dimension_semantics=sem, vmem_limit_bytes=VMEM_LIMIT_BYTES)


def _dot(a, b):
    return jnp.dot(a, b, preferred_element_type=f32)


def _dot_nt(a, b):
    return lax.dot_general(a, b, (((1,), (1,)), ((), ())), preferred_element_type=f32)


def _rms_mod(x, g, sc, sh):
    y = x * lax.rsqrt(jnp.mean(x * x, axis=-1, keepdims=True) + EPS)
    return (y * g) * (1.0 + sc) + sh


def _split_dot(x, w, parts):
    acc = None
    for _ in range(parts):
        piece = x.astype(bf16)
        term = _dot(piece, w)
        acc = term if acc is None else acc + term
        x = x - piece.astype(f32)
    return acc


def _ada_body(c_ref, w_ref, b_ref, o_ref):
    c = c_ref[...]
    a = (c * jax.nn.sigmoid(c)).astype(bf16)
    o_ref[0] = _dot(a, w_ref[0]) + b_ref[0]


def _ada(c_all, w_ada_b, b_ada):
    depth, d, n = w_ada_b.shape
    rows = c_all.shape[0]
    tn = n // 4
    return pl.pallas_call(
        _ada_body,
        out_shape=jax.ShapeDtypeStruct((depth, rows, n), f32),
        grid=(depth, n // tn),
        in_specs=[pl.BlockSpec((rows, d), lambda l, j: (0, 0)),
                  pl.BlockSpec((1, d, tn), lambda l, j: (l, 0, j)),
                  pl.BlockSpec((1, 1, tn), lambda l, j: (l, 0, j))],
        out_specs=pl.BlockSpec((1, rows, tn), lambda l, j: (l, 0, j)),
        compiler_params=_cparams("parallel", "parallel"),
        name="ada_mod",
    )(c_all, w_ada_b, b_ada.reshape(depth, 1, n))


def _rope_group(a, cos, sa, sb):
    return a * cos + pltpu.roll(a, LANES - ROPE_HALF, 1) * sa + pltpu.roll(a, ROPE_HALF, 1) * sb


def _proj_body(plan, x_ref, g_ref, sc_ref, sh_ref, w_ref, cos_ref, sa_ref, sb_ref, bias_ref, *out_refs):
    h = _rms_mod(x_ref[...], g_ref[...], sc_ref[0], sh_ref[0]).astype(bf16)
    cos, sa, sb = cos_ref[...], sa_ref[...], sb_ref[...]
    for src, groups in plan:
        width = LANES * len(groups)
        acc = _dot(h, w_ref[:, src:src + width])
        for gi, (kind, scale, dests) in enumerate(groups):
            a = acc[:, gi * LANES:(gi + 1) * LANES]
            if kind == "rope":
                a = _rope_group(a, cos, sa, sb)
            elif kind == "sigmoid":
                a = jax.nn.sigmoid(a)
            elif kind == "logsig":
                z = a + bias_ref[...]
                a = jnp.minimum(z, 0.0) - jnp.log1p(jnp.exp(-jnp.abs(z)))
            if scale != 1.0:
                a = a * scale
            for oi, col in dests:
                out_refs[oi][:, col:col + LANES] = a.astype(out_refs[oi].dtype)


def _proj(x2d, g, sc, sh, tiles_per_mod, w_b, tables, bias, plan, out_defs, tm):
    m, d = x2d.shape
    n = w_b.shape[1]
    cos, sa, sb = tables
    mod_rows = sc.shape[1]
    row = lambda i: (i, 0)
    const = lambda i: (0, 0)
    mod = lambda i: (i // tiles_per_mod, 0, 0)
    return pl.pallas_call(
        functools.partial(_proj_body, plan),
        out_shape=[jax.ShapeDtypeStruct((m, w), dt) for w, dt in out_defs],
        grid=(m // tm,),
        in_specs=[pl.BlockSpec((tm, d), row),
                  pl.BlockSpec((1, d), const),
                  pl.BlockSpec((1, mod_rows, d), mod),
                  pl.BlockSpec((1, mod_rows, d), mod),
                  pl.BlockSpec((d, n), const, pipeline_mode=pl.Buffered(1)),
                  pl.BlockSpec((tm, LANES), row),
                  pl.BlockSpec((tm, LANES), row),
                  pl.BlockSpec((tm, LANES), row),
                  pl.BlockSpec((1, LANES), const)],
        out_specs=[pl.BlockSpec((tm, w), row) for w, _ in out_defs],
        compiler_params=_cparams("parallel"),
        name="norm_mod_proj",
    )(x2d, g.reshape(1, d), sc, sh, w_b, cos, sa, sb, bias)


def _rope_tables(pos):
    inv = np.float32(ROPE_THETA) ** (-np.arange(ROPE_HALF, dtype=np.float32) / np.float32(ROPE_HALF))
    ang = (pos.astype(np.float32)[:, None] * inv[None, :]).astype(np.float64)
    n = pos.shape[0]
    cos = np.ones((n, HD), np.float64)
    sa = np.zeros((n, HD), np.float64)
    sb = np.zeros((n, HD), np.float64)
    cos[:, :ROPE_HALF] = np.cos(ang)
    cos[:, ROPE_HALF:ROPE_DIM] = np.cos(ang)
    sa[:, :ROPE_HALF] = -np.sin(ang)
    sb[:, ROPE_HALF:ROPE_DIM] = np.sin(ang)
    tile = lambda t: jnp.asarray(np.tile(t, (1, LANES // HD)).astype(np.float32))
    return tile(cos), tile(sa), tile(sb)


EV_QN, EV_KVC, EV_KVS, EV_KVW, EV_QF, EV_KVF, EV_GATE, EV_FF = 0, 512, 768, 1024, 1280, 1792, 2816, 4352


def _even_weight(w):
    d = w.shape[0]
    sizes = [NSA_H * HD, 2 * NSA_KVH * HD, 2 * NSA_KVH * HD, 2 * NSA_KVH * HD, 3 * NSA_H, FOX_H * HD,
             2 * FOX_H * HD, FOX_H]
    offs = np.concatenate([[0], np.cumsum(sizes)])
    q_n, kv_c, kv_s, kv_w, gates, q_f, kv_f, f_f = [w[:, offs[i]:offs[i + 1]] for i in range(8)]
    q_n = q_n.reshape(d, NSA_H, HD)[:, NSA_PERM, :].reshape(d, NSA_H * HD)
    gates = gates.reshape(d, NSA_H, 3)[:, NSA_PERM, :]
    gates = jnp.repeat(jnp.transpose(gates, (0, 2, 1))[..., None], HD, axis=-1).reshape(d, 3 * NSA_H * HD)
    f_f = jnp.pad(f_f, ((0, 0), (0, LANES - FOX_H)))
    return jnp.concatenate([q_n, kv_c, kv_s, kv_w, q_f, kv_f, gates, f_f], axis=1).astype(bf16)


(O_QN, O_KVC, O_KVS, O_KS, O_VS, O_KVW, O_KW, O_VW, O_QF, O_KVF, O_KF, O_VF, O_GATE, O_LOGF) = range(14)
EVEN_OUT_DEFS = [(512, bf16), (256, f32), (256, f32), (128, bf16), (128, bf16), (256, f32), (128, bf16),
                 (128, bf16), (512, bf16), (1024, f32), (512, bf16), (512, bf16), (1536, f32), (128, f32)]


def _even_plan():
    scale = HD ** -0.5
    plan = [(EV_QN, [("rope", scale, [(O_QN, LANES * i)]) for i in range(4)]),
            (EV_KVC, [("rope", 1.0, [(O_KVC, 0)]), ("plain", 1.0, [(O_KVC, LANES)])]),
            (EV_KVS, [("rope", 1.0, [(O_KVS, 0), (O_KS, 0)]), ("plain", 1.0, [(O_KVS, LANES), (O_VS, 0)])]),
            (EV_KVW, [("rope", 1.0, [(O_KVW, 0), (O_KW, 0)]), ("plain", 1.0, [(O_KVW, LANES), (O_VW, 0)])]),
            (EV_QF, [("plain", scale, [(O_QF, LANES * i)]) for i in range(4)]),
            (EV_KVF, [("plain", 1.0, [(O_KVF, LANES * i), (O_KF, LANES * i)]) for i in range(4)]),
            (EV_KVF + 512, [("plain", 1.0, [(O_KVF, 512 + LANES * i), (O_VF, LANES * i)]) for i in range(4)])]
    for j in range(3):
        plan.append((EV_GATE + 512 * j, [("sigmoid", 1.0, [(O_GATE, 512 * j + LANES * i)]) for i in range(4)]))
    plan.append((EV_FF, [("logsig", 1.0, [(O_LOGF, 0)])]))
    return plan


(O_QD, O_KVD, O_KD, O_VD) = range(4)
ODD_OUT_DEFS = [(1024, bf16), (2048, f32), (1024, bf16), (1024, bf16)]


def _odd_plan():
    scale = HD ** -0.5
    plan = []
    for c in range(2):
        plan.append((512 * c, [("rope", scale, [(O_QD, 512 * c + LANES * i)]) for i in range(4)]))
    for c in range(2):
        plan.append((1024 + 512 * c,
                     [("rope", 1.0, [(O_KVD, 512 * c + LANES * i), (O_KD, 512 * c + LANES * i)]) for i in range(4)]))
    for c in range(2):
        plan.append((2048 + 512 * c,
                     [("plain", 1.0, [(O_KVD, 1024 + 512 * c + LANES * i), (O_VD, 512 * c + LANES * i)])
                      for i in range(4)]))
    return plan


def _outproj_even_body(x_ref, gt_ref, oc_ref, os_ref, ow_ref, gates_ref, of_ref, w_ref, o_ref):
    nw = NSA_H * HD
    o_nsa = (gates_ref[:, 0:nw] * oc_ref[...] + gates_ref[:, nw:2 * nw] * os_ref[...]
             + gates_ref[:, 2 * nw:3 * nw] * ow_ref[...])
    a = jnp.concatenate([o_nsa.astype(bf16), of_ref[...]], axis=1)
    o_ref[...] = x_ref[...] + gt_ref[0] * _dot(a, w_ref[...])


def _outproj_odd_body(x_ref, gt_ref, od_ref, w_ref, o_ref):
    o_ref[...] = x_ref[...] + gt_ref[0] * _dot(od_ref[...], w_ref[...])


def _outproj(body, x2d, gt, tiles_per_mod, acts, w_b, tm, name):
    m, d = x2d.shape
    row = lambda i: (i, 0)
    mod = lambda i: (i // tiles_per_mod, 0, 0)
    return pl.pallas_call(
        body,
        out_shape=jax.ShapeDtypeStruct((m, d), f32),
        grid=(m // tm,),
        in_specs=[pl.BlockSpec((tm, d), row), pl.BlockSpec((1, gt.shape[1], d), mod)]
                 + [pl.BlockSpec((tm, a.shape[1]), row) for a in acts]
                 + [pl.BlockSpec(w_b.shape, lambda i: (0, 0), pipeline_mode=pl.Buffered(1))],
        out_specs=pl.BlockSpec((tm, d), row),
        compiler_params=_cparams("parallel"),
        name=name,
    )(x2d, gt, *acts, w_b)


def _mlp_body(final, x_ref, g_ref, sc_ref, sh_ref, gt_ref, w1_ref, w2_ref, fg_ref, o_ref, h_ref, acc_ref):
    f = pl.program_id(1)

    @pl.when(f == 0)
    def _():
        h_ref[...] = _rms_mod(x_ref[...], g_ref[...], sc_ref[0], sh_ref[0]).astype(bf16)
        acc_ref[...] = jnp.zeros_like(acc_ref)

    a = jnp.maximum(_dot(h_ref[...], w1_ref[...]), 0.0)
    acc_ref[...] += _dot((a * a).astype(bf16), w2_ref[...])

    @pl.when(f == pl.num_programs(1) - 1)
    def _():
        y = x_ref[...] + gt_ref[0] * acc_ref[...]
        if final:
            y = y * lax.rsqrt(jnp.mean(y * y, axis=-1, keepdims=True) + EPS) * fg_ref[...]
        o_ref[...] = y


def _mlp(x2d, g, sc, sh, gt, tiles_per_mod, w1_b, w2_b, final_g, final, tm, tf):
    m, d = x2d.shape
    dff = w1_b.shape[1]
    mod_rows = sc.shape[1]
    row = lambda i, f: (i, 0)
    const = lambda i, f: (0, 0)
    mod = lambda i, f: (i // tiles_per_mod, 0, 0)
    return pl.pallas_call(
        functools.partial(_mlp_body, final),
        out_shape=jax.ShapeDtypeStruct((m, d), f32),
        grid=(m // tm, dff // tf),
        in_specs=[pl.BlockSpec((tm, d), row), pl.BlockSpec((1, d), const),
                  pl.BlockSpec((1, mod_rows, d), mod), pl.BlockSpec((1, mod_rows, d), mod),
                  pl.BlockSpec((1, mod_rows, d), mod),
                  pl.BlockSpec((d, tf), lambda i, f: (0, f)), pl.BlockSpec((tf, d), lambda i, f: (f, 0)),
                  pl.BlockSpec((1, d), const)],
        out_specs=pl.BlockSpec((tm, d), row),
        scratch_shapes=[pltpu.VMEM((tm, d), bf16), pltpu.VMEM((tm, d), f32)],
        compiler_params=_cparams("parallel", "arbitrary"),
        name="relu2_mlp",
    )(x2d, g.reshape(1, d), sc, sh, gt, w1_b, w2_b, final_g.reshape(1, d))


def _compress_body(x_ref, wp_ref, o_ref):
    wp = wp_ref[...]
    rows = []
    for blk in range(x_ref.shape[1] // CMP_BLOCK):
        rows.append(jnp.sum(x_ref[0, blk * CMP_BLOCK:(blk + 1) * CMP_BLOCK, :] * wp, axis=0, keepdims=True))
    o_ref[0] = jnp.concatenate(rows, axis=0)


def _compress(kv, w_pool_e):
    b, t, c = kv.shape
    wp = jnp.repeat(jnp.transpose(w_pool_e, (1, 0, 2))[..., None], HD, axis=-1).reshape(CMP_BLOCK, c)
    rows_in = PAGES_PER_STEP * PAGE_SIZE
    rows_out = rows_in // CMP_BLOCK
    return pl.pallas_call(
        _compress_body,
        out_shape=jax.ShapeDtypeStruct((b, t // CMP_BLOCK, c), f32),
        grid=(b, t // rows_in),
        in_specs=[pl.BlockSpec((1, rows_in, c), lambda bi, g: (bi, g, 0)),
                  pl.BlockSpec((CMP_BLOCK, c), lambda bi, g: (0, 0))],
        out_specs=pl.BlockSpec((1, rows_out, c), lambda bi, g: (bi, g, 0)),
        compiler_params=_cparams("parallel", "parallel"),
        name="nsa_compress",
    )(kv, wp)


def _cumsum_body(x_ref, u_ref, o_ref, carry_ref):
    @pl.when(pl.program_id(1) == 0)
    def _():
        carry_ref[...] = jnp.zeros_like(carry_ref)

    cs = _split_dot(x_ref[0], u_ref[...], 3) + carry_ref[...]
    o_ref[0] = cs
    carry_ref[...] = cs[:, cs.shape[1] - 1:]


def _upper_ones(n):
    return jnp.asarray(np.triu(np.ones((n, n), np.float32)), dtype=bf16)


def _cumsum_lanes(x_t, tc):
    b, h, l = x_t.shape
    return pl.pallas_call(
        _cumsum_body,
        out_shape=jax.ShapeDtypeStruct((b, h, l), f32),
        grid=(b, l // tc),
        in_specs=[pl.BlockSpec((1, h, tc), lambda bi, c: (bi, 0, c)),
                  pl.BlockSpec((tc, tc), lambda bi, c: (0, 0))],
        out_specs=pl.BlockSpec((1, h, tc), lambda bi, c: (bi, 0, c)),
        scratch_shapes=[pltpu.VMEM((h, 1), f32)],
        compiler_params=_cparams("parallel", "arbitrary"),
        name="logf_cumsum",
    )(x_t, _upper_ones(tc))


def _stack_heads(q, qs_ref, n_slab, tq):
    lane = lax.broadcasted_iota(jnp.int32, (tq, LANES), 1)
    low = lane < HD
    for s in range(n_slab):
        slab = q[:, s * LANES:(s + 1) * LANES].astype(f32)
        qs_ref[(2 * s) * tq:(2 * s + 1) * tq, 0:LANES] = jnp.where(low, slab, 0.0).astype(qs_ref.dtype)
        qs_ref[(2 * s + 1) * tq:(2 * s + 2) * tq, 0:LANES] = jnp.where(low, 0.0, slab).astype(qs_ref.dtype)


def _unstack_heads(o, n_slab, tq):
    lane = lax.broadcasted_iota(jnp.int32, (tq, LANES), 1)
    low = lane < HD
    slabs = [jnp.where(low, o[(2 * s) * tq:(2 * s + 1) * tq], o[(2 * s + 1) * tq:(2 * s + 2) * tq])
             for s in range(n_slab)]
    return slabs[0] if n_slab == 1 else jnp.concatenate(slabs, axis=1)


def _masked_softmax(s, valid):
    sm = jnp.where(valid, s, NEG_MASK)
    m = jnp.max(sm, axis=1, keepdims=True)
    p = jnp.where(valid, jnp.exp(sm - m), 0.0)
    return p / jnp.maximum(jnp.sum(p, axis=1, keepdims=True), 1e-30)


def _select_blocks(p, tg, tq, nb_valid):
    n = lax.broadcasted_iota(jnp.int32, (1, LANES), 1)
    lane_f = n.astype(f32)
    cur = jnp.right_shift(tg, CMP_SHIFT)
    out = []
    for g in range(NSA_KVH):
        imp = p[g * tq:(g + 1) * tq]
        for s_i in range(1, NSA_H // 2):
            imp = imp + p[(2 * s_i + g) * tq:(2 * s_i + g + 1) * tq]
        work = jnp.where(jnp.logical_and(n < cur, n < nb_valid), imp, -1.0)
        sel = n == cur
        for _ in range(N_SEL - 1):
            mx = jnp.max(work, axis=1, keepdims=True)
            first = jnp.min(jnp.where(work == mx, lane_f, float(LANES)), axis=1, keepdims=True)
            hit = lane_f == first
            sel = jnp.logical_or(sel, jnp.logical_and(hit, mx >= 0.0))
            work = jnp.where(hit, -2.0, work)
        out.append(jnp.where(sel, 0.0, NEG_BLOCK))
    return out


def _flash_body(cfg, *refs):
    mode, n_slab, tq, tk, n_chunks = cfg["mode"], cfg["n_slab"], cfg["tq"], cfg["tk"], cfg["n_chunks"]
    window = cfg["window"]
    rows = 2 * n_slab * tq
    q_ref, k_ref, v_ref = refs[:3]
    extra = refs[3:-5]
    o_ref, qs_ref, m_ref, l_ref, acc_ref = refs[-5:]
    qi = pl.program_id(2)

    _stack_heads(q_ref[0], qs_ref, n_slab, tq)
    if mode == "slc":
        mb_ref = extra[0]
        for s in range(n_slab):
            for g in range(2):
                qs_ref[(2 * s + g) * tq:(2 * s + g + 1) * tq, LANES:2 * LANES] = mb_ref[0, g]
    m_ref[...] = jnp.full_like(m_ref, NEG_MASK)
    l_ref[...] = jnp.zeros_like(l_ref)
    acc_ref[...] = jnp.zeros_like(acc_ref)

    q_lo = qi * tq
    q_hi = q_lo + tq - 1
    c_hi = jnp.minimum(n_chunks, q_hi // tk + 1)
    if window is None:
        c_lo = 0
        c_full = jnp.minimum(c_hi, (q_lo + 1) // tk)
    else:
        c_lo = jnp.maximum(q_lo - window + 1, 0) // tk
        c_full = c_lo
    qpos = q_lo + (lax.broadcasted_iota(jnp.int32, (rows, 1), 0) & (tq - 1))

    def chunk(c, masked):
        ks = pl.multiple_of(c * tk, tk)
        k = k_ref[0, pl.ds(ks, tk), :]
        if mode == "slc":
            blk = jnp.right_shift(ks + lax.broadcasted_iota(jnp.int32, (tk, LANES), 0), CMP_SHIFT)
            hot = jnp.where(blk == lax.broadcasted_iota(jnp.int32, (tk, LANES), 1), 1.0, 0.0).astype(bf16)
            k = jnp.concatenate([k, hot], axis=1)
        s = _dot_nt(qs_ref[...], k)
        if mode == "fox":
            fk_ref, fq_ref = extra
            fk = fk_ref[0, 0, c]
            s = s + jnp.concatenate([fq_ref[0, 0, 0] - fk[0:1], fq_ref[0, 0, 1] - fk[1:2]], axis=0)
        if masked:
            kpos = ks + lax.broadcasted_iota(jnp.int32, (1, tk), 1)
            mask = kpos <= qpos
            if window is not None:
                mask = jnp.logical_and(mask, (qpos - kpos) < window)
            s = jnp.where(mask, s, NEG_MASK)
        m_old = m_ref[...]
        m_new = jnp.maximum(m_old, jnp.max(s, axis=1, keepdims=True))
        alpha = jnp.exp(m_old - m_new)
        p = jnp.exp(s - m_new)
        if masked:
            p = jnp.where(mask, p, 0.0)
        l_ref[...] = alpha * l_ref[...] + jnp.sum(p, axis=1, keepdims=True)
        acc_ref[...] = alpha * acc_ref[...] + _dot(p.astype(bf16), v_ref[0, pl.ds(ks, tk), :])
        m_ref[...] = m_new

    def full_step(c, carry):
        chunk(c, False)
        return carry

    def masked_step(c, carry):
        chunk(c, True)
        return carry

    lax.fori_loop(c_lo, c_full, full_step, 0)
    lax.fori_loop(c_full, c_hi, masked_step, 0)

    o = acc_ref[...] / jnp.maximum(l_ref[...], 1e-30)
    if mode == "diff":
        lq_ref, sg_ref = extra
        o_ref[0] = _diff_combine(o[0:tq], o[tq:2 * tq], lq_ref[...], sg_ref[...], cfg["lam_init"]).astype(o_ref.dtype)
    else:
        o_ref[0] = _unstack_heads(o, n_slab, tq).astype(o_ref.dtype)


def _diff_combine(o1, o2, lp, sub_g, lam_init):
    lam = (jnp.exp(jnp.sum(lp[0:1] * lp[1:2], axis=1, keepdims=True))
           - jnp.exp(jnp.sum(lp[2:3] * lp[3:4], axis=1, keepdims=True)) + lam_init)
    d = o1 - lam * o2
    d = d * lax.rsqrt(jnp.mean(d * d, axis=-1, keepdims=True) + EPS) * sub_g
    return d * (1.0 - lam_init)


def _flash(mode, q, k, v, extra, *, tq, tk, out_dtype, lam_init=0.0):
    b, t_q, cq = q.shape
    t_k = k.shape[1]
    n_slab = 4 if mode in ("slc", "win") else 1
    n_grid_slab = cq // (LANES * n_slab)
    n_chunks = t_k // tk
    rows = 2 * n_slab * tq
    ck = 2 * LANES if mode == "slc" else LANES
    cfg = dict(mode=mode, n_slab=n_slab, tq=tq, tk=tk, n_chunks=n_chunks,
               window=WINDOW if mode == "win" else None, lam_init=lam_init)
    qw = LANES * n_slab
    in_specs = [pl.BlockSpec((1, tq, qw), lambda bi, s, i: (bi, i, s)),
                pl.BlockSpec((1, t_k, LANES), lambda bi, s, i: (bi, 0, s)),
                pl.BlockSpec((1, t_k, LANES), lambda bi, s, i: (bi, 0, s))]
    if mode == "fox":
        in_specs += [pl.BlockSpec((1, 1, n_chunks, 2, tk), lambda bi, s, i: (bi, s, 0, 0, 0)),
                     pl.BlockSpec((1, 1, 2, tq, 1), lambda bi, s, i: (bi, s, 0, i, 0))]
    elif mode == "slc":
        in_specs += [pl.BlockSpec((1, 2, tq, LANES), lambda bi, s, i: (bi, 0, i, 0))]
    elif mode == "diff":
        in_specs += [pl.BlockSpec((4, HD), lambda bi, s, i: (0, 0)),
                     pl.BlockSpec((1, LANES), lambda bi, s, i: (0, 0))]
    return pl.pallas_call(
        functools.partial(_flash_body, cfg),
        out_shape=jax.ShapeDtypeStruct((b, t_q, cq), out_dtype),
        grid=(b, n_grid_slab, t_q // tq),
        in_specs=in_specs,
        out_specs=pl.BlockSpec((1, tq, qw), lambda bi, s, i: (bi, i, s)),
        scratch_shapes=[pltpu.VMEM((rows, ck), bf16), pltpu.VMEM((rows, 1), f32), pltpu.VMEM((rows, 1), f32),
                        pltpu.VMEM((rows, LANES), f32)],
        compiler_params=_cparams("parallel", "parallel", "parallel"),
        name="flash_" + mode,
    )(q, k, v, *extra)


def _cmp_body(tq, nb_valid, q_ref, ck_ref, cv_ref, o_ref, mb_ref, qs_ref):
    n_slab = NSA_H // 2
    rows = 2 * n_slab * tq
    qi = pl.program_id(1)
    _stack_heads(q_ref[0], qs_ref, n_slab, tq)
    s = _dot_nt(qs_ref[...], ck_ref[0])
    t = qi * tq + (lax.broadcasted_iota(jnp.int32, (rows, 1), 0) & (tq - 1))
    n = lax.broadcasted_iota(jnp.int32, (1, LANES), 1)
    valid = jnp.logical_and((n + 1) * CMP_BLOCK - 1 <= t, n < nb_valid)
    p = _masked_softmax(s, valid)
    o_ref[0] = _unstack_heads(_dot(p.astype(bf16), cv_ref[0]), n_slab, tq)
    bias = _select_blocks(p, t[0:tq], tq, nb_valid)
    for g in range(NSA_KVH):
        mb_ref[0, g] = bias[g].astype(bf16)


def _cmp_topk(q, ck, cv, *, tq, nb_valid):
    b, t_q, cq = q.shape
    rows = NSA_H * tq
    return pl.pallas_call(
        functools.partial(_cmp_body, tq, nb_valid),
        out_shape=[jax.ShapeDtypeStruct((b, t_q, cq), f32), jax.ShapeDtypeStruct((b, NSA_KVH, t_q, LANES), bf16)],
        grid=(b, t_q // tq),
        in_specs=[pl.BlockSpec((1, tq, cq), lambda bi, i: (bi, i, 0)),
                  pl.BlockSpec((1, LANES, LANES), lambda bi, i: (bi, 0, 0)),
                  pl.BlockSpec((1, LANES, LANES), lambda bi, i: (bi, 0, 0))],
        out_specs=[pl.BlockSpec((1, tq, cq), lambda bi, i: (bi, i, 0)),
                   pl.BlockSpec((1, NSA_KVH, tq, LANES), lambda bi, i: (bi, 0, i, 0))],
        scratch_shapes=[pltpu.VMEM((rows, LANES), bf16)],
        compiler_params=_cparams("parallel", "parallel"),
        name="nsa_cmp_topk",
    )(q, ck, cv)


DEC_ROWS = SUBLANES


def _pad_keys(new, lo, width):
    x = new[:, lo:lo + width]
    return jnp.concatenate([x, jnp.zeros((LANES - x.shape[0], width), f32)], axis=0).astype(bf16)


def _new_key_mask(rows, n_new):
    qi = lax.broadcasted_iota(jnp.int32, (rows, 1), 0) & (DEC_ROWS - 1)
    kj = lax.broadcasted_iota(jnp.int32, (1, LANES), 1)
    return jnp.logical_and(kj <= qi, kj < n_new)


def _softmax2(s1, s2, mask2):
    s2 = jnp.where(mask2, s2, NEG_MASK)
    m = jnp.maximum(jnp.max(s1, axis=1, keepdims=True), jnp.max(s2, axis=1, keepdims=True))
    p1 = jnp.exp(s1 - m)
    p2 = jnp.where(mask2, jnp.exp(s2 - m), 0.0)
    den = jnp.sum(p1, axis=1, keepdims=True) + jnp.sum(p2, axis=1, keepdims=True)
    return p1, p2, jnp.maximum(den, 1e-30)


def _dec_nsa_body(cfg, pt_ref, *refs):
    n_pages, qpos0, n_new, w_buf = cfg["n_pages"], cfg["qpos0"], cfg["n_new"], cfg["w_buf"]
    cmp_pages = refs[:n_pages]
    slc_pages = refs[n_pages:2 * n_pages]
    (q_ref, win_ref, news_ref, neww_ref, wp_ref, e_ref, oht_ref,
     oc_ref, os_ref, ow_ref, qs_ref) = refs[2 * n_pages:]
    tq, n_slab = DEC_ROWS, NSA_H // 2
    rows = NSA_H * tq
    gd = NSA_KVH * HD

    _stack_heads(q_ref[0], qs_ref, n_slab, tq)
    qs = qs_ref[...].astype(bf16)
    t = qpos0 + (lax.broadcasted_iota(jnp.int32, (rows, 1), 0) & (tq - 1))
    n = lax.broadcasted_iota(jnp.int32, (1, LANES), 1)
    mask_new = _new_key_mask(rows, n_new)

    def compressed(kv):
        xt = jnp.concatenate([p[0, kv].reshape(gd, PAGE_SIZE) for p in cmp_pages], axis=1)
        w = jnp.concatenate([wp_ref[kv]] * n_pages, axis=1)
        return _split_dot(xt * w, e_ref[...], 2).astype(bf16)

    ck_t, cv_t = compressed(0), compressed(1)
    nb_valid = n_pages * PAGE_SIZE // CMP_BLOCK
    valid = jnp.logical_and((n + 1) * CMP_BLOCK - 1 <= t, n < nb_valid)
    p = _masked_softmax(_dot(qs, ck_t), valid)
    oc_ref[0] = _unstack_heads(_dot_nt(p.astype(bf16), cv_t), n_slab, tq)

    bias = _select_blocks(p, t[0:tq], tq, nb_valid)
    qq = jnp.concatenate([qs, jnp.concatenate(bias * n_slab, axis=0).astype(bf16)], axis=1)
    ks_t = jnp.concatenate([p_[0, 0].reshape(gd, PAGE_SIZE) for p_ in slc_pages], axis=1).astype(bf16)
    vs_t = jnp.concatenate([p_[0, 1].reshape(gd, PAGE_SIZE) for p_ in slc_pages], axis=1).astype(bf16)
    s1 = _dot(qq, jnp.concatenate([ks_t, oht_ref[...]], axis=0))
    new_s = news_ref[0]
    s2 = _dot_nt(qs, _pad_keys(new_s, 0, gd))
    p1, p2, den = _softmax2(s1, s2, mask_new)
    o = _dot_nt(p1.astype(bf16), vs_t) + _dot(p2.astype(bf16), _pad_keys(new_s, gd, gd))
    os_ref[0] = _unstack_heads(o / den, n_slab, tq)

    kw_t = win_ref[0, 0].reshape(gd, w_buf).astype(bf16)
    vw_t = win_ref[0, 1].reshape(gd, w_buf).astype(bf16)
    kpos = qpos0 - w_buf + lax.broadcasted_iota(jnp.int32, (1, w_buf), 1)
    s1 = jnp.where((t - kpos) < WINDOW, _dot(qs, kw_t), NEG_MASK)
    new_w = neww_ref[0]
    s2 = _dot_nt(qs, _pad_keys(new_w, 0, gd))
    p1, p2, den = _softmax2(s1, s2, mask_new)
    o = _dot_nt(p1.astype(bf16), vw_t) + _dot(p2.astype(bf16), _pad_keys(new_w, gd, gd))
    ow_ref[0] = _unstack_heads(o / den, n_slab, tq)


def _dec_nsa(page_table, qn, cmp_pool, slc_pool, win_t, new_s, new_w, w_pool_e, *, qpos0, n_new):
    b, n_pages = page_table.shape
    w_buf = win_t.shape[-1]
    l_k = n_pages * PAGE_SIZE
    gd = NSA_KVH * HD
    cfg = dict(n_pages=n_pages, qpos0=qpos0, n_new=n_new, w_buf=w_buf)
    wp = jnp.tile(jnp.repeat(jnp.transpose(w_pool_e, (0, 2, 1)), HD, axis=1), (1, 1, PAGE_SIZE // CMP_BLOCK))
    blk_of = np.arange(l_k) // CMP_BLOCK
    e_all = jnp.asarray(blk_of[:, None] == np.arange(LANES)[None, :], dtype=bf16)
    oh_t = jnp.asarray(np.arange(LANES)[:, None] == blk_of[None, :], dtype=bf16)
    page = lambda i: pl.BlockSpec((1, 2, NSA_KVH, HD, PAGE_SIZE), lambda bi, pt: (pt[bi, i], 0, 0, 0, 0))
    per_b = lambda shape: pl.BlockSpec((1,) + shape, lambda bi, pt: (bi,) + (0,) * len(shape))
    const = lambda shape: pl.BlockSpec(shape, lambda bi, pt: (0,) * len(shape))
    grid_spec = pltpu.PrefetchScalarGridSpec(
        num_scalar_prefetch=1, grid=(b,),
        in_specs=[page(i) for i in range(n_pages)] * 2
                 + [per_b((DEC_ROWS, NSA_H * HD)), per_b((2, NSA_KVH, HD, w_buf)), per_b((DEC_ROWS, 2 * gd)),
                    per_b((DEC_ROWS, 2 * gd)), const((2, gd, PAGE_SIZE)), const((l_k, LANES)), const((LANES, l_k))],
        out_specs=[per_b((DEC_ROWS, NSA_H * HD))] * 3,
        scratch_shapes=[pltpu.VMEM((NSA_H * DEC_ROWS, LANES), f32)])
    return pl.pallas_call(
        functools.partial(_dec_nsa_body, cfg), grid_spec=grid_spec,
        out_shape=[jax.ShapeDtypeStruct((b, DEC_ROWS, NSA_H * HD), f32)] * 3,
        compiler_params=_cparams("parallel"),
        name="decode_nsa",
    )(page_table, *([cmp_pool] * n_pages), *([slc_pool] * n_pages), qn, win_t, new_s, new_w, wp, e_all, oh_t)


def _dec_fox_body(cfg, pt_ref, *refs):
    n_pages, n_new = cfg["n_pages"], cfg["n_new"]
    kv_pages = refs[:n_pages]
    lf_pages = refs[n_pages:2 * n_pages]
    qbd_ref, new_ref, lfn_ref, u_ref, o_ref = refs[2 * n_pages:]
    hd_all = FOX_H * HD
    rows = FOX_H * n_new
    qbd = qbd_ref[0]
    u = u_ref[...]

    carry = jnp.zeros((FOX_H, 1), f32)
    f_pages = []
    for p in lf_pages:
        c = _split_dot(p[0], u, 3) + carry
        f_pages.append(c)
        carry = c[:, PAGE_SIZE - 1:]
    f_k = jnp.concatenate(f_pages, axis=1)
    f_new = _split_dot(lfn_ref[0], u, 3)

    k_t = jnp.concatenate([p[0, 0].reshape(hd_all, PAGE_SIZE) for p in kv_pages], axis=1).astype(bf16)
    v_t = jnp.concatenate([p[0, 1].reshape(hd_all, PAGE_SIZE) for p in kv_pages], axis=1).astype(bf16)
    new = new_ref[0]
    s1 = _dot(qbd, k_t)
    s2 = _dot_nt(qbd, _pad_keys(new, 0, hd_all))
    kj = lax.broadcasted_iota(jnp.int32, (1, LANES), 1)
    s1_q, s2_q, m2_q = [], [], []
    for qi in range(n_new):
        f_q = carry + f_new[:, qi:qi + 1]
        s1_q.append(s1[qi * FOX_H:(qi + 1) * FOX_H] + f_q - f_k)
        s2_q.append(s2[qi * FOX_H:(qi + 1) * FOX_H] + f_q - (carry + f_new))
        m2_q.append(jnp.broadcast_to(kj <= qi, (FOX_H, LANES)))
    p1, p2, den = _softmax2(jnp.concatenate(s1_q, axis=0), jnp.concatenate(s2_q, axis=0),
                            jnp.concatenate(m2_q, axis=0))
    o = (_dot_nt(p1.astype(bf16), v_t) + _dot(p2.astype(bf16), _pad_keys(new, hd_all, hd_all))) / den
    head = lax.broadcasted_iota(jnp.int32, (FOX_H, hd_all), 0)
    own = jnp.right_shift(lax.broadcasted_iota(jnp.int32, (FOX_H, hd_all), 1), CMP_SHIFT) == head
    out = [jnp.sum(jnp.where(own, o[qi * FOX_H:(qi + 1) * FOX_H], 0.0), axis=0, keepdims=True)
           for qi in range(n_new)]
    out.append(jnp.zeros((DEC_ROWS - n_new, hd_all), f32))
    o_ref[0] = jnp.concatenate(out, axis=0).astype(o_ref.dtype)


def _dec_fox(page_table, qbd, kv_pool, lf_pool, new_kv, lf_new, *, n_new):
    b, n_pages = page_table.shape
    hd_all = FOX_H * HD
    cfg = dict(n_pages=n_pages, n_new=n_new)
    kv_page = lambda i: pl.BlockSpec((1, 2, FOX_H, HD, PAGE_SIZE), lambda bi, pt: (pt[bi, i], 0, 0, 0, 0))
    lf_page = lambda i: pl.BlockSpec((1, FOX_H, PAGE_SIZE), lambda bi, pt: (pt[bi, i], 0, 0))
    per_b = lambda shape: pl.BlockSpec((1,) + shape, lambda bi, pt: (bi,) + (0,) * len(shape))
    grid_spec = pltpu.PrefetchScalarGridSpec(
        num_scalar_prefetch=1, grid=(b,),
        in_specs=[kv_page(i) for i in range(n_pages)] + [lf_page(i) for i in range(n_pages)]
                 + [per_b((FOX_H * n_new, hd_all)), per_b((DEC_ROWS, 2 * hd_all)), per_b((FOX_H, LANES)),
                    pl.BlockSpec((PAGE_SIZE, PAGE_SIZE), lambda bi, pt: (0, 0))],
        out_specs=per_b((DEC_ROWS, hd_all)))
    return pl.pallas_call(
        functools.partial(_dec_fox_body, cfg), grid_spec=grid_spec,
        out_shape=jax.ShapeDtypeStruct((b, DEC_ROWS, hd_all), bf16),
        compiler_params=_cparams("parallel"),
        name="decode_fox",
    )(page_table, *([kv_pool] * n_pages), *([lf_pool] * n_pages), qbd, new_kv, lf_new, _upper_ones(PAGE_SIZE))


def _dec_diff_body(cfg, pt_ref, *refs):
    n_pages, n_new, lam_init = cfg["n_pages"], cfg["n_new"], cfg["lam_init"]
    pages = refs[:n_pages]
    q_ref, knew_ref, vnew_ref, lq_ref, sg_ref, o_ref, qs_ref = refs[n_pages:]
    tq = DEC_ROWS
    rows = 2 * DIFF_H * tq
    cols = PAGE_SIZE * DIFF_H
    _stack_heads(q_ref[0], qs_ref, DIFF_H, tq)
    qs = qs_ref[...].astype(bf16)
    row = lax.broadcasted_iota(jnp.int32, (rows, 1), 0)
    row_head = jnp.right_shift(row, 4)
    col = lax.broadcasted_iota(jnp.int32, (1, cols), 1)
    own = (col & (DIFF_H - 1)) == row_head

    def update(state, s, mask, v_rows):
        m_old, l_old, acc = state
        s = jnp.where(mask, s, NEG_MASK)
        m_new = jnp.maximum(m_old, jnp.max(s, axis=1, keepdims=True))
        alpha = jnp.exp(m_old - m_new)
        p = jnp.where(mask, jnp.exp(s - m_new), 0.0)
        return (m_new, alpha * l_old + jnp.sum(p, axis=1, keepdims=True),
                alpha * acc + _dot(p.astype(bf16), v_rows))

    state = (jnp.full((rows, 1), NEG_MASK, f32), jnp.zeros((rows, 1), f32), jnp.zeros((rows, 2 * HD), f32))
    for p in pages:
        k_rows = p[0, :, 0].reshape(cols, 2 * HD).astype(bf16)
        v_rows = p[0, :, 1].reshape(cols, 2 * HD).astype(bf16)
        state = update(state, _dot_nt(qs, k_rows), own, v_rows)
    pad = jnp.zeros((LANES - knew_ref.shape[1], 2 * HD), f32)
    k_new = jnp.concatenate([knew_ref[0], pad], axis=0).astype(bf16)
    v_new = jnp.concatenate([vnew_ref[0], pad], axis=0).astype(bf16)
    ncol = lax.broadcasted_iota(jnp.int32, (1, LANES), 1)
    mask_new = jnp.logical_and(jnp.logical_and((ncol & (DIFF_H - 1)) == row_head,
                                               jnp.right_shift(ncol, 3) <= (row & (tq - 1))),
                               ncol < n_new * DIFF_H)
    _, l_fin, acc = update(state, _dot_nt(qs, k_new), mask_new, v_new)
    o = acc / jnp.maximum(l_fin, 1e-30)
    heads = [_diff_combine(o[(2 * h) * tq:(2 * h + 1) * tq], o[(2 * h + 1) * tq:(2 * h + 2) * tq],
                           lq_ref[...], sg_ref[...], lam_init) for h in range(DIFF_H)]
    o_ref[0] = jnp.concatenate(heads, axis=1).astype(o_ref.dtype)


def _dec_diff(page_table, qd, pool, k_new, v_new, lam_qk_o, sub_g, *, n_new, lam_init):
    b, n_pages = page_table.shape
    cfg = dict(n_pages=n_pages, n_new=n_new, lam_init=lam_init)
    page = lambda i: pl.BlockSpec((1, PAGE_SIZE, 2, DIFF_H, 2 * HD), lambda bi, pt: (pt[bi, i], 0, 0, 0, 0))
    per_b = lambda shape: pl.BlockSpec((1,) + shape, lambda bi, pt: (bi,) + (0,) * len(shape))
    const = lambda shape: pl.BlockSpec(shape, lambda bi, pt: (0,) * len(shape))
    width = 2 * DIFF_H * HD
    grid_spec = pltpu.PrefetchScalarGridSpec(
        num_scalar_prefetch=1, grid=(b,),
        in_specs=[page(i) for i in range(n_pages)]
                 + [per_b((DEC_ROWS, width)), per_b((DIFF_H * n_new, 2 * HD)), per_b((DIFF_H * n_new, 2 * HD)),
                    const((4, HD)), const((1, 2 * HD))],
        out_specs=per_b((DEC_ROWS, width)),
        scratch_shapes=[pltpu.VMEM((2 * DIFF_H * DEC_ROWS, LANES), f32)])
    return pl.pallas_call(
        functools.partial(_dec_diff_body, cfg), grid_spec=grid_spec,
        out_shape=jax.ShapeDtypeStruct((b, DEC_ROWS, width), bf16),
        compiler_params=_cparams("parallel"),
        name="decode_diff",
    )(page_table, *([pool] * n_pages), qd, k_new, v_new, lam_qk_o, sub_g)


def _pad_rows(a, n):
    return jnp.pad(a, ((0, 0), (0, n - a.shape[1])) + ((0, 0),) * (a.ndim - 2))


def _trunk(x, mods, qpos0, page_table, caches, wts, params):
    (b_f, w_pool, lam_qk, subln_g, norm_mix_g, norm_ffn_g, final_g) = params
    b, t, d = x.shape
    m = b * t
    decode = caches is not None
    depth = len(mods)
    tm = min(256, m)
    x2d = x.reshape(m, d)
    pos = qpos0 + np.tile(np.arange(t), b)
    tables = _rope_tables(pos)

    if decode:
        tiles_per_mod = 1
        tm_mlp, mlp_tiles_per_mod = tm, 1
        expand = lambda v: jnp.repeat(v, t, axis=0).reshape(m // tm, tm, d)
    else:
        tiles_per_mod = t // tm
        tm_mlp = min(512, t)
        mlp_tiles_per_mod = t // tm_mlp
        expand = lambda v: v.reshape(b, 1, d)

    per_seq = lambda a, rows=DEC_ROWS: _pad_rows(a.reshape(b, t, a.shape[-1]), rows)
    outs = {}
    for li in range(depth):
        sh1, sc1, gt1, sh2, sc2, gt2 = [expand(v) for v in jnp.split(mods[li], 6, axis=-1)]
        if li % 2 == 0:
            e = li // 2
            bias = jnp.pad(b_f[e].reshape(1, FOX_H), ((0, 0), (0, LANES - FOX_H)))
            po = _proj(x2d, norm_mix_g[li], sc1, sh1, tiles_per_mod, wts["in_even"][e], tables, bias,
                       _even_plan(), EVEN_OUT_DEFS, tm)
            logf = po[O_LOGF][:, :FOX_H].reshape(b, t, FOX_H)
            outs.setdefault("cmp", []).append(po[O_KVC].reshape(b, t, 2, NSA_KVH, HD))
            outs.setdefault("slc", []).append(po[O_KVS].reshape(b, t, 2, NSA_KVH, HD))
            outs.setdefault("fox", []).append(po[O_KVF].reshape(b, t, 2, FOX_H, HD))
            outs.setdefault("logf", []).append(logf)
            kvw_new = po[O_KVW].reshape(b, t, 2, NSA_KVH, HD)
            if not decode:
                outs.setdefault("win", []).append(kvw_new[:, t - min(WINDOW, t):])
                qn, qf = po[O_QN].reshape(b, t, NSA_H * HD), po[O_QF].reshape(b, t, FOX_H * HD)
                cmp = _pad_rows(_compress(po[O_KVC].reshape(b, t, 4 * HD), w_pool[e]), LANES)
                ck, cv = cmp[:, :, :LANES].astype(bf16), cmp[:, :, LANES:].astype(bf16)
                o_cmp, mb = _cmp_topk(qn, ck, cv, tq=min(128, t), nb_valid=t // CMP_BLOCK)
                lane_slab = lambda oi, w: po[oi].reshape(b, t, w)
                o_s = _flash("slc", qn, lane_slab(O_KS, LANES), lane_slab(O_VS, LANES), [mb],
                             tq=min(128, t), tk=min(2048, t), out_dtype=f32)
                o_w = _flash("win", qn, lane_slab(O_KW, LANES), lane_slab(O_VW, LANES), [],
                             tq=min(256, t), tk=min(512, t), out_dtype=f32)
                tk_f = min(2048, t)
                f_t = _cumsum_lanes(jnp.transpose(logf, (0, 2, 1)), min(512, t))
                fk = jnp.transpose(f_t.reshape(b, FOX_H // 2, 2, t // tk_f, tk_f), (0, 1, 3, 2, 4))
                fq = f_t.reshape(b, FOX_H // 2, 2, t, 1)
                o_f = _flash("fox", qf, lane_slab(O_KF, 4 * LANES), lane_slab(O_VF, 4 * LANES), [fk, fq],
                             tq=min(512, t), tk=tk_f, out_dtype=bf16)
                acts = [a.reshape(m, a.shape[-1]) for a in (o_cmp, o_s, o_w)] + [po[O_GATE], o_f.reshape(m, -1)]
            else:
                c_cmp, c_slc, c_win, c_fox, c_logf, _ = caches
                outs.setdefault("win", []).append(jnp.concatenate([c_win[e], kvw_new], axis=1)[:, t:])
                kt_view = lambda c: jnp.transpose(c, (0, 2, 3, 4, 1))
                qn = per_seq(po[O_QN])
                o_cmp, o_s, o_w = _dec_nsa(page_table, qn, kt_view(c_cmp[e]), kt_view(c_slc[e]),
                                           kt_view(c_win[e]), per_seq(po[O_KVS]), per_seq(po[O_KVW]),
                                           w_pool[e], qpos0=qpos0, n_new=t)
                qf = po[O_QF].reshape(b, t, 1, FOX_H * HD)
                head_of_lane = np.arange(FOX_H * HD) // HD
                own = jnp.asarray(head_of_lane[None, :] == np.arange(FOX_H)[:, None], dtype=bf16)
                qbd = (qf * own[None, None]).reshape(b, t * FOX_H, FOX_H * HD)
                lf_new = jnp.pad(jnp.transpose(logf, (0, 2, 1)), ((0, 0), (0, 0), (0, LANES - t)))
                o_f = _dec_fox(page_table, qbd, kt_view(c_fox[e]), jnp.transpose(c_logf[e], (0, 2, 1)),
                               per_seq(po[O_KVF]), lf_new, n_new=t)
                acts = [a[:, :t].reshape(m, a.shape[-1]) for a in (o_cmp, o_s, o_w)]
                acts += [po[O_GATE], o_f[:, :t].reshape(m, -1)]
            x2d = _outproj(_outproj_even_body, x2d, gt1, tiles_per_mod, acts, wts["out_even"][e], tm,
                           "outproj_even")
        else:
            o = li // 2
            po = _proj(x2d, norm_mix_g[li], sc1, sh1, tiles_per_mod, wts["in_odd"][o], tables,
                       jnp.zeros((1, LANES), f32), _odd_plan(), ODD_OUT_DEFS, tm)
            outs.setdefault("diff", []).append(po[O_KVD].reshape(b, t, 2, DIFF_H, 2 * HD))
            lam_init = 0.8 - 0.6 * math.exp(-0.3 * li)
            sub_g = subln_g[o].reshape(1, 2 * HD)
            width = 2 * DIFF_H * HD
            if not decode:
                o_d = _flash("diff", po[O_QD].reshape(b, t, width), po[O_KD].reshape(b, t, width),
                             po[O_VD].reshape(b, t, width), [lam_qk[o], sub_g], tq=min(512, t), tk=min(2048, t),
                             out_dtype=bf16, lam_init=lam_init)
            else:
                kvd = po[O_KVD].reshape(b, t, 2, DIFF_H, 2 * HD)
                rows_new = lambda a: a.reshape(b, t * DIFF_H, 2 * HD)
                o_d = _dec_diff(page_table, per_seq(po[O_QD]), caches[5][o], rows_new(kvd[:, :, 0]),
                                rows_new(kvd[:, :, 1]), lam_qk[o], sub_g, n_new=t, lam_init=lam_init)[:, :t]
            x2d = _outproj(_outproj_odd_body, x2d, gt1, tiles_per_mod, [o_d.reshape(m, d)],
                           wts["out_odd"][o], tm, "outproj_odd")
        x2d = _mlp(x2d, norm_ffn_g[li], sc2, sh2, gt2, mlp_tiles_per_mod, wts["ff_in"][li], wts["ff_out"][li],
                   final_g, li == depth - 1, tm_mlp, 1024)
    stack = lambda name: jnp.stack(outs[name], 0)
    return (x2d.reshape(b, t, d), stack("cmp"), stack("slc"), stack("win"), stack("fox"), stack("logf"),
            stack("diff"))


def kernel(x_prompt, x_sample, cache_nsa_cmp_kv, cache_nsa_slc_kv, cache_nsa_win_kv, cache_fox_kv,
           cache_fox_logf, cache_diff_kv, page_table, c_prompt, c_sample, w_in_even, b_f, w_pool,
           w_out_even, w_in_odd, lam_qk, subln_g, w_out_odd, norm_mix_g, norm_ffn_g, w_ada, b_ada,
           w_ff_in, w_ff_out, final_g):
    depth = w_ada.shape[0]
    nb_p, nb_s = c_prompt.shape[0], c_sample.shape[0]
    past_len = page_table.shape[1] * PAGE_SIZE

    perm_rows = lambda w: jnp.concatenate(
        [w[:NSA_H * HD].reshape(NSA_H, HD, -1)[NSA_PERM, :, :].reshape(NSA_H * HD, -1), w[NSA_H * HD:]], axis=0)
    wts = dict(
        in_even=[_even_weight(w_in_even[e]) for e in range(w_in_even.shape[0])],
        out_even=[perm_rows(w_out_even[e]).astype(bf16) for e in range(w_out_even.shape[0])],
        in_odd=[w_in_odd[o].astype(bf16) for o in range(w_in_odd.shape[0])],
        out_odd=[w_out_odd[o].astype(bf16) for o in range(w_out_odd.shape[0])],
        ff_in=[w_ff_in[l].astype(bf16) for l in range(depth)],
        ff_out=[w_ff_out[l].astype(bf16) for l in range(depth)],
    )
    rows = -(-(nb_p + nb_s) // SUBLANES) * SUBLANES
    c_all = jnp.pad(jnp.concatenate([c_prompt, c_sample], axis=0), ((0, rows - nb_p - nb_s), (0, 0)))
    mod_all = _ada(c_all, w_ada.astype(bf16), b_ada)
    mods_p = [mod_all[l, :nb_p] for l in range(depth)]
    mods_s = [mod_all[l, nb_p:nb_p + nb_s] for l in range(depth)]

    params = (b_f, w_pool, lam_qk, subln_g, norm_mix_g, norm_ffn_g, final_g)
    caches = (cache_nsa_cmp_kv, cache_nsa_slc_kv, cache_nsa_win_kv, cache_fox_kv, cache_fox_logf, cache_diff_kv)
    y_p, p_cmp, p_slc, p_win, p_fox, p_logf, p_diff = _trunk(x_prompt, mods_p, 0, None, None, wts, params)
    y_s, s_cmp, s_slc, s_win, s_fox, s_logf, s_diff = _trunk(x_sample, mods_s, past_len, page_table, caches,
                                                             wts, params)
    return (y_p, y_s, p_cmp, s_cmp, p_slc, s_slc, p_win, s_win, p_fox, s_fox, p_logf, s_logf, p_diff, s_diff)
```

```python
import functools
import math

import numpy as np
import jax
import jax.numpy as jnp
from jax import lax
from jax.experimental import pallas as pl
from jax.experimental.pallas import tpu as pltpu

HD = 64
ROPE_DIM = HD // 4
ROPE_HALF = ROPE_DIM // 2
ROPE_THETA = 500000.0
NSA_H = 8
NSA_KVH = 2
CMP_BLOCK = 64
CMP_SHIFT = 6
N_SEL = 16
WINDOW = 512
FOX_H = 8
DIFF_H = 8
EPS = 1e-6
PAGE_SIZE = 128

LANES = 128
SUBLANES = 8
VMEM_LIMIT_BYTES = 48 * 1024 * 1024

NEG_MASK = -1e30
NEG_BLOCK = -1e9
PAGES_PER_STEP = 4
FLASH_TK = 1024

NSA_PERM = (0, 4, 1, 5, 2, 6, 3, 7)

f32 = jnp.float32
bf16 = jnp.bfloat16


def _cparams(*sem):
    return pltpu.CompilerParams(dimension_semantics=sem, vmem_limit_bytes=VMEM_LIMIT_BYTES)


def _dot(a, b):
    return jnp.dot(a, b, preferred_element_type=f32)


def _dot_nt(a, b):
    return lax.dot_general(a, b, (((1,), (1,)), ((), ())), preferred_element_type=f32)


def _rms_mod(x, g, sc, sh):
    y = x * lax.rsqrt(jnp.mean(x * x, axis=-1, keepdims=True) + EPS)
    return (y * g) * (1.0 + sc) + sh


def _split_dot(x, w, parts):
    acc = None
    for _ in range(parts):
        piece = x.astype(bf16)
        term = _dot(piece, w)
        acc = term if acc is None else acc + term
        x = x - piece.astype(f32)
    return acc


def _ada_body(c_ref, w_ref, b_ref, o_ref):
    c = c_ref[...]
    a = (c * jax.nn.sigmoid(c)).astype(bf16)
    o_ref[0] = _dot(a, w_ref[0]) + b_ref[0]


def _ada(c_all, w_ada_b, b_ada):
    depth, d, n = w_ada_b.shape
    rows = c_all.shape[0]
    tn = n // 4
    return pl.pallas_call(
        _ada_body,
        out_shape=jax.ShapeDtypeStruct((depth, rows, n), f32),
        grid=(depth, n // tn),
        in_specs=[pl.BlockSpec((rows, d), lambda l, j: (0, 0)),
                  pl.BlockSpec((1, d, tn), lambda l, j: (l, 0, j)),
                  pl.BlockSpec((1, 1, tn), lambda l, j: (l, 0, j))],
        out_specs=pl.BlockSpec((1, rows, tn), lambda l, j: (l, 0, j)),
        compiler_params=_cparams("parallel", "parallel"),
        name="ada_mod",
    )(c_all, w_ada_b, b_ada.reshape(depth, 1, n))


def _rope_group(a, cos, sa, sb):
    return a * cos + pltpu.roll(a, LANES - ROPE_HALF, 1) * sa + pltpu.roll(a, ROPE_HALF, 1) * sb


def _proj_body(plan, x_ref, g_ref, sc_ref, sh_ref, w_ref, cos_ref, sa_ref, sb_ref, bias_ref, *out_refs):
    h = _rms_mod(x_ref[...], g_ref[...], sc_ref[0], sh_ref[0]).astype(bf16)
    cos, sa, sb = cos_ref[...], sa_ref[...], sb_ref[...]
    for src, groups in plan:
        width = LANES * len(groups)
        acc = _dot(h, w_ref[:, src:src + width])
        for gi, (kind, scale, dests) in enumerate(groups):
            a = acc[:, gi * LANES:(gi + 1) * LANES]
            if kind == "rope":
                a = _rope_group(a, cos, sa, sb)
            elif kind == "sigmoid":
                a = jax.nn.sigmoid(a)
            elif kind == "logsig":
                z = a + bias_ref[...]
                a = jnp.minimum(z, 0.0) - jnp.log1p(jnp.exp(-jnp.abs(z)))
            if scale != 1.0:
                a = a * scale
            for oi, col in dests:
                out_refs[oi][:, col:col + LANES] = a.astype(out_refs[oi].dtype)


def _proj(x2d, g, sc, sh, tiles_per_mod, w_b, tables, bias, plan, out_defs, tm):
    m, d = x2d.shape
    n = w_b.shape[1]
    cos, sa, sb = tables
    mod_rows = sc.shape[1]
    row = lambda i: (i, 0)
    const = lambda i: (0, 0)
    mod = lambda i: (i // tiles_per_mod, 0, 0)
    return pl.pallas_call(
        functools.partial(_proj_body, plan),
        out_shape=[jax.ShapeDtypeStruct((m, w), dt) for w, dt in out_defs],
        grid=(m // tm,),
        in_specs=[pl.BlockSpec((tm, d), row),
                  pl.BlockSpec((1, d), const),
                  pl.BlockSpec((1, mod_rows, d), mod),
                  pl.BlockSpec((1, mod_rows, d), mod),
                  pl.BlockSpec((d, n), const, pipeline_mode=pl.Buffered(1)),
                  pl.BlockSpec((tm, LANES), row),
                  pl.BlockSpec((tm, LANES), row),
                  pl.BlockSpec((tm, LANES), row),
                  pl.BlockSpec((1, LANES), const)],
        out_specs=[pl.BlockSpec((tm, w), row) for w, _ in out_defs],
        compiler_params=_cparams("parallel"),
        name="norm_mod_proj",
    )(x2d, g.reshape(1, d), sc, sh, w_b, cos, sa, sb, bias)


def _rope_tables(pos):
    inv = np.float32(ROPE_THETA) ** (-np.arange(ROPE_HALF, dtype=np.float32) / np.float32(ROPE_HALF))
    ang = (pos.astype(np.float32)[:, None] * inv[None, :]).astype(np.float64)
    n = pos.shape[0]
    cos = np.ones((n, HD), np.float64)
    sa = np.zeros((n, HD), np.float64)
    sb = np.zeros((n, HD), np.float64)
    cos[:, :ROPE_HALF] = np.cos(ang)
    cos[:, ROPE_HALF:ROPE_DIM] = np.cos(ang)
    sa[:, :ROPE_HALF] = -np.sin(ang)
    sb[:, ROPE_HALF:ROPE_DIM] = np.sin(ang)
    tile = lambda t: jnp.asarray(np.tile(t, (1, LANES // HD)).astype(np.float32))
    return tile(cos), tile(sa), tile(sb)


EV_QN, EV_KVC, EV_KVS, EV_KVW, EV_QF, EV_KVF, EV_GATE, EV_FF = 0, 512, 768, 1024, 1280, 1792, 2816, 4352


def _even_weight(w):
    d = w.shape[0]
    sizes = [NSA_H * HD, 2 * NSA_KVH * HD, 2 * NSA_KVH * HD, 2 * NSA_KVH * HD, 3 * NSA_H, FOX_H * HD,
             2 * FOX_H * HD, FOX_H]
    offs = np.concatenate([[0], np.cumsum(sizes)])
    q_n, kv_c, kv_s, kv_w, gates, q_f, kv_f, f_f = [w[:, offs[i]:offs[i + 1]] for i in range(8)]
    q_n = q_n.reshape(d, NSA_H, HD)[:, NSA_PERM, :].reshape(d, NSA_H * HD)
    gates = gates.reshape(d, NSA_H, 3)[:, NSA_PERM, :]
    gates = jnp.repeat(jnp.transpose(gates, (0, 2, 1))[..., None], HD, axis=-1).reshape(d, 3 * NSA_H * HD)
    f_f = jnp.pad(f_f, ((0, 0), (0, LANES - FOX_H)))
    return jnp.concatenate([q_n, kv_c, kv_s, kv_w, q_f, kv_f, gates, f_f], axis=1).astype(bf16)


(O_QN, O_KVC, O_KVS, O_KS, O_VS, O_KVW, O_KW, O_VW, O_QF, O_KVF, O_KF, O_VF, O_GATE, O_LOGF) = range(14)
EVEN_OUT_DEFS = [(512, bf16), (256, f32), (256, f32), (128, bf16), (128, bf16), (256, f32), (128, bf16),
                 (128, bf16), (512, bf16), (1024, f32), (512, bf16), (512, bf16), (1536, f32), (128, f32)]


def _even_plan():
    scale = HD ** -0.5
    plan = [(EV_QN, [("rope", scale, [(O_QN, LANES * i)]) for i in range(4)]),
            (EV_KVC, [("rope", 1.0, [(O_KVC, 0)]), ("plain", 1.0, [(O_KVC, LANES)])]),
            (EV_KVS, [("rope", 1.0, [(O_KVS, 0), (O_KS, 0)]), ("plain", 1.0, [(O_KVS, LANES), (O_VS, 0)])]),
            (EV_KVW, [("rope", 1.0, [(O_KVW, 0), (O_KW, 0)]), ("plain", 1.0, [(O_KVW, LANES), (O_VW, 0)])]),
            (EV_QF, [("plain", scale, [(O_QF, LANES * i)]) for i in range(4)]),
            (EV_KVF, [("plain", 1.0, [(O_KVF, LANES * i), (O_KF, LANES * i)]) for i in range(4)]),
            (EV_KVF + 512, [("plain", 1.0, [(O_KVF, 512 + LANES * i), (O_VF, LANES * i)]) for i in range(4)])]
    for j in range(3):
        plan.append((EV_GATE + 512 * j, [("sigmoid", 1.0, [(O_GATE, 512 * j + LANES * i)]) for i in range(4)]))
    plan.append((EV_FF, [("logsig", 1.0, [(O_LOGF, 0)])]))
    return plan


(O_QD, O_KVD, O_KD, O_VD) = range(4)
ODD_OUT_DEFS = [(1024, bf16), (2048, f32), (1024, bf16), (1024, bf16)]


def _odd_plan():
    scale = HD ** -0.5
    plan = []
    for c in range(2):
        plan.append((512 * c, [("rope", scale, [(O_QD, 512 * c + LANES * i)]) for i in range(4)]))
    for c in range(2):
        plan.append((1024 + 512 * c,
                     [("rope", 1.0, [(O_KVD, 512 * c + LANES * i), (O_KD, 512 * c + LANES * i)]) for i in range(4)]))
    for c in range(2):
        plan.append((2048 + 512 * c,
                     [("plain", 1.0, [(O_KVD, 1024 + 512 * c + LANES * i), (O_VD, 512 * c + LANES * i)])
                      for i in range(4)]))
    return plan


def _outproj_even_body(x_ref, gt_ref, oc_ref, os_ref, ow_ref, gates_ref, of_ref, w_ref, o_ref):
    nw = NSA_H * HD
    o_nsa = (gates_ref[:, 0:nw] * oc_ref[...] + gates_ref[:, nw:2 * nw] * os_ref[...]
             + gates_ref[:, 2 * nw:3 * nw] * ow_ref[...])
    a = jnp.concatenate([o_nsa.astype(bf16), of_ref[...]], axis=1)
    o_ref[...] = x_ref[...] + gt_ref[0] * _dot(a, w_ref[...])


def _outproj_odd_body(x_ref, gt_ref, od_ref, w_ref, o_ref):
    o_ref[...] = x_ref[...] + gt_ref[0] * _dot(od_ref[...], w_ref[...])


def _outproj(body, x2d, gt, tiles_per_mod, acts, w_b, tm, name):
    m, d = x2d.shape
    row = lambda i: (i, 0)
    mod = lambda i: (i // tiles_per_mod, 0, 0)
    return pl.pallas_call(
        body,
        out_shape=jax.ShapeDtypeStruct((m, d), f32),
        grid=(m // tm,),
        in_specs=[pl.BlockSpec((tm, d), row), pl.BlockSpec((1, gt.shape[1], d), mod)]
                 + [pl.BlockSpec((tm, a.shape[1]), row) for a in acts]
                 + [pl.BlockSpec(w_b.shape, lambda i: (0, 0), pipeline_mode=pl.Buffered(1))],
        out_specs=pl.BlockSpec((tm, d), row),
        compiler_params=_cparams("parallel"),
        name=name,
    )(x2d, gt, *acts, w_b)


def _mlp_body(final, x_ref, g_ref, sc_ref, sh_ref, gt_ref, w1_ref, w2_ref, fg_ref, o_ref, h_ref, acc_ref):
    f = pl.program_id(1)

    @pl.when(f == 0)
    def _():
        h_ref[...] = _rms_mod(x_ref[...], g_ref[...], sc_ref[0], sh_ref[0]).astype(bf16)
        acc_ref[...] = jnp.zeros_like(acc_ref)

    a = jnp.maximum(_dot(h_ref[...], w1_ref[...]), 0.0)
    acc_ref[...] += _dot((a * a).astype(bf16), w2_ref[...])

    @pl.when(f == pl.num_programs(1) - 1)
    def _():
        y = x_ref[...] + gt_ref[0] * acc_ref[...]
        if final:
            y = y * lax.rsqrt(jnp.mean(y * y, axis=-1, keepdims=True) + EPS) * fg_ref[...]
        o_ref[...] = y


def _mlp(x2d, g, sc, sh, gt, tiles_per_mod, w1_b, w2_b, final_g, final, tm, tf):
    m, d = x2d.shape
    dff = w1_b.shape[1]
    mod_rows = sc.shape[1]
    row = lambda i, f: (i, 0)
    const = lambda i, f: (0, 0)
    mod = lambda i, f: (i // tiles_per_mod, 0, 0)
    return pl.pallas_call(
        functools.partial(_mlp_body, final),
        out_shape=jax.ShapeDtypeStruct((m, d), f32),
        grid=(m // tm, dff // tf),
        in_specs=[pl.BlockSpec((tm, d), row), pl.BlockSpec((1, d), const),
                  pl.BlockSpec((1, mod_rows, d), mod), pl.BlockSpec((1, mod_rows, d), mod),
                  pl.BlockSpec((1, mod_rows, d), mod),
                  pl.BlockSpec((d, tf), lambda i, f: (0, f)), pl.BlockSpec((tf, d), lambda i, f: (f, 0)),
                  pl.BlockSpec((1, d), const)],
        out_specs=pl.BlockSpec((tm, d), row),
        scratch_shapes=[pltpu.VMEM((tm, d), bf16), pltpu.VMEM((tm, d), f32)],
        compiler_params=_cparams("parallel", "arbitrary"),
        name="relu2_mlp",
    )(x2d, g.reshape(1, d), sc, sh, gt, w1_b, w2_b, final_g.reshape(1, d))


def _compress_body(x_ref, wp_ref, o_ref):
    wp = wp_ref[...]
    rows = []
    for blk in range(x_ref.shape[1] // CMP_BLOCK):
        rows.append(jnp.sum(x_ref[0, blk * CMP_BLOCK:(blk + 1) * CMP_BLOCK, :] * wp, axis=0, keepdims=True))
    o_ref[0] = jnp.concatenate(rows, axis=0)


def _compress(kv, w_pool_e):
    b, t, c = kv.shape
    wp = jnp.repeat(jnp.transpose(w_pool_e, (1, 0, 2))[..., None], HD, axis=-1).reshape(CMP_BLOCK, c)
    rows_in = PAGES_PER_STEP * PAGE_SIZE
    rows_out = rows_in // CMP_BLOCK
    return pl.pallas_call(
        _compress_body,
        out_shape=jax.ShapeDtypeStruct((b, t // CMP_BLOCK, c), f32),
        grid=(b, t // rows_in),
        in_specs=[pl.BlockSpec((1, rows_in, c), lambda bi, g: (bi, g, 0)),
                  pl.BlockSpec((CMP_BLOCK, c), lambda bi, g: (0, 0))],
        out_specs=pl.BlockSpec((1, rows_out, c), lambda bi, g: (bi, g, 0)),
        compiler_params=_cparams("parallel", "parallel"),
        name="nsa_compress",
    )(kv, wp)


def _cumsum_body(x_ref, u_ref, o_ref, carry_ref):
    @pl.when(pl.program_id(1) == 0)
    def _():
        carry_ref[...] = jnp.zeros_like(carry_ref)

    cs = _split_dot(x_ref[0], u_ref[...], 3) + carry_ref[...]
    o_ref[0] = cs
    carry_ref[...] = cs[:, cs.shape[1] - 1:]


def _upper_ones(n):
    return jnp.asarray(np.triu(np.ones((n, n), np.float32)), dtype=bf16)


def _cumsum_lanes(x_t, tc):
    b, h, l = x_t.shape
    return pl.pallas_call(
        _cumsum_body,
        out_shape=jax.ShapeDtypeStruct((b, h, l), f32),
        grid=(b, l // tc),
        in_specs=[pl.BlockSpec((1, h, tc), lambda bi, c: (bi, 0, c)),
                  pl.BlockSpec((tc, tc), lambda bi, c: (0, 0))],
        out_specs=pl.BlockSpec((1, h, tc), lambda bi, c: (bi, 0, c)),
        scratch_shapes=[pltpu.VMEM((h, 1), f32)],
        compiler_params=_cparams("parallel", "arbitrary"),
        name="logf_cumsum",
    )(x_t, _upper_ones(tc))


def _stack_heads(q, qs_ref, n_slab, tq):
    lane = lax.broadcasted_iota(jnp.int32, (tq, LANES), 1)
    low = lane < HD
    for s in range(n_slab):
        slab = q[:, s * LANES:(s + 1) * LANES].astype(f32)
        qs_ref[(2 * s) * tq:(2 * s + 1) * tq, 0:LANES] = jnp.where(low, slab, 0.0).astype(qs_ref.dtype)
        qs_ref[(2 * s + 1) * tq:(2 * s + 2) * tq, 0:LANES] = jnp.where(low, 0.0, slab).astype(qs_ref.dtype)


def _unstack_heads(o, n_slab, tq):
    lane = lax.broadcasted_iota(jnp.int32, (tq, LANES), 1)
    low = lane < HD
    slabs = [jnp.where(low, o[(2 * s) * tq:(2 * s + 1) * tq], o[(2 * s + 1) * tq:(2 * s + 2) * tq])
             for s in range(n_slab)]
    return slabs[0] if n_slab == 1 else jnp.concatenate(slabs, axis=1)


def _masked_softmax(s, valid):
    sm = jnp.where(valid, s, NEG_MASK)
    m = jnp.max(sm, axis=1, keepdims=True)
    p = jnp.where(valid, jnp.exp(sm - m), 0.0)
    return p / jnp.maximum(jnp.sum(p, axis=1, keepdims=True), 1e-30)


def _select_blocks(p, tg, tq, nb_valid):
    n = lax.broadcasted_iota(jnp.int32, (1, LANES), 1)
    cur = jnp.right_shift(tg, CMP_SHIFT)
    n_cand = jnp.minimum(cur, nb_valid)
    out = []
    for g in range(NSA_KVH):
        imp = p[g * tq:(g + 1) * tq]
        for s_i in range(1, NSA_H // 2):
            imp = imp + p[(2 * s_i + g) * tq:(2 * s_i + g + 1) * tq]
        work = jnp.where(n < n_cand, imp, -1.0)
        sel = n == cur
        for it in range(N_SEL - 1):
            hit = n == jnp.argmax(work, axis=1, keepdims=True).astype(jnp.int32)
            sel = jnp.logical_or(sel, jnp.logical_and(hit, it < n_cand))
            work = jnp.where(hit, -2.0, work)
        out.append(jnp.where(sel, 0.0, NEG_BLOCK))
    return out


def _flash_body(cfg, *refs):
    mode, n_slab, tq, tk, n_chunks = cfg["mode"], cfg["n_slab"], cfg["tq"], cfg["tk"], cfg["n_chunks"]
    window = cfg["window"]
    rows = 2 * n_slab * tq
    q_ref, k_ref, v_ref = refs[:3]
    extra = refs[3:-5]
    o_ref, qs_ref, s_ref, m_ref, acc_ref = refs[-5:]
    qi = pl.program_id(2)

    _stack_heads(q_ref[0], qs_ref, n_slab, tq)
    if mode == "slc":
        mb_ref = extra[0]
        for s in range(n_slab):
            for g in range(2):
                qs_ref[(2 * s + g) * tq:(2 * s + g + 1) * tq, LANES:2 * LANES] = mb_ref[0, g]
    m_ref[...] = jnp.full_like(m_ref, NEG_MASK)
    acc_ref[...] = jnp.zeros_like(acc_ref)

    q_lo = qi * tq
    qpos = q_lo + (lax.broadcasted_iota(jnp.int32, (rows, 1), 0) & (tq - 1))

    def scores(c):
        ks = pl.multiple_of(c * tk, tk)
        k = k_ref[0, pl.ds(ks, tk), :]
        if mode == "slc":
            blk = jnp.right_shift(ks + lax.broadcasted_iota(jnp.int32, (tk, LANES), 0), CMP_SHIFT)
            hot = jnp.where(blk == lax.broadcasted_iota(jnp.int32, (tk, LANES), 1), 1.0, 0.0).astype(bf16)
            k = jnp.concatenate([k, hot], axis=1)
        return _dot_nt(qs_ref[...], k)

    def update(s, c, masked):
        ks = pl.multiple_of(c * tk, tk)
        if mode == "fox":
            fk_ref, fq_ref = extra
            fk = fk_ref[0, 0, c]
            s = s + jnp.concatenate([fq_ref[0, 0, 0] - fk[0:1], fq_ref[0, 0, 1] - fk[1:2]], axis=0)
        if masked:
            kpos = ks + lax.broadcasted_iota(jnp.int32, (1, tk), 1)
            mask = kpos <= qpos
            if window is not None:
                mask = jnp.logical_and(mask, (qpos - kpos) < window)
            s = jnp.where(mask, s, NEG_MASK)
        m_old = m_ref[...]
        m_new = jnp.maximum(m_old, jnp.max(s, axis=1, keepdims=True))
        alpha = jnp.exp(m_old - m_new)
        p = jnp.exp((s - m_new).astype(bf16))
        v_ones = jnp.concatenate([v_ref[0, pl.ds(ks, tk), :], jnp.ones((tk, LANES), bf16)], axis=1)
        acc_ref[...] = alpha * acc_ref[...] + _dot(p, v_ones)
        m_ref[...] = m_new

    if window is not None:
        c_lo = jnp.maximum(q_lo - window + 1, 0) // tk
        c_hi = jnp.minimum(n_chunks, (q_lo + tq - 1) // tk + 1)

        def step(c, carry):
            update(scores(c), c, True)
            return carry

        lax.fori_loop(c_lo, c_hi, step, 0)
    else:
        n_full = (q_lo + 1) // tk

        def issue(c, slot):
            s_ref[slot] = scores(c)

        issue(0, 0)

        def pair(j, carry):
            c = 2 * j
            issue(c + 1, 1)
            update(s_ref[0], c, False)
            issue(c + 2, 0)
            update(s_ref[1], c + 1, False)
            return carry

        lax.fori_loop(0, n_full // 2, pair, 0)

        @pl.when((n_full & 1) == 1)
        def _():
            issue(n_full, 1)
            update(s_ref[0], n_full - 1, False)
            update(s_ref[1], n_full, True)

        @pl.when((n_full & 1) == 0)
        def _():
            update(s_ref[0], n_full, True)

    acc = acc_ref[...]
    o = acc[:, :LANES] / jnp.maximum(acc[:, LANES:], 1e-30)
    if mode == "diff":
        lq_ref, sg_ref = extra
        o_ref[0] = _diff_combine(o[0:tq], o[tq:2 * tq], lq_ref[...], sg_ref[...], cfg["lam_init"]).astype(o_ref.dtype)
    else:
        o_ref[0] = _unstack_heads(o, n_slab, tq).astype(o_ref.dtype)


def _diff_combine(o1, o2, lp, sub_g, lam_init):
    lam = (jnp.exp(jnp.sum(lp[0:1] * lp[1:2], axis=1, keepdims=True))
           - jnp.exp(jnp.sum(lp[2:3] * lp[3:4], axis=1, keepdims=True)) + lam_init)
    d = o1 - lam * o2
    d = d * lax.rsqrt(jnp.mean(d * d, axis=-1, keepdims=True) + EPS) * sub_g
    return d * (1.0 - lam_init)


def _flash(mode, q, k, v, extra, *, tq, tk, out_dtype, lam_init=0.0):
    b, t_q, cq = q.shape
    t_k = k.shape[1]
    n_slab = 4 if mode in ("slc", "win") else 1
    n_grid_slab = cq // (LANES * n_slab)
    n_chunks = t_k // tk
    rows = 2 * n_slab * tq
    ck = 2 * LANES if mode == "slc" else LANES
    if mode != "win":
        assert tk % tq == 0
    score_slots = (2, rows, tk) if mode != "win" else (2, SUBLANES, LANES)
    cfg = dict(mode=mode, n_slab=n_slab, tq=tq, tk=tk, n_chunks=n_chunks,
               window=WINDOW if mode == "win" else None, lam_init=lam_init)
    qw = LANES * n_slab
    in_specs = [pl.BlockSpec((1, tq, qw), lambda bi, s, i: (bi, i, s)),
                pl.BlockSpec((1, t_k, LANES), lambda bi, s, i: (bi, 0, s)),
                pl.BlockSpec((1, t_k, LANES), lambda bi, s, i: (bi, 0, s))]
    if mode == "fox":
        in_specs += [pl.BlockSpec((1, 1, n_chunks, 2, tk), lambda bi, s, i: (bi, s, 0, 0, 0)),
                     pl.BlockSpec((1, 1, 2, tq, 1), lambda bi, s, i: (bi, s, 0, i, 0))]
    elif mode == "slc":
        in_specs += [pl.BlockSpec((1, 2, tq, LANES), lambda bi, s, i: (bi, 0, i, 0))]
    elif mode == "diff":
        in_specs += [pl.BlockSpec((4, HD), lambda bi, s, i: (0, 0)),
                     pl.BlockSpec((1, LANES), lambda bi, s, i: (0, 0))]
    return pl.pallas_call(
        functools.partial(_flash_body, cfg),
        out_shape=jax.ShapeDtypeStruct((b, t_q, cq), out_dtype),
        grid=(b, n_grid_slab, t_q // tq),
        in_specs=in_specs,
        out_specs=pl.BlockSpec((1, tq, qw), lambda bi, s, i: (bi, i, s)),
        scratch_shapes=[pltpu.VMEM((rows, ck), bf16), pltpu.VMEM(score_slots, f32), pltpu.VMEM((rows, 1), f32),
                        pltpu.VMEM((rows, 2 * LANES), f32)],
        compiler_params=_cparams("parallel", "parallel", "parallel"),
        name="flash_" + mode,
    )(q, k, v, *extra)


def _cmp_body(tq, nb_valid, q_ref, ck_ref, cv_ref, o_ref, mb_ref, qs_ref):
    n_slab = NSA_H // 2
    rows = 2 * n_slab * tq
    qi = pl.program_id(1)
    _stack_heads(q_ref[0], qs_ref, n_slab, tq)
    s = _dot_nt(qs_ref[...], ck_ref[0])
    t = qi * tq + (lax.broadcasted_iota(jnp.int32, (rows, 1), 0) & (tq - 1))
    n = lax.broadcasted_iota(jnp.int32, (1, LANES), 1)
    valid = jnp.logical_and((n + 1) * CMP_BLOCK - 1 <= t, n < nb_valid)
    p = _masked_softmax(s, valid)
    o_ref[0] = _unstack_heads(_dot(p.astype(bf16), cv_ref[0]), n_slab, tq)
    bias = _select_blocks(p, t[0:tq], tq, nb_valid)
    for g in range(NSA_KVH):
        mb_ref[0, g] = bias[g].astype(bf16)


def _cmp_topk(q, ck, cv, *, tq, nb_valid):
    b, t_q, cq = q.shape
    rows = NSA_H * tq
    return pl.pallas_call(
        functools.partial(_cmp_body, tq, nb_valid),
        out_shape=[jax.ShapeDtypeStruct((b, t_q, cq), f32), jax.ShapeDtypeStruct((b, NSA_KVH, t_q, LANES), bf16)],
        grid=(b, t_q // tq),
        in_specs=[pl.BlockSpec((1, tq, cq), lambda bi, i: (bi, i, 0)),
                  pl.BlockSpec((1, LANES, LANES), lambda bi, i: (bi, 0, 0)),
                  pl.BlockSpec((1, LANES, LANES), lambda bi, i: (bi, 0, 0))],
        out_specs=[pl.BlockSpec((1, tq, cq), lambda bi, i: (bi, i, 0)),
                   pl.BlockSpec((1, NSA_KVH, tq, LANES), lambda bi, i: (bi, 0, i, 0))],
        scratch_shapes=[pltpu.VMEM((rows, LANES), bf16)],
        compiler_params=_cparams("parallel", "parallel"),
        name="nsa_cmp_topk",
    )(q, ck, cv)


DEC_ROWS = SUBLANES


def _pad_keys(new, lo, width):
    x = new[:, lo:lo + width]
    return jnp.concatenate([x, jnp.zeros((LANES - x.shape[0], width), f32)], axis=0).astype(bf16)


def _new_key_mask(rows, n_new):
    qi = lax.broadcasted_iota(jnp.int32, (rows, 1), 0) & (DEC_ROWS - 1)
    kj = lax.broadcasted_iota(jnp.int32, (1, LANES), 1)
    return jnp.logical_and(kj <= qi, kj < n_new)


def _softmax2(s1, s2, mask2):
    s2 = jnp.where(mask2, s2, NEG_MASK)
    m = jnp.maximum(jnp.max(s1, axis=1, keepdims=True), jnp.max(s2, axis=1, keepdims=True))
    p1 = jnp.exp(s1 - m)
    p2 = jnp.where(mask2, jnp.exp(s2 - m), 0.0)
    den = jnp.sum(p1, axis=1, keepdims=True) + jnp.sum(p2, axis=1, keepdims=True)
    return p1, p2, jnp.maximum(den, 1e-30)


def _dec_nsa_body(cfg, pt_ref, *refs):
    n_pages, qpos0, n_new, w_buf = cfg["n_pages"], cfg["qpos0"], cfg["n_new"], cfg["w_buf"]
    cmp_pages = refs[:n_pages]
    slc_pages = refs[n_pages:2 * n_pages]
    (q_ref, win_ref, news_ref, neww_ref, wp_ref, e_ref, oht_ref,
     oc_ref, os_ref, ow_ref, qs_ref) = refs[2 * n_pages:]
    tq, n_slab = DEC_ROWS, NSA_H // 2
    rows = NSA_H * tq
    gd = NSA_KVH * HD

    _stack_heads(q_ref[0], qs_ref, n_slab, tq)
    qs = qs_ref[...].astype(bf16)
    t = qpos0 + (lax.broadcasted_iota(jnp.int32, (rows, 1), 0) & (tq - 1))
    n = lax.broadcasted_iota(jnp.int32, (1, LANES), 1)
    mask_new = _new_key_mask(rows, n_new)

    def compressed(kv):
        xt = jnp.concatenate([p[0, kv].reshape(gd, PAGE_SIZE) for p in cmp_pages], axis=1)
        w = jnp.concatenate([wp_ref[kv]] * n_pages, axis=1)
        return _split_dot(xt * w, e_ref[...], 2).astype(bf16)

    ck_t, cv_t = compressed(0), compressed(1)
    nb_valid = n_pages * PAGE_SIZE // CMP_BLOCK
    valid = jnp.logical_and((n + 1) * CMP_BLOCK - 1 <= t, n < nb_valid)
    p = _masked_softmax(_dot(qs, ck_t), valid)
    oc_ref[0] = _unstack_heads(_dot_nt(p.astype(bf16), cv_t), n_slab, tq)

    bias = _select_blocks(p, t[0:tq], tq, nb_valid)
    qq = jnp.concatenate([qs, jnp.concatenate(bias * n_slab, axis=0).astype(bf16)], axis=1)
    ks_t = jnp.concatenate([p_[0, 0].reshape(gd, PAGE_SIZE) for p_ in slc_pages], axis=1).astype(bf16)
    vs_t = jnp.concatenate([p_[0, 1].reshape(gd, PAGE_SIZE) for p_ in slc_pages], axis=1).astype(bf16)
    s1 = _dot(qq, jnp.concatenate([ks_t, oht_ref[...]], axis=0))
    new_s = news_ref[0]
    s2 = _dot_nt(qs, _pad_keys(new_s, 0, gd))
    p1, p2, den = _softmax2(s1, s2, mask_new)
    o = _dot_nt(p1.astype(bf16), vs_t) + _dot(p2.astype(bf16), _pad_keys(new_s, gd, gd))
    os_ref[0] = _unstack_heads(o / den, n_slab, tq)

    kw_t = win_ref[0, 0].reshape(gd, w_buf).astype(bf16)
    vw_t = win_ref[0, 1].reshape(gd, w_buf).astype(bf16)
    kpos = qpos0 - w_buf + lax.broadcasted_iota(jnp.int32, (1, w_buf), 1)
    s1 = jnp.where((t - kpos) < WINDOW, _dot(qs, kw_t), NEG_MASK)
    new_w = neww_ref[0]
    s2 = _dot_nt(qs, _pad_keys(new_w, 0, gd))
    p1, p2, den = _softmax2(s1, s2, mask_new)
    o = _dot_nt(p1.astype(bf16), vw_t) + _dot(p2.astype(bf16), _pad_keys(new_w, gd, gd))
    ow_ref[0] = _unstack_heads(o / den, n_slab, tq)


def _dec_nsa(page_table, qn, cmp_pool, slc_pool, win_t, new_s, new_w, w_pool_e, *, qpos0, n_new):
    b, n_pages = page_table.shape
    w_buf = win_t.shape[-1]
    l_k = n_pages * PAGE_SIZE
    gd = NSA_KVH * HD
    cfg = dict(n_pages=n_pages, qpos0=qpos0, n_new=n_new, w_buf=w_buf)
    wp = jnp.tile(jnp.repeat(jnp.transpose(w_pool_e, (0, 2, 1)), HD, axis=1), (1, 1, PAGE_SIZE // CMP_BLOCK))
    blk_of = np.arange(l_k) // CMP_BLOCK
    e_all = jnp.asarray(blk_of[:, None] == np.arange(LANES)[None, :], dtype=bf16)
    oh_t = jnp.asarray(np.arange(LANES)[:, None] == blk_of[None, :], dtype=bf16)
    page = lambda i: pl.BlockSpec((1, 2, NSA_KVH, HD, PAGE_SIZE), lambda bi, pt: (pt[bi, i], 0, 0, 0, 0))
    per_b = lambda shape: pl.BlockSpec((1,) + shape, lambda bi, pt: (bi,) + (0,) * len(shape))
    const = lambda shape: pl.BlockSpec(shape, lambda bi, pt: (0,) * len(shape))
    grid_spec = pltpu.PrefetchScalarGridSpec(
        num_scalar_prefetch=1, grid=(b,),
        in_specs=[page(i) for i in range(n_pages)] * 2
                 + [per_b((DEC_ROWS, NSA_H * HD)), per_b((2, NSA_KVH, HD, w_buf)), per_b((DEC_ROWS, 2 * gd)),
                    per_b((DEC_ROWS, 2 * gd)), const((2, gd, PAGE_SIZE)), const((l_k, LANES)), const((LANES, l_k))],
        out_specs=[per_b((DEC_ROWS, NSA_H * HD))] * 3,
        scratch_shapes=[pltpu.VMEM((NSA_H * DEC_ROWS, LANES), f32)])
    return pl.pallas_call(
        functools.partial(_dec_nsa_body, cfg), grid_spec=grid_spec,
        out_shape=[jax.ShapeDtypeStruct((b, DEC_ROWS, NSA_H * HD), f32)] * 3,
        compiler_params=_cparams("parallel"),
        name="decode_nsa",
    )(page_table, *([cmp_pool] * n_pages), *([slc_pool] * n_pages), qn, win_t, new_s, new_w, wp, e_all, oh_t)


def _dec_fox_body(cfg, pt_ref, *refs):
    n_pages, n_new = cfg["n_pages"], cfg["n_new"]
    kv_pages = refs[:n_pages]
    lf_pages = refs[n_pages:2 * n_pages]
    qbd_ref, new_ref, lfn_ref, u_ref, o_ref = refs[2 * n_pages:]
    hd_all = FOX_H * HD
    rows = FOX_H * n_new
    qbd = qbd_ref[0]
    u = u_ref[...]

    carry = jnp.zeros((FOX_H, 1), f32)
    f_pages = []
    for p in lf_pages:
        c = _split_dot(p[0], u, 3) + carry
        f_pages.append(c)
        carry = c[:, PAGE_SIZE - 1:]
    f_k = jnp.concatenate(f_pages, axis=1)
    f_new = _split_dot(lfn_ref[0], u, 3)

    k_t = jnp.concatenate([p[0, 0].reshape(hd_all, PAGE_SIZE) for p in kv_pages], axis=1).astype(bf16)
    v_t = jnp.concatenate([p[0, 1].reshape(hd_all, PAGE_SIZE) for p in kv_pages], axis=1).astype(bf16)
    new = new_ref[0]
    s1 = _dot(qbd, k_t)
    s2 = _dot_nt(qbd, _pad_keys(new, 0, hd_all))
    kj = lax.broadcasted_iota(jnp.int32, (1, LANES), 1)
    s1_q, s2_q, m2_q = [], [], []
    for qi in range(n_new):
        f_q = carry + f_new[:, qi:qi + 1]
        s1_q.append(s1[qi * FOX_H:(qi + 1) * FOX_H] + f_q - f_k)
        s2_q.append(s2[qi * FOX_H:(qi + 1) * FOX_H] + f_q - (carry + f_new))
        m2_q.append(jnp.broadcast_to(kj <= qi, (FOX_H, LANES)))
    p1, p2, den = _softmax2(jnp.concatenate(s1_q, axis=0), jnp.concatenate(s2_q, axis=0),
                            jnp.concatenate(m2_q, axis=0))
    o = (_dot_nt(p1.astype(bf16), v_t) + _dot(p2.astype(bf16), _pad_keys(new, hd_all, hd_all))) / den
    head = lax.broadcasted_iota(jnp.int32, (FOX_H, hd_all), 0)
    own = jnp.right_shift(lax.broadcasted_iota(jnp.int32, (FOX_H, hd_all), 1), CMP_SHIFT) == head
    out = [jnp.sum(jnp.where(own, o[qi * FOX_H:(qi + 1) * FOX_H], 0.0), axis=0, keepdims=True)
           for qi in range(n_new)]
    out.append(jnp.zeros((DEC_ROWS - n_new, hd_all), f32))
    o_ref[0] = jnp.concatenate(out, axis=0).astype(o_ref.dtype)


def _dec_fox(page_table, qbd, kv_pool, lf_pool, new_kv, lf_new, *, n_new):
    b, n_pages = page_table.shape
    hd_all = FOX_H * HD
    cfg = dict(n_pages=n_pages, n_new=n_new)
    kv_page = lambda i: pl.BlockSpec((1, 2, FOX_H, HD, PAGE_SIZE), lambda bi, pt: (pt[bi, i], 0, 0, 0, 0))
    lf_page = lambda i: pl.BlockSpec((1, FOX_H, PAGE_SIZE), lambda bi, pt: (pt[bi, i], 0, 0))
    per_b = lambda shape: pl.BlockSpec((1,) + shape, lambda bi, pt: (bi,) + (0,) * len(shape))
    grid_spec = pltpu.PrefetchScalarGridSpec(
        num_scalar_prefetch=1, grid=(b,),
        in_specs=[kv_page(i) for i in range(n_pages)] + [lf_page(i) for i in range(n_pages)]
                 + [per_b((FOX_H * n_new, hd_all)), per_b((DEC_ROWS, 2 * hd_all)), per_b((FOX_H, LANES)),
                    pl.BlockSpec((PAGE_SIZE, PAGE_SIZE), lambda bi, pt: (0, 0))],
        out_specs=per_b((DEC_ROWS, hd_all)))
    return pl.pallas_call(
        functools.partial(_dec_fox_body, cfg), grid_spec=grid_spec,
        out_shape=jax.ShapeDtypeStruct((b, DEC_ROWS, hd_all), bf16),
        compiler_params=_cparams("parallel"),
        name="decode_fox",
    )(page_table, *([kv_pool] * n_pages), *([lf_pool] * n_pages), qbd, new_kv, lf_new, _upper_ones(PAGE_SIZE))


def _dec_diff_body(cfg, pt_ref, *refs):
    n_pages, n_new, lam_init = cfg["n_pages"], cfg["n_new"], cfg["lam_init"]
    pages = refs[:n_pages]
    q_ref, knew_ref, vnew_ref, lq_ref, sg_ref, o_ref, qs_ref = refs[n_pages:]
    tq = DEC_ROWS
    rows = 2 * DIFF_H * tq
    cols = PAGE_SIZE * DIFF_H
    _stack_heads(q_ref[0], qs_ref, DIFF_H, tq)
    row = lax.broadcasted_iota(jnp.int32, (rows, 1), 0)
    row_head = jnp.right_shift(row, 4)
    qs = qs_ref[...].astype(bf16)
    col = lax.broadcasted_iota(jnp.int32, (1, cols), 1)
    own = (col & (DIFF_H - 1)) == row_head

    def part(s, mask, v_rows):
        s = jnp.where(mask, s, NEG_MASK)
        m = jnp.max(s, axis=1, keepdims=True)
        p = jnp.exp(s - m)
        return m, jnp.sum(p, axis=1, keepdims=True), _dot(p.astype(bf16), v_rows)

    parts = []
    for p in pages:
        k_rows = p[0, :, 0].reshape(cols, 2 * HD).astype(bf16)
        v_rows = p[0, :, 1].reshape(cols, 2 * HD).astype(bf16)
        parts.append(part(_dot_nt(qs, k_rows), own, v_rows))
    pad = jnp.zeros((LANES - knew_ref.shape[1], 2 * HD), f32)
    k_new = jnp.concatenate([knew_ref[0], pad], axis=0).astype(bf16)
    v_new = jnp.concatenate([vnew_ref[0], pad], axis=0).astype(bf16)
    lane = lax.broadcasted_iota(jnp.int32, (1, LANES), 1)
    mask_new = jnp.logical_and(jnp.logical_and((lane & (DIFF_H - 1)) == row_head,
                                               jnp.right_shift(lane, 3) <= (row & (tq - 1))),
                               lane < n_new * DIFF_H)
    parts.append(part(_dot_nt(qs, k_new), mask_new, v_new))
    m_all = functools.reduce(jnp.maximum, [m for m, _, _ in parts])
    scale = [jnp.exp(m - m_all) for m, _, _ in parts]
    l_fin = functools.reduce(lambda a, b_: a + b_, [w * l for w, (_, l, _) in zip(scale, parts)])
    acc = functools.reduce(lambda a, b_: a + b_, [w * a for w, (_, _, a) in zip(scale, parts)])
    o = acc / jnp.maximum(l_fin, 1e-30)
    heads = [_diff_combine(o[(2 * h) * tq:(2 * h + 1) * tq], o[(2 * h + 1) * tq:(2 * h + 2) * tq],
                           lq_ref[...], sg_ref[...], lam_init) for h in range(DIFF_H)]
    o_ref[0] = jnp.concatenate(heads, axis=1).astype(o_ref.dtype)


def _dec_diff(page_table, qd, pool, k_new, v_new, lam_qk_o, sub_g, *, n_new, lam_init):
    b, n_pages = page_table.shape
    cfg = dict(n_pages=n_pages, n_new=n_new, lam_init=lam_init)
    page = lambda i: pl.BlockSpec((1, PAGE_SIZE, 2, DIFF_H, 2 * HD), lambda bi, pt: (pt[bi, i], 0, 0, 0, 0))
    per_b = lambda shape: pl.BlockSpec((1,) + shape, lambda bi, pt: (bi,) + (0,) * len(shape))
    const = lambda shape: pl.BlockSpec(shape, lambda bi, pt: (0,) * len(shape))
    width = 2 * DIFF_H * HD
    grid_spec = pltpu.PrefetchScalarGridSpec(
        num_scalar_prefetch=1, grid=(b,),
        in_specs=[page(i) for i in range(n_pages)]
                 + [per_b((DEC_ROWS, width)), per_b((DIFF_H * n_new, 2 * HD)), per_b((DIFF_H * n_new, 2 * HD)),
                    const((4, HD)), const((1, 2 * HD))],
        out_specs=per_b((DEC_ROWS, width)),
        scratch_shapes=[pltpu.VMEM((2 * DIFF_H * DEC_ROWS, LANES), f32)])
    return pl.pallas_call(
        functools.partial(_dec_diff_body, cfg), grid_spec=grid_spec,
        out_shape=jax.ShapeDtypeStruct((b, DEC_ROWS, width), bf16),
        compiler_params=_cparams("parallel"),
        name="decode_diff",
    )(page_table, *([pool] * n_pages), qd, k_new, v_new, lam_qk_o, sub_g)


def _pad_rows(a, n):
    return jnp.pad(a, ((0, 0), (0, n - a.shape[1])) + ((0, 0),) * (a.ndim - 2))


def _trunk(x, mods, qpos0, page_table, caches, wts, params):
    (b_f, w_pool, lam_qk, subln_g, norm_mix_g, norm_ffn_g, final_g) = params
    b, t, d = x.shape
    m = b * t
    decode = caches is not None
    depth = len(mods)
    tm = min(256, m)
    x2d = x.reshape(m, d)
    pos = qpos0 + np.tile(np.arange(t), b)
    tables = _rope_tables(pos)

    if decode:
        tiles_per_mod = 1
        tm_mlp, mlp_tiles_per_mod = tm, 1
        expand = lambda v: jnp.repeat(v, t, axis=0).reshape(m // tm, tm, d)
    else:
        tiles_per_mod = t // tm
        tm_mlp = min(512, t)
        mlp_tiles_per_mod = t // tm_mlp
        expand = lambda v: v.reshape(b, 1, d)

    per_seq = lambda a, rows=DEC_ROWS: _pad_rows(a.reshape(b, t, a.shape[-1]), rows)
    outs = {}
    for li in range(depth):
        sh1, sc1, gt1, sh2, sc2, gt2 = [expand(v) for v in jnp.split(mods[li], 6, axis=-1)]
        if li % 2 == 0:
            e = li // 2
            bias = jnp.pad(b_f[e].reshape(1, FOX_H), ((0, 0), (0, LANES - FOX_H)))
            po = _proj(x2d, norm_mix_g[li], sc1, sh1, tiles_per_mod, wts["in_even"][e], tables, bias,
                       _even_plan(), EVEN_OUT_DEFS, tm)
            logf = po[O_LOGF][:, :FOX_H].reshape(b, t, FOX_H)
            outs.setdefault("cmp", []).append(po[O_KVC].reshape(b, t, 2, NSA_KVH, HD))
            outs.setdefault("slc", []).append(po[O_KVS].reshape(b, t, 2, NSA_KVH, HD))
            outs.setdefault("fox", []).append(po[O_KVF].reshape(b, t, 2, FOX_H, HD))
            outs.setdefault("logf", []).append(logf)
            kvw_new = po[O_KVW].reshape(b, t, 2, NSA_KVH, HD)
            if not decode:
                outs.setdefault("win", []).append(kvw_new[:, t - min(WINDOW, t):])
                qn, qf = po[O_QN].reshape(b, t, NSA_H * HD), po[O_QF].reshape(b, t, FOX_H * HD)
                cmp = _pad_rows(_compress(po[O_KVC].reshape(b, t, 4 * HD), w_pool[e]), LANES)
                ck, cv = cmp[:, :, :LANES].astype(bf16), cmp[:, :, LANES:].astype(bf16)
                o_cmp, mb = _cmp_topk(qn, ck, cv, tq=min(128, t), nb_valid=t // CMP_BLOCK)
                lane_slab = lambda oi, w: po[oi].reshape(b, t, w)
                o_s = _flash("slc", qn, lane_slab(O_KS, LANES), lane_slab(O_VS, LANES), [mb],
                             tq=min(128, t), tk=min(FLASH_TK, t), out_dtype=f32)
                o_w = _flash("win", qn, lane_slab(O_KW, LANES), lane_slab(O_VW, LANES), [],
                             tq=min(256, t), tk=min(512, t), out_dtype=f32)
                tk_f = min(FLASH_TK, t)
                f_t = _cumsum_lanes(jnp.transpose(logf, (0, 2, 1)), min(512, t))
                fk = jnp.transpose(f_t.reshape(b, FOX_H // 2, 2, t // tk_f, tk_f), (0, 1, 3, 2, 4))
                fq = f_t.reshape(b, FOX_H // 2, 2, t, 1)
                o_f = _flash("fox", qf, lane_slab(O_KF, 4 * LANES), lane_slab(O_VF, 4 * LANES), [fk, fq],
                             tq=min(512, t), tk=tk_f, out_dtype=bf16)
                acts = [a.reshape(m, a.shape[-1]) for a in (o_cmp, o_s, o_w)] + [po[O_GATE], o_f.reshape(m, -1)]
            else:
                c_cmp, c_slc, c_win, c_fox, c_logf, _ = caches
                outs.setdefault("win", []).append(jnp.concatenate([c_win[e], kvw_new], axis=1)[:, t:])
                kt_view = lambda c: jnp.transpose(c, (0, 2, 3, 4, 1))
                qn = per_seq(po[O_QN])
                o_cmp, o_s, o_w = _dec_nsa(page_table, qn, kt_view(c_cmp[e]), kt_view(c_slc[e]),
                                           kt_view(c_win[e]), per_seq(po[O_KVS]), per_seq(po[O_KVW]),
                                           w_pool[e], qpos0=qpos0, n_new=t)
                qf = po[O_QF].reshape(b, t, 1, FOX_H * HD)
                head_of_lane = np.arange(FOX_H * HD) // HD
                own = jnp.asarray(head_of_lane[None, :] == np.arange(FOX_H)[:, None], dtype=bf16)
                qbd = (qf * own[None, None]).reshape(b, t * FOX_H, FOX_H * HD)
                lf_new = jnp.pad(jnp.transpose(logf, (0, 2, 1)), ((0, 0), (0, 0), (0, LANES - t)))
                o_f = _dec_fox(page_table, qbd, kt_view(c_fox[e]), jnp.transpose(c_logf[e], (0, 2, 1)),
                               per_seq(po[O_KVF]), lf_new, n_new=t)
                acts = [a[:, :t].reshape(m, a.shape[-1]) for a in (o_cmp, o_s, o_w)]
                acts += [po[O_GATE], o_f[:, :t].reshape(m, -1)]
            x2d = _outproj(_outproj_even_body, x2d, gt1, tiles_per_mod, acts, wts["out_even"][e], tm,
                           "outproj_even")
        else:
            o = li // 2
            po = _proj(x2d, norm_mix_g[li], sc1, sh1, tiles_per_mod, wts["in_odd"][o], tables,
                       jnp.zeros((1, LANES), f32), _odd_plan(), ODD_OUT_DEFS, tm)
            outs.setdefault("diff", []).append(po[O_KVD].reshape(b, t, 2, DIFF_H, 2 * HD))
            lam_init = 0.8 - 0.6 * math.exp(-0.3 * li)
            sub_g = subln_g[o].reshape(1, 2 * HD)
            width = 2 * DIFF_H * HD
            if not decode:
                o_d = _flash("diff", po[O_QD].reshape(b, t, width), po[O_KD].reshape(b, t, width),
                             po[O_VD].reshape(b, t, width), [lam_qk[o], sub_g], tq=min(512, t), tk=min(FLASH_TK, t),
                             out_dtype=bf16, lam_init=lam_init)
            else:
                kvd = po[O_KVD].reshape(b, t, 2, DIFF_H, 2 * HD)
                rows_new = lambda a: a.reshape(b, t * DIFF_H, 2 * HD)
                o_d = _dec_diff(page_table, per_seq(po[O_QD]), caches[5][o], rows_new(kvd[:, :, 0]),
                                rows_new(kvd[:, :, 1]), lam_qk[o], sub_g, n_new=t, lam_init=lam_init)[:, :t]
            x2d = _outproj(_outproj_odd_body, x2d, gt1, tiles_per_mod, [o_d.reshape(m, d)],
                           wts["out_odd"][o], tm, "outproj_odd")
        x2d = _mlp(x2d, norm_ffn_g[li], sc2, sh2, gt2, mlp_tiles_per_mod, wts["ff_in"][li], wts["ff_out"][li],
                   final_g, li == depth - 1, tm_mlp, 1024)
    stack = lambda name: jnp.stack(outs[name], 0)
    return (x2d.reshape(b, t, d), stack("cmp"), stack("slc"), stack("win"), stack("fox"), stack("logf"),
            stack("diff"))


def kernel(x_prompt, x_sample, cache_nsa_cmp_kv, cache_nsa_slc_kv, cache_nsa_win_kv, cache_fox_kv,
           cache_fox_logf, cache_diff_kv, page_table, c_prompt, c_sample, w_in_even, b_f, w_pool,
           w_out_even, w_in_odd, lam_qk, subln_g, w_out_odd, norm_mix_g, norm_ffn_g, w_ada, b_ada,
           w_ff_in, w_ff_out, final_g):
    depth = w_ada.shape[0]
    nb_p, nb_s = c_prompt.shape[0], c_sample.shape[0]
    past_len = page_table.shape[1] * PAGE_SIZE

    perm_rows = lambda w: jnp.concatenate(
        [w[:NSA_H * HD].reshape(NSA_H, HD, -1)[NSA_PERM, :, :].reshape(NSA_H * HD, -1), w[NSA_H * HD:]], axis=0)
    wts = dict(
        in_even=[_even_weight(w_in_even[e]) for e in range(w_in_even.shape[0])],
        out_even=[perm_rows(w_out_even[e]).astype(bf16) for e in range(w_out_even.shape[0])],
        in_odd=[w_in_odd[o].astype(bf16) for o in range(w_in_odd.shape[0])],
        out_odd=[w_out_odd[o].astype(bf16) for o in range(w_out_odd.shape[0])],
        ff_in=[w_ff_in[l].astype(bf16) for l in range(depth)],
        ff_out=[w_ff_out[l].astype(bf16) for l in range(depth)],
    )
    rows = -(-(nb_p + nb_s) // SUBLANES) * SUBLANES
    c_all = jnp.pad(jnp.concatenate([c_prompt, c_sample], axis=0), ((0, rows - nb_p - nb_s), (0, 0)))
    mod_all = _ada(c_all, w_ada.astype(bf16), b_ada)
    mods_p = [mod_all[l, :nb_p] for l in range(depth)]
    mods_s = [mod_all[l, nb_p:nb_p + nb_s] for l in range(depth)]

    params = (b_f, w_pool, lam_qk, subln_g, norm_mix_g, norm_ffn_g, final_g)
    caches = (cache_nsa_cmp_kv, cache_nsa_slc_kv, cache_nsa_win_kv, cache_fox_kv, cache_fox_logf, cache_diff_kv)
    y_p, p_cmp, p_slc, p_win, p_fox, p_logf, p_diff = _trunk(x_prompt, mods_p, 0, None, None, wts, params)
    y_s, s_cmp, s_slc, s_win, s_fox, s_logf, s_diff = _trunk(x_sample, mods_s, past_len, page_table, caches,
                                                             wts, params)
    return (y_p, y_s, p_cmp, s_cmp, p_slc, s_slc, p_win, s_win, p_fox, s_fox, p_logf, s_logf, p_diff, s_diff)
```

```python
import functools
import math

import numpy as np
import jax
import jax.numpy as jnp
from jax import lax
from jax.experimental import pallas as pl
from jax.experimental.pallas import tpu as pltpu

HD = 64
ROPE_DIM = HD // 4
ROPE_HALF = ROPE_DIM // 2
ROPE_THETA = 500000.0
NSA_H = 8
NSA_KVH = 2
CMP_BLOCK = 64
CMP_SHIFT = 6
N_SEL = 16
WINDOW = 512
FOX_H = 8
DIFF_H = 8
EPS = 1e-6
PAGE_SIZE = 128

LANES = 128
SUBLANES = 8
VMEM_LIMIT_BYTES = 48 * 1024 * 1024

NEG_MASK = -1e30
NEG_BLOCK = -1e9
PAGES_PER_STEP = 4
FLASH_TK = 1024

NSA_PERM = (0, 4, 1, 5, 2, 6, 3, 7)

f32 = jnp.float32
bf16 = jnp.bfloat16


def _cparams(*sem):
    return pltpu.CompilerParams(dimension_semantics=sem, vmem_limit_bytes=VMEM_LIMIT_BYTES)


def _dot(a, b):
    return jnp.dot(a, b, preferred_element_type=f32)


def _dot_nt(a, b):
    return lax.dot_general(a, b, (((1,), (1,)), ((), ())), preferred_element_type=f32)


def _rms_mod(x, g, sc, sh):
    y = x * lax.rsqrt(jnp.mean(x * x, axis=-1, keepdims=True) + EPS)
    return (y * g) * (1.0 + sc) + sh


def _split_dot(x, w, parts):
    acc = None
    for _ in range(parts):
        piece = x.astype(bf16)
        term = _dot(piece, w)
        acc = term if acc is None else acc + term
        x = x - piece.astype(f32)
    return acc


def _ada_body(c_ref, w_ref, b_ref, o_ref):
    c = c_ref[...]
    a = (c * jax.nn.sigmoid(c)).astype(bf16)
    o_ref[0] = _dot(a, w_ref[0]) + b_ref[0]


def _ada(c_all, w_ada_b, b_ada):
    depth, d, n = w_ada_b.shape
    rows = c_all.shape[0]
    tn = n // 4
    return pl.pallas_call(
        _ada_body,
        out_shape=jax.ShapeDtypeStruct((depth, rows, n), f32),
        grid=(depth, n // tn),
        in_specs=[pl.BlockSpec((rows, d), lambda l, j: (0, 0)),
                  pl.BlockSpec((1, d, tn), lambda l, j: (l, 0, j)),
                  pl.BlockSpec((1, 1, tn), lambda l, j: (l, 0, j))],
        out_specs=pl.BlockSpec((1, rows, tn), lambda l, j: (l, 0, j)),
        compiler_params=_cparams("parallel", "parallel"),
        name="ada_mod",
    )(c_all, w_ada_b, b_ada.reshape(depth, 1, n))


def _rope_group(a, cos, sa, sb):
    return a * cos + pltpu.roll(a, LANES - ROPE_HALF, 1) * sa + pltpu.roll(a, ROPE_HALF, 1) * sb


def _proj_body(plan, x_ref, g_ref, sc_ref, sh_ref, w_ref, cos_ref, sa_ref, sb_ref, bias_ref, *out_refs):
    h = _rms_mod(x_ref[...], g_ref[...], sc_ref[0], sh_ref[0]).astype(bf16)
    cos, sa, sb = cos_ref[...], sa_ref[...], sb_ref[...]
    for src, groups in plan:
        width = LANES * len(groups)
        acc = _dot(h, w_ref[:, src:src + width])
        for gi, (kind, scale, dests) in enumerate(groups):
            a = acc[:, gi * LANES:(gi + 1) * LANES]
            if kind == "rope":
                a = _rope_group(a, cos, sa, sb)
            elif kind == "sigmoid":
                a = jax.nn.sigmoid(a)
            elif kind == "logsig":
                z = a + bias_ref[...]
                a = jnp.minimum(z, 0.0) - jnp.log1p(jnp.exp(-jnp.abs(z)))
            if scale != 1.0:
                a = a * scale
            for oi, col in dests:
                if len(out_refs[oi].shape) == 3:
                    out_refs[oi][0, col:col + LANES, :] = a.T.astype(out_refs[oi].dtype)
                else:
                    out_refs[oi][:, col:col + LANES] = a.astype(out_refs[oi].dtype)


def _proj(x2d, g, sc, sh, tiles_per_mod, w_b, tables, bias, plan, out_defs, tm, seq_len):
    m, d = x2d.shape
    n = w_b.shape[1]
    cos, sa, sb = tables
    mod_rows = sc.shape[1]
    row = lambda i: (i, 0)
    const = lambda i: (0, 0)
    mod = lambda i: (i // tiles_per_mod, 0, 0)
    tiles_per_seq = seq_len // tm
    out_shape, out_specs = [], []
    for w, dt, position_minor in out_defs:
        if position_minor:
            out_shape.append(jax.ShapeDtypeStruct((m // seq_len, w, seq_len), dt))
            out_specs.append(pl.BlockSpec((1, w, tm), lambda i: (i // tiles_per_seq, 0, i % tiles_per_seq)))
        else:
            out_shape.append(jax.ShapeDtypeStruct((m, w), dt))
            out_specs.append(pl.BlockSpec((tm, w), row))
    return pl.pallas_call(
        functools.partial(_proj_body, plan),
        out_shape=out_shape,
        grid=(m // tm,),
        in_specs=[pl.BlockSpec((tm, d), row),
                  pl.BlockSpec((1, d), const),
                  pl.BlockSpec((1, mod_rows, d), mod),
                  pl.BlockSpec((1, mod_rows, d), mod),
                  pl.BlockSpec((d, n), const, pipeline_mode=pl.Buffered(1)),
                  pl.BlockSpec((tm, LANES), row),
                  pl.BlockSpec((tm, LANES), row),
                  pl.BlockSpec((tm, LANES), row),
                  pl.BlockSpec((1, LANES), const)],
        out_specs=out_specs,
        compiler_params=_cparams("parallel"),
        name="norm_mod_proj",
    )(x2d, g.reshape(1, d), sc, sh, w_b, cos, sa, sb, bias)


def _rope_tables(pos):
    inv = np.float32(ROPE_THETA) ** (-np.arange(ROPE_HALF, dtype=np.float32) / np.float32(ROPE_HALF))
    ang = (pos.astype(np.float32)[:, None] * inv[None, :]).astype(np.float64)
    n = pos.shape[0]
    cos = np.ones((n, HD), np.float64)
    sa = np.zeros((n, HD), np.float64)
    sb = np.zeros((n, HD), np.float64)
    cos[:, :ROPE_HALF] = np.cos(ang)
    cos[:, ROPE_HALF:ROPE_DIM] = np.cos(ang)
    sa[:, :ROPE_HALF] = -np.sin(ang)
    sb[:, ROPE_HALF:ROPE_DIM] = np.sin(ang)
    tile = lambda t: jnp.asarray(np.tile(t, (1, LANES // HD)).astype(np.float32))
    return tile(cos), tile(sa), tile(sb)


EV_QN, EV_KVC, EV_KVS, EV_KVW, EV_QF, EV_KVF, EV_GATE, EV_FF = 0, 512, 768, 1024, 1280, 1792, 2816, 4352


def _even_weight(w):
    d = w.shape[0]
    sizes = [NSA_H * HD, 2 * NSA_KVH * HD, 2 * NSA_KVH * HD, 2 * NSA_KVH * HD, 3 * NSA_H, FOX_H * HD,
             2 * FOX_H * HD, FOX_H]
    offs = np.concatenate([[0], np.cumsum(sizes)])
    q_n, kv_c, kv_s, kv_w, gates, q_f, kv_f, f_f = [w[:, offs[i]:offs[i + 1]] for i in range(8)]
    q_n = q_n.reshape(d, NSA_H, HD)[:, NSA_PERM, :].reshape(d, NSA_H * HD)
    gates = gates.reshape(d, NSA_H, 3)[:, NSA_PERM, :]
    gates = jnp.repeat(jnp.transpose(gates, (0, 2, 1))[..., None], HD, axis=-1).reshape(d, 3 * NSA_H * HD)
    f_f = jnp.pad(f_f, ((0, 0), (0, LANES - FOX_H)))
    return jnp.concatenate([q_n, kv_c, kv_s, kv_w, q_f, kv_f, gates, f_f], axis=1).astype(bf16)


(O_QN, O_KVC, O_KVS, O_KS, O_VS, O_KVW, O_KW, O_VW, O_QF, O_KVF, O_KF, O_VF, O_GATE, O_LOGF, O_KVC_ROWS) = range(15)


def _even_out_defs(position_minor):
    pm = position_minor
    return [(512, bf16, False), (256, f32, pm), (256, f32, pm), (128, bf16, False), (128, bf16, False),
            (256, f32, pm), (128, bf16, False), (128, bf16, False), (512, bf16, False), (1024, f32, pm),
            (512, bf16, False), (512, bf16, False), (1536, f32, False), (128, f32, False), (256, f32, False)]


def _even_plan():
    scale = HD ** -0.5
    plan = [(EV_QN, [("rope", scale, [(O_QN, LANES * i)]) for i in range(4)]),
            (EV_KVC, [("rope", 1.0, [(O_KVC, 0), (O_KVC_ROWS, 0)]),
                      ("plain", 1.0, [(O_KVC, LANES), (O_KVC_ROWS, LANES)])]),
            (EV_KVS, [("rope", 1.0, [(O_KVS, 0), (O_KS, 0)]), ("plain", 1.0, [(O_KVS, LANES), (O_VS, 0)])]),
            (EV_KVW, [("rope", 1.0, [(O_KVW, 0), (O_KW, 0)]), ("plain", 1.0, [(O_KVW, LANES), (O_VW, 0)])]),
            (EV_QF, [("plain", scale, [(O_QF, LANES * i)]) for i in range(4)]),
            (EV_KVF, [("plain", 1.0, [(O_KVF, LANES * i), (O_KF, LANES * i)]) for i in range(4)]),
            (EV_KVF + 512, [("plain", 1.0, [(O_KVF, 512 + LANES * i), (O_VF, LANES * i)]) for i in range(4)])]
    for j in range(3):
        plan.append((EV_GATE + 512 * j, [("sigmoid", 1.0, [(O_GATE, 512 * j + LANES * i)]) for i in range(4)]))
    plan.append((EV_FF, [("logsig", 1.0, [(O_LOGF, 0)])]))
    return plan


(O_QD, O_KVD, O_KD, O_VD) = range(4)
ODD_OUT_DEFS = [(1024, bf16, False), (2048, f32, False), (1024, bf16, False), (1024, bf16, False)]


def _odd_plan():
    scale = HD ** -0.5
    plan = []
    for c in range(2):
        plan.append((512 * c, [("rope", scale, [(O_QD, 512 * c + LANES * i)]) for i in range(4)]))
    for c in range(2):
        plan.append((1024 + 512 * c,
                     [("rope", 1.0, [(O_KVD, 512 * c + LANES * i), (O_KD, 512 * c + LANES * i)]) for i in range(4)]))
    for c in range(2):
        plan.append((2048 + 512 * c,
                     [("plain", 1.0, [(O_KVD, 1024 + 512 * c + LANES * i), (O_VD, 512 * c + LANES * i)])
                      for i in range(4)]))
    return plan


def _outproj_even_body(x_ref, gt_ref, oc_ref, os_ref, ow_ref, gates_ref, of_ref, w_ref, o_ref):
    nw = NSA_H * HD
    o_nsa = (gates_ref[:, 0:nw] * oc_ref[...] + gates_ref[:, nw:2 * nw] * os_ref[...]
             + gates_ref[:, 2 * nw:3 * nw] * ow_ref[...])
    a = jnp.concatenate([o_nsa.astype(bf16), of_ref[...]], axis=1)
    o_ref[...] = x_ref[...] + gt_ref[0] * _dot(a, w_ref[...])


def _outproj_odd_body(x_ref, gt_ref, od_ref, w_ref, o_ref):
    o_ref[...] = x_ref[...] + gt_ref[0] * _dot(od_ref[...], w_ref[...])


def _outproj(body, x2d, gt, tiles_per_mod, acts, w_b, tm, name):
    m, d = x2d.shape
    row = lambda i: (i, 0)
    mod = lambda i: (i // tiles_per_mod, 0, 0)
    return pl.pallas_call(
        body,
        out_shape=jax.ShapeDtypeStruct((m, d), f32),
        grid=(m // tm,),
        in_specs=[pl.BlockSpec((tm, d), row), pl.BlockSpec((1, gt.shape[1], d), mod)]
                 + [pl.BlockSpec((tm, a.shape[1]), row) for a in acts]
                 + [pl.BlockSpec(w_b.shape, lambda i: (0, 0), pipeline_mode=pl.Buffered(1))],
        out_specs=pl.BlockSpec((tm, d), row),
        compiler_params=_cparams("parallel"),
        name=name,
    )(x2d, gt, *acts, w_b)


def _mlp_body(final, x_ref, g_ref, sc_ref, sh_ref, gt_ref, w1_ref, w2_ref, fg_ref, o_ref, h_ref, acc_ref):
    f = pl.program_id(1)

    @pl.when(f == 0)
    def _():
        h_ref[...] = _rms_mod(x_ref[...], g_ref[...], sc_ref[0], sh_ref[0]).astype(bf16)
        acc_ref[...] = jnp.zeros_like(acc_ref)

    a = jnp.maximum(_dot(h_ref[...], w1_ref[...]), 0.0)
    acc_ref[...] += _dot((a * a).astype(bf16), w2_ref[...])

    @pl.when(f == pl.num_programs(1) - 1)
    def _():
        y = x_ref[...] + gt_ref[0] * acc_ref[...]
        if final:
            y = y * lax.rsqrt(jnp.mean(y * y, axis=-1, keepdims=True) + EPS) * fg_ref[...]
        o_ref[...] = y


def _mlp(x2d, g, sc, sh, gt, tiles_per_mod, w1_b, w2_b, final_g, final, tm, tf):
    m, d = x2d.shape
    dff = w1_b.shape[1]
    mod_rows = sc.shape[1]
    row = lambda i, f: (i, 0)
    const = lambda i, f: (0, 0)
    mod = lambda i, f: (i // tiles_per_mod, 0, 0)
    return pl.pallas_call(
        functools.partial(_mlp_body, final),
        out_shape=jax.ShapeDtypeStruct((m, d), f32),
        grid=(m // tm, dff // tf),
        in_specs=[pl.BlockSpec((tm, d), row), pl.BlockSpec((1, d), const),
                  pl.BlockSpec((1, mod_rows, d), mod), pl.BlockSpec((1, mod_rows, d), mod),
                  pl.BlockSpec((1, mod_rows, d), mod),
                  pl.BlockSpec((d, tf), lambda i, f: (0, f)), pl.BlockSpec((tf, d), lambda i, f: (f, 0)),
                  pl.BlockSpec((1, d), const)],
        out_specs=pl.BlockSpec((tm, d), row),
        scratch_shapes=[pltpu.VMEM((tm, d), bf16), pltpu.VMEM((tm, d), f32)],
        compiler_params=_cparams("parallel", "arbitrary"),
        name="relu2_mlp",
    )(x2d, g.reshape(1, d), sc, sh, gt, w1_b, w2_b, final_g.reshape(1, d))


def _compress_body(x_ref, wp_ref, o_ref):
    wp = wp_ref[...]
    rows = []
    for blk in range(x_ref.shape[1] // CMP_BLOCK):
        rows.append(jnp.sum(x_ref[0, blk * CMP_BLOCK:(blk + 1) * CMP_BLOCK, :] * wp, axis=0, keepdims=True))
    o_ref[0] = jnp.concatenate(rows, axis=0)


def _compress(kv, w_pool_e):
    b, t, c = kv.shape
    wp = jnp.repeat(jnp.transpose(w_pool_e, (1, 0, 2))[..., None], HD, axis=-1).reshape(CMP_BLOCK, c)
    rows_in = PAGES_PER_STEP * PAGE_SIZE
    rows_out = rows_in // CMP_BLOCK
    return pl.pallas_call(
        _compress_body,
        out_shape=jax.ShapeDtypeStruct((b, t // CMP_BLOCK, c), f32),
        grid=(b, t // rows_in),
        in_specs=[pl.BlockSpec((1, rows_in, c), lambda bi, g: (bi, g, 0)),
                  pl.BlockSpec((CMP_BLOCK, c), lambda bi, g: (0, 0))],
        out_specs=pl.BlockSpec((1, rows_out, c), lambda bi, g: (bi, g, 0)),
        compiler_params=_cparams("parallel", "parallel"),
        name="nsa_compress",
    )(kv, wp)


def _cumsum_body(x_ref, u_ref, o_ref, carry_ref):
    @pl.when(pl.program_id(1) == 0)
    def _():
        carry_ref[...] = jnp.zeros_like(carry_ref)

    cs = _split_dot(x_ref[0], u_ref[...], 3) + carry_ref[...]
    o_ref[0] = cs
    carry_ref[...] = cs[:, cs.shape[1] - 1:]


def _upper_ones(n):
    return jnp.asarray(np.triu(np.ones((n, n), np.float32)), dtype=bf16)


def _cumsum_lanes(x_t, tc):
    b, h, l = x_t.shape
    return pl.pallas_call(
        _cumsum_body,
        out_shape=jax.ShapeDtypeStruct((b, h, l), f32),
        grid=(b, l // tc),
        in_specs=[pl.BlockSpec((1, h, tc), lambda bi, c: (bi, 0, c)),
                  pl.BlockSpec((tc, tc), lambda bi, c: (0, 0))],
        out_specs=pl.BlockSpec((1, h, tc), lambda bi, c: (bi, 0, c)),
        scratch_shapes=[pltpu.VMEM((h, 1), f32)],
        compiler_params=_cparams("parallel", "arbitrary"),
        name="logf_cumsum",
    )(x_t, _upper_ones(tc))


def _stack_heads(q, qs_ref, n_slab, tq):
    lane = lax.broadcasted_iota(jnp.int32, (tq, LANES), 1)
    low = lane < HD
    for s in range(n_slab):
        slab = q[:, s * LANES:(s + 1) * LANES].astype(f32)
        qs_ref[(2 * s) * tq:(2 * s + 1) * tq, 0:LANES] = jnp.where(low, slab, 0.0).astype(qs_ref.dtype)
        qs_ref[(2 * s + 1) * tq:(2 * s + 2) * tq, 0:LANES] = jnp.where(low, 0.0, slab).astype(qs_ref.dtype)


def _unstack_heads(o, n_slab, tq):
    lane = lax.broadcasted_iota(jnp.int32, (tq, LANES), 1)
    low = lane < HD
    slabs = [jnp.where(low, o[(2 * s) * tq:(2 * s + 1) * tq], o[(2 * s + 1) * tq:(2 * s + 2) * tq])
             for s in range(n_slab)]
    return slabs[0] if n_slab == 1 else jnp.concatenate(slabs, axis=1)


def _masked_softmax(s, valid):
    sm = jnp.where(valid, s, NEG_MASK)
    m = jnp.max(sm, axis=1, keepdims=True)
    p = jnp.where(valid, jnp.exp(sm - m), 0.0)
    return p / jnp.maximum(jnp.sum(p, axis=1, keepdims=True), 1e-30)


def _select_blocks(p, tg, tq, nb_valid):
    n = lax.broadcasted_iota(jnp.int32, (1, LANES), 1)
    cur = jnp.right_shift(tg, CMP_SHIFT)
    n_cand = jnp.minimum(cur, nb_valid)
    out = []
    for g in range(NSA_KVH):
        imp = p[g * tq:(g + 1) * tq]
        for s_i in range(1, NSA_H // 2):
            imp = imp + p[(2 * s_i + g) * tq:(2 * s_i + g + 1) * tq]
        work = jnp.where(n < n_cand, imp, -1.0)
        sel = n == cur
        for it in range(N_SEL - 1):
            hit = n == jnp.argmax(work, axis=1, keepdims=True).astype(jnp.int32)
            sel = jnp.logical_or(sel, jnp.logical_and(hit, it < n_cand))
            work = jnp.where(hit, -2.0, work)
        out.append(jnp.where(sel, 0.0, NEG_BLOCK))
    return out


def _flash_body(cfg, *refs):
    mode, n_slab, tq, tk, n_chunks = cfg["mode"], cfg["n_slab"], cfg["tq"], cfg["tk"], cfg["n_chunks"]
    window = cfg["window"]
    rows = 2 * n_slab * tq
    q_ref, k_ref, v_ref = refs[:3]
    extra = refs[3:-5]
    o_ref, qs_ref, s_ref, m_ref, acc_ref = refs[-5:]
    qi = pl.program_id(2)

    _stack_heads(q_ref[0], qs_ref, n_slab, tq)
    if mode == "slc":
        mb_ref = extra[0]
        for s in range(n_slab):
            for g in range(2):
                qs_ref[(2 * s + g) * tq:(2 * s + g + 1) * tq, LANES:2 * LANES] = mb_ref[0, g]
    m_ref[...] = jnp.full_like(m_ref, NEG_MASK)
    acc_ref[...] = jnp.zeros_like(acc_ref)

    q_lo = qi * tq
    qpos = q_lo + (lax.broadcasted_iota(jnp.int32, (rows, 1), 0) & (tq - 1))

    def scores(c):
        ks = pl.multiple_of(c * tk, tk)
        k = k_ref[0, pl.ds(ks, tk), :]
        if mode == "slc":
            blk = jnp.right_shift(ks + lax.broadcasted_iota(jnp.int32, (tk, LANES), 0), CMP_SHIFT)
            hot = jnp.where(blk == lax.broadcasted_iota(jnp.int32, (tk, LANES), 1), 1.0, 0.0).astype(bf16)
            k = jnp.concatenate([k, hot], axis=1)
        return _dot_nt(qs_ref[...], k)

    def update(s, c, masked):
        ks = pl.multiple_of(c * tk, tk)
        row_bias = 0.0
        if mode == "fox":
            fk_ref, fq_ref = extra
            fk = fk_ref[0, 0, c]
            s = jnp.concatenate([s[0:tq] - fk[0:1], s[tq:2 * tq] - fk[1:2]], axis=0)
            row_bias = jnp.concatenate([fq_ref[0, 0, 0], fq_ref[0, 0, 1]], axis=0)
        if masked:
            kpos = ks + lax.broadcasted_iota(jnp.int32, (1, tk), 1)
            mask = kpos <= qpos
            if window is not None:
                mask = jnp.logical_and(mask, (qpos - kpos) < window)
            s = jnp.where(mask, s, NEG_MASK)
        m_old = m_ref[...]
        m_new = jnp.maximum(m_old, jnp.max(s, axis=1, keepdims=True) + row_bias)
        alpha = jnp.exp(m_old - m_new)
        p = jnp.exp((s + (row_bias - m_new)).astype(bf16))
        v_ones = jnp.concatenate([v_ref[0, pl.ds(ks, tk), :], jnp.ones((tk, LANES), bf16)], axis=1)
        acc_ref[...] = alpha * acc_ref[...] + _dot(p, v_ones)
        m_ref[...] = m_new

    def issue(c, slot):
        s_ref[slot] = scores(c)

    if window is not None:
        c_lo = jnp.maximum(q_lo - window + 1, 0) // tk
        c_last = (q_lo + tq - 1) // tk
        issue(c_lo, 0)
        issue(c_last, 1)
        update(s_ref[0], c_lo, True)

        @pl.when(c_last > c_lo)
        def _():
            update(s_ref[1], c_last, True)
    else:
        n_full = (q_lo + 1) // tk

        issue(0, 0)

        def pair(j, carry):
            c = 2 * j
            issue(c + 1, 1)
            update(s_ref[0], c, False)
            issue(c + 2, 0)
            update(s_ref[1], c + 1, False)
            return carry

        lax.fori_loop(0, n_full // 2, pair, 0)

        @pl.when((n_full & 1) == 1)
        def _():
            issue(n_full, 1)
            update(s_ref[0], n_full - 1, False)
            update(s_ref[1], n_full, True)

        @pl.when((n_full & 1) == 0)
        def _():
            update(s_ref[0], n_full, True)

    acc = acc_ref[...]
    o = acc[:, :LANES] / jnp.maximum(acc[:, LANES:], 1e-30)
    if mode == "diff":
        lq_ref, sg_ref = extra
        o_ref[0] = _diff_combine(o[0:tq], o[tq:2 * tq], lq_ref[...], sg_ref[...], cfg["lam_init"]).astype(o_ref.dtype)
    else:
        o_ref[0] = _unstack_heads(o, n_slab, tq).astype(o_ref.dtype)


def _diff_combine(o1, o2, lp, sub_g, lam_init):
    lam = (jnp.exp(jnp.sum(lp[0:1] * lp[1:2], axis=1, keepdims=True))
           - jnp.exp(jnp.sum(lp[2:3] * lp[3:4], axis=1, keepdims=True)) + lam_init)
    d = o1 - lam * o2
    d = d * lax.rsqrt(jnp.mean(d * d, axis=-1, keepdims=True) + EPS) * sub_g
    return d * (1.0 - lam_init)


def _flash(mode, q, k, v, extra, *, tq, tk, out_dtype, lam_init=0.0):
    b, t_q, cq = q.shape
    t_k = k.shape[1]
    n_slab = 4 if mode in ("slc", "win") else 1
    n_grid_slab = cq // (LANES * n_slab)
    n_chunks = t_k // tk
    rows = 2 * n_slab * tq
    ck = 2 * LANES if mode == "slc" else LANES
    assert tk % tq == 0
    if mode == "win":
        assert tk == WINDOW
    score_slots = (2, rows, tk)
    cfg = dict(mode=mode, n_slab=n_slab, tq=tq, tk=tk, n_chunks=n_chunks,
               window=WINDOW if mode == "win" else None, lam_init=lam_init)
    qw = LANES * n_slab
    in_specs = [pl.BlockSpec((1, tq, qw), lambda bi, s, i: (bi, i, s)),
                pl.BlockSpec((1, t_k, LANES), lambda bi, s, i: (bi, 0, s)),
                pl.BlockSpec((1, t_k, LANES), lambda bi, s, i: (bi, 0, s))]
    if mode == "fox":
        in_specs += [pl.BlockSpec((1, 1, n_chunks, 2, tk), lambda bi, s, i: (bi, s, 0, 0, 0)),
                     pl.BlockSpec((1, 1, 2, tq, 1), lambda bi, s, i: (bi, s, 0, i, 0))]
    elif mode == "slc":
        in_specs += [pl.BlockSpec((1, 2, tq, LANES), lambda bi, s, i: (bi, 0, i, 0))]
    elif mode == "diff":
        in_specs += [pl.BlockSpec((4, HD), lambda bi, s, i: (0, 0)),
                     pl.BlockSpec((1, LANES), lambda bi, s, i: (0, 0))]
    return pl.pallas_call(
        functools.partial(_flash_body, cfg),
        out_shape=jax.ShapeDtypeStruct((b, t_q, cq), out_dtype),
        grid=(b, n_grid_slab, t_q // tq),
        in_specs=in_specs,
        out_specs=pl.BlockSpec((1, tq, qw), lambda bi, s, i: (bi, i, s)),
        scratch_shapes=[pltpu.VMEM((rows, ck), bf16), pltpu.VMEM(score_slots, f32), pltpu.VMEM((rows, 1), f32),
                        pltpu.VMEM((rows, 2 * LANES), f32)],
        compiler_params=_cparams("parallel", "parallel", "parallel"),
        name="flash_" + mode,
    )(q, k, v, *extra)


def _cmp_body(tq, nb_valid, q_ref, ck_ref, cv_ref, o_ref, mb_ref, qs_ref):
    n_slab = NSA_H // 2
    rows = 2 * n_slab * tq
    qi = pl.program_id(1)
    _stack_heads(q_ref[0], qs_ref, n_slab, tq)
    s = _dot_nt(qs_ref[...], ck_ref[0])
    t = qi * tq + (lax.broadcasted_iota(jnp.int32, (rows, 1), 0) & (tq - 1))
    n = lax.broadcasted_iota(jnp.int32, (1, LANES), 1)
    valid = jnp.logical_and((n + 1) * CMP_BLOCK - 1 <= t, n < nb_valid)
    p = _masked_softmax(s, valid)
    o_ref[0] = _unstack_heads(_dot(p.astype(bf16), cv_ref[0]), n_slab, tq)
    bias = _select_blocks(p, t[0:tq], tq, nb_valid)
    for g in range(NSA_KVH):
        mb_ref[0, g] = bias[g].astype(bf16)


def _cmp_topk(q, ck, cv, *, tq, nb_valid):
    b, t_q, cq = q.shape
    rows = NSA_H * tq
    return pl.pallas_call(
        functools.partial(_cmp_body, tq, nb_valid),
        out_shape=[jax.ShapeDtypeStruct((b, t_q, cq), f32), jax.ShapeDtypeStruct((b, NSA_KVH, t_q, LANES), bf16)],
        grid=(b, t_q // tq),
        in_specs=[pl.BlockSpec((1, tq, cq), lambda bi, i: (bi, i, 0)),
                  pl.BlockSpec((1, LANES, LANES), lambda bi, i: (bi, 0, 0)),
                  pl.BlockSpec((1, LANES, LANES), lambda bi, i: (bi, 0, 0))],
        out_specs=[pl.BlockSpec((1, tq, cq), lambda bi, i: (bi, i, 0)),
                   pl.BlockSpec((1, NSA_KVH, tq, LANES), lambda bi, i: (bi, 0, i, 0))],
        scratch_shapes=[pltpu.VMEM((rows, LANES), bf16)],
        compiler_params=_cparams("parallel", "parallel"),
        name="nsa_cmp_topk",
    )(q, ck, cv)


DEC_ROWS = SUBLANES


def _pad_keys(new, lo, width):
    x = new[:, lo:lo + width]
    return jnp.concatenate([x, jnp.zeros((LANES - x.shape[0], width), f32)], axis=0).astype(bf16)


def _new_key_mask(rows, n_new):
    qi = lax.broadcasted_iota(jnp.int32, (rows, 1), 0) & (DEC_ROWS - 1)
    kj = lax.broadcasted_iota(jnp.int32, (1, LANES), 1)
    return jnp.logical_and(kj <= qi, kj < n_new)


def _softmax2(s1, s2, mask2):
    s2 = jnp.where(mask2, s2, NEG_MASK)
    m = jnp.maximum(jnp.max(s1, axis=1, keepdims=True), jnp.max(s2, axis=1, keepdims=True))
    p1 = jnp.exp(s1 - m)
    p2 = jnp.where(mask2, jnp.exp(s2 - m), 0.0)
    den = jnp.sum(p1, axis=1, keepdims=True) + jnp.sum(p2, axis=1, keepdims=True)
    return p1, p2, jnp.maximum(den, 1e-30)


def _dec_nsa_body(cfg, pt_ref, *refs):
    n_pages, qpos0, n_new, w_buf = cfg["n_pages"], cfg["qpos0"], cfg["n_new"], cfg["w_buf"]
    cmp_pages = refs[:n_pages]
    slc_pages = refs[n_pages:2 * n_pages]
    (q_ref, win_ref, news_ref, neww_ref, wp_ref, e_ref, oht_ref,
     oc_ref, os_ref, ow_ref, qs_ref) = refs[2 * n_pages:]
    tq, n_slab = DEC_ROWS, NSA_H // 2
    rows = NSA_H * tq
    gd = NSA_KVH * HD

    _stack_heads(q_ref[0], qs_ref, n_slab, tq)
    qs = qs_ref[...].astype(bf16)
    t = qpos0 + (lax.broadcasted_iota(jnp.int32, (rows, 1), 0) & (tq - 1))
    n = lax.broadcasted_iota(jnp.int32, (1, LANES), 1)
    mask_new = _new_key_mask(rows, n_new)

    def compressed(kv):
        xt = jnp.concatenate([p[0, kv].reshape(gd, PAGE_SIZE) for p in cmp_pages], axis=1)
        w = jnp.concatenate([wp_ref[kv]] * n_pages, axis=1)
        return _split_dot(xt * w, e_ref[...], 2).astype(bf16)

    ck_t, cv_t = compressed(0), compressed(1)
    nb_valid = n_pages * PAGE_SIZE // CMP_BLOCK
    valid = jnp.logical_and((n + 1) * CMP_BLOCK - 1 <= t, n < nb_valid)
    p = _masked_softmax(_dot(qs, ck_t), valid)
    oc_ref[0] = _unstack_heads(_dot_nt(p.astype(bf16), cv_t), n_slab, tq)

    bias = _select_blocks(p, t[0:tq], tq, nb_valid)
    qq = jnp.concatenate([qs, jnp.concatenate(bias * n_slab, axis=0).astype(bf16)], axis=1)
    ks_t = jnp.concatenate([p_[0, 0].reshape(gd, PAGE_SIZE) for p_ in slc_pages], axis=1).astype(bf16)
    vs_t = jnp.concatenate([p_[0, 1].reshape(gd, PAGE_SIZE) for p_ in slc_pages], axis=1).astype(bf16)
    s1 = _dot(qq, jnp.concatenate([ks_t, oht_ref[...]], axis=0))
    new_s = news_ref[0]
    s2 = _dot_nt(qs, _pad_keys(new_s, 0, gd))
    p1, p2, den = _softmax2(s1, s2, mask_new)
    o = _dot_nt(p1.astype(bf16), vs_t) + _dot(p2.astype(bf16), _pad_keys(new_s, gd, gd))
    os_ref[0] = _unstack_heads(o / den, n_slab, tq)

    kw_t = win_ref[0, 0].reshape(gd, w_buf).astype(bf16)
    vw_t = win_ref[0, 1].reshape(gd, w_buf).astype(bf16)
    kpos = qpos0 - w_buf + lax.broadcasted_iota(jnp.int32, (1, w_buf), 1)
    s1 = jnp.where((t - kpos) < WINDOW, _dot(qs, kw_t), NEG_MASK)
    new_w = neww_ref[0]
    s2 = _dot_nt(qs, _pad_keys(new_w, 0, gd))
    p1, p2, den = _softmax2(s1, s2, mask_new)
    o = _dot_nt(p1.astype(bf16), vw_t) + _dot(p2.astype(bf16), _pad_keys(new_w, gd, gd))
    ow_ref[0] = _unstack_heads(o / den, n_slab, tq)


def _dec_nsa(page_table, qn, cmp_pool, slc_pool, win_t, new_s, new_w, w_pool_e, *, qpos0, n_new):
    b, n_pages = page_table.shape
    w_buf = win_t.shape[-1]
    l_k = n_pages * PAGE_SIZE
    gd = NSA_KVH * HD
    cfg = dict(n_pages=n_pages, qpos0=qpos0, n_new=n_new, w_buf=w_buf)
    wp = jnp.tile(jnp.repeat(jnp.transpose(w_pool_e, (0, 2, 1)), HD, axis=1), (1, 1, PAGE_SIZE // CMP_BLOCK))
    blk_of = np.arange(l_k) // CMP_BLOCK
    e_all = jnp.asarray(blk_of[:, None] == np.arange(LANES)[None, :], dtype=bf16)
    oh_t = jnp.asarray(np.arange(LANES)[:, None] == blk_of[None, :], dtype=bf16)
    page = lambda i: pl.BlockSpec((1, 2, NSA_KVH, HD, PAGE_SIZE), lambda bi, pt: (pt[bi, i], 0, 0, 0, 0))
    per_b = lambda shape: pl.BlockSpec((1,) + shape, lambda bi, pt: (bi,) + (0,) * len(shape))
    const = lambda shape: pl.BlockSpec(shape, lambda bi, pt: (0,) * len(shape))
    grid_spec = pltpu.PrefetchScalarGridSpec(
        num_scalar_prefetch=1, grid=(b,),
        in_specs=[page(i) for i in range(n_pages)] * 2
                 + [per_b((DEC_ROWS, NSA_H * HD)), per_b((2, NSA_KVH, HD, w_buf)), per_b((DEC_ROWS, 2 * gd)),
                    per_b((DEC_ROWS, 2 * gd)), const((2, gd, PAGE_SIZE)), const((l_k, LANES)), const((LANES, l_k))],
        out_specs=[per_b((DEC_ROWS, NSA_H * HD))] * 3,
        scratch_shapes=[pltpu.VMEM((NSA_H * DEC_ROWS, LANES), f32)])
    return pl.pallas_call(
        functools.partial(_dec_nsa_body, cfg), grid_spec=grid_spec,
        out_shape=[jax.ShapeDtypeStruct((b, DEC_ROWS, NSA_H * HD), f32)] * 3,
        compiler_params=_cparams("parallel"),
        name="decode_nsa",
    )(page_table, *([cmp_pool] * n_pages), *([slc_pool] * n_pages), qn, win_t, new_s, new_w, wp, e_all, oh_t)


def _dec_fox_body(cfg, pt_ref, *refs):
    n_pages, n_new = cfg["n_pages"], cfg["n_new"]
    kv_pages = refs[:n_pages]
    lf_pages = refs[n_pages:2 * n_pages]
    qbd_ref, new_ref, lfn_ref, u_ref, o_ref = refs[2 * n_pages:]
    hd_all = FOX_H * HD
    rows = FOX_H * n_new
    qbd = qbd_ref[0]
    u = u_ref[...]

    carry = jnp.zeros((FOX_H, 1), f32)
    f_pages = []
    for p in lf_pages:
        c = _split_dot(p[0], u, 3) + carry
        f_pages.append(c)
        carry = c[:, PAGE_SIZE - 1:]
    f_k = jnp.concatenate(f_pages, axis=1)
    f_new = _split_dot(lfn_ref[0], u, 3)

    k_t = jnp.concatenate([p[0, 0].reshape(hd_all, PAGE_SIZE) for p in kv_pages], axis=1).astype(bf16)
    v_t = jnp.concatenate([p[0, 1].reshape(hd_all, PAGE_SIZE) for p in kv_pages], axis=1).astype(bf16)
    new = new_ref[0]
    s1 = _dot(qbd, k_t)
    s2 = _dot_nt(qbd, _pad_keys(new, 0, hd_all))
    kj = lax.broadcasted_iota(jnp.int32, (1, LANES), 1)
    s1_q, s2_q, m2_q = [], [], []
    for qi in range(n_new):
        f_q = carry + f_new[:, qi:qi + 1]
        s1_q.append(s1[qi * FOX_H:(qi + 1) * FOX_H] + f_q - f_k)
        s2_q.append(s2[qi * FOX_H:(qi + 1) * FOX_H] + f_q - (carry + f_new))
        m2_q.append(jnp.broadcast_to(kj <= qi, (FOX_H, LANES)))
    p1, p2, den = _softmax2(jnp.concatenate(s1_q, axis=0), jnp.concatenate(s2_q, axis=0),
                            jnp.concatenate(m2_q, axis=0))
    o = (_dot_nt(p1.astype(bf16), v_t) + _dot(p2.astype(bf16), _pad_keys(new, hd_all, hd_all))) / den
    head = lax.broadcasted_iota(jnp.int32, (FOX_H, hd_all), 0)
    own = jnp.right_shift(lax.broadcasted_iota(jnp.int32, (FOX_H, hd_all), 1), CMP_SHIFT) == head
    out = [jnp.sum(jnp.where(own, o[qi * FOX_H:(qi + 1) * FOX_H], 0.0), axis=0, keepdims=True)
           for qi in range(n_new)]
    out.append(jnp.zeros((DEC_ROWS - n_new, hd_all), f32))
    o_ref[0] = jnp.concatenate(out, axis=0).astype(o_ref.dtype)


def _dec_fox(page_table, qbd, kv_pool, lf_pool, new_kv, lf_new, *, n_new):
    b, n_pages = page_table.shape
    hd_all = FOX_H * HD
    cfg = dict(n_pages=n_pages, n_new=n_new)
    kv_page = lambda i: pl.BlockSpec((1, 2, FOX_H, HD, PAGE_SIZE), lambda bi, pt: (pt[bi, i], 0, 0, 0, 0))
    lf_page = lambda i: pl.BlockSpec((1, FOX_H, PAGE_SIZE), lambda bi, pt: (pt[bi, i], 0, 0))
    per_b = lambda shape: pl.BlockSpec((1,) + shape, lambda bi, pt: (bi,) + (0,) * len(shape))
    grid_spec = pltpu.PrefetchScalarGridSpec(
        num_scalar_prefetch=1, grid=(b,),
        in_specs=[kv_page(i) for i in range(n_pages)] + [lf_page(i) for i in range(n_pages)]
                 + [per_b((FOX_H * n_new, hd_all)), per_b((DEC_ROWS, 2 * hd_all)), per_b((FOX_H, LANES)),
                    pl.BlockSpec((PAGE_SIZE, PAGE_SIZE), lambda bi, pt: (0, 0))],
        out_specs=per_b((DEC_ROWS, hd_all)))
    return pl.pallas_call(
        functools.partial(_dec_fox_body, cfg), grid_spec=grid_spec,
        out_shape=jax.ShapeDtypeStruct((b, DEC_ROWS, hd_all), bf16),
        compiler_params=_cparams("parallel"),
        name="decode_fox",
    )(page_table, *([kv_pool] * n_pages), *([lf_pool] * n_pages), qbd, new_kv, lf_new, _upper_ones(PAGE_SIZE))


def _dec_diff_body(cfg, pt_ref, *refs):
    n_pages, n_new, lam_init = cfg["n_pages"], cfg["n_new"], cfg["lam_init"]
    pages = refs[:n_pages]
    q_ref, knew_ref, vnew_ref, lq_ref, sg_ref, o_ref, qs_ref = refs[n_pages:]
    tq = DEC_ROWS
    rows = 2 * DIFF_H * tq
    cols = PAGE_SIZE * DIFF_H
    _stack_heads(q_ref[0], qs_ref, DIFF_H, tq)
    row = lax.broadcasted_iota(jnp.int32, (rows, 1), 0)
    row_head = jnp.right_shift(row, 4)
    qs = qs_ref[...].astype(bf16)
    col = lax.broadcasted_iota(jnp.int32, (1, cols), 1)
    own = (col & (DIFF_H - 1)) == row_head

    def part(s, mask, v_rows):
        s = jnp.where(mask, s, NEG_MASK)
        m = jnp.max(s, axis=1, keepdims=True)
        p = jnp.exp(s - m)
        return m, jnp.sum(p, axis=1, keepdims=True), _dot(p.astype(bf16), v_rows)

    parts = []
    for p in pages:
        k_rows = p[0, :, 0].reshape(cols, 2 * HD).astype(bf16)
        v_rows = p[0, :, 1].reshape(cols, 2 * HD).astype(bf16)
        parts.append(part(_dot_nt(qs, k_rows), own, v_rows))
    pad = jnp.zeros((LANES - knew_ref.shape[1], 2 * HD), f32)
    k_new = jnp.concatenate([knew_ref[0], pad], axis=0).astype(bf16)
    v_new = jnp.concatenate([vnew_ref[0], pad], axis=0).astype(bf16)
    lane = lax.broadcasted_iota(jnp.int32, (1, LANES), 1)
    mask_new = jnp.logical_and(jnp.logical_and((lane & (DIFF_H - 1)) == row_head,
                                               jnp.right_shift(lane, 3) <= (row & (tq - 1))),
                               lane < n_new * DIFF_H)
    parts.append(part(_dot_nt(qs, k_new), mask_new, v_new))
    m_all = functools.reduce(jnp.maximum, [m for m, _, _ in parts])
    scale = [jnp.exp(m - m_all) for m, _, _ in parts]
    l_fin = functools.reduce(lambda a, b_: a + b_, [w * l for w, (_, l, _) in zip(scale, parts)])
    acc = functools.reduce(lambda a, b_: a + b_, [w * a for w, (_, _, a) in zip(scale, parts)])
    o = acc / jnp.maximum(l_fin, 1e-30)
    heads = [_diff_combine(o[(2 * h) * tq:(2 * h + 1) * tq], o[(2 * h + 1) * tq:(2 * h + 2) * tq],
                           lq_ref[...], sg_ref[...], lam_init) for h in range(DIFF_H)]
    o_ref[0] = jnp.concatenate(heads, axis=1).astype(o_ref.dtype)


def _dec_diff(page_table, qd, pool, k_new, v_new, lam_qk_o, sub_g, *, n_new, lam_init):
    b, n_pages = page_table.shape
    cfg = dict(n_pages=n_pages, n_new=n_new, lam_init=lam_init)
    page = lambda i: pl.BlockSpec((1, PAGE_SIZE, 2, DIFF_H, 2 * HD), lambda bi, pt: (pt[bi, i], 0, 0, 0, 0))
    per_b = lambda shape: pl.BlockSpec((1,) + shape, lambda bi, pt: (bi,) + (0,) * len(shape))
    const = lambda shape: pl.BlockSpec(shape, lambda bi, pt: (0,) * len(shape))
    width = 2 * DIFF_H * HD
    grid_spec = pltpu.PrefetchScalarGridSpec(
        num_scalar_prefetch=1, grid=(b,),
        in_specs=[page(i) for i in range(n_pages)]
                 + [per_b((DEC_ROWS, width)), per_b((DIFF_H * n_new, 2 * HD)), per_b((DIFF_H * n_new, 2 * HD)),
                    const((4, HD)), const((1, 2 * HD))],
        out_specs=per_b((DEC_ROWS, width)),
        scratch_shapes=[pltpu.VMEM((2 * DIFF_H * DEC_ROWS, LANES), f32)])
    return pl.pallas_call(
        functools.partial(_dec_diff_body, cfg), grid_spec=grid_spec,
        out_shape=jax.ShapeDtypeStruct((b, DEC_ROWS, width), bf16),
        compiler_params=_cparams("parallel"),
        name="decode_diff",
    )(page_table, *([pool] * n_pages), qd, k_new, v_new, lam_qk_o, sub_g)


def _pad_rows(a, n):
    return jnp.pad(a, ((0, 0), (0, n - a.shape[1])) + ((0, 0),) * (a.ndim - 2))


def _trunk(x, mods, qpos0, page_table, caches, wts, params):
    (b_f, w_pool, lam_qk, subln_g, norm_mix_g, norm_ffn_g, final_g) = params
    b, t, d = x.shape
    m = b * t
    decode = caches is not None
    depth = len(mods)
    tm = min(256, m)
    x2d = x.reshape(m, d)
    pos = qpos0 + np.tile(np.arange(t), b)
    tables = _rope_tables(pos)

    if decode:
        tiles_per_mod = 1
        tm_mlp, mlp_tiles_per_mod = tm, 1
        expand = lambda v: jnp.repeat(v, t, axis=0).reshape(m // tm, tm, d)
    else:
        tiles_per_mod = t // tm
        tm_mlp = min(1024, t)
        mlp_tiles_per_mod = t // tm_mlp
        expand = lambda v: v.reshape(b, 1, d)

    per_seq = lambda a, rows=DEC_ROWS: _pad_rows(a.reshape(b, t, a.shape[-1]), rows)
    outs = {}
    for li in range(depth):
        sh1, sc1, gt1, sh2, sc2, gt2 = [expand(v) for v in jnp.split(mods[li], 6, axis=-1)]
        if li % 2 == 0:
            e = li // 2
            bias = jnp.pad(b_f[e].reshape(1, FOX_H), ((0, 0), (0, LANES - FOX_H)))
            po = _proj(x2d, norm_mix_g[li], sc1, sh1, tiles_per_mod, wts["in_even"][e], tables, bias,
                       _even_plan(), _even_out_defs(not decode), tm, t)
            logf = po[O_LOGF][:, :FOX_H].reshape(b, t, FOX_H)
            if decode:
                kv_leaf = lambda oi, heads: po[oi].reshape(b, t, 2, heads, HD)
            else:
                kv_leaf = lambda oi, heads: jnp.transpose(po[oi].reshape(b, 2, heads, HD, t), (0, 4, 1, 2, 3))
            outs.setdefault("cmp", []).append(kv_leaf(O_KVC, NSA_KVH))
            outs.setdefault("slc", []).append(kv_leaf(O_KVS, NSA_KVH))
            outs.setdefault("fox", []).append(kv_leaf(O_KVF, FOX_H))
            outs.setdefault("logf", []).append(logf)
            kvw_new = kv_leaf(O_KVW, NSA_KVH)
            if not decode:
                outs.setdefault("win", []).append(kvw_new[:, t - min(WINDOW, t):])
                qn, qf = po[O_QN].reshape(b, t, NSA_H * HD), po[O_QF].reshape(b, t, FOX_H * HD)
                cmp = _pad_rows(_compress(po[O_KVC_ROWS].reshape(b, t, 4 * HD), w_pool[e]), LANES)
                ck, cv = cmp[:, :, :LANES].astype(bf16), cmp[:, :, LANES:].astype(bf16)
                o_cmp, mb = _cmp_topk(qn, ck, cv, tq=min(128, t), nb_valid=t // CMP_BLOCK)
                lane_slab = lambda oi, w: po[oi].reshape(b, t, w)
                o_s = _flash("slc", qn, lane_slab(O_KS, LANES), lane_slab(O_VS, LANES), [mb],
                             tq=min(128, t), tk=min(FLASH_TK, t), out_dtype=f32)
                o_w = _flash("win", qn, lane_slab(O_KW, LANES), lane_slab(O_VW, LANES), [],
                             tq=min(256, t), tk=min(512, t), out_dtype=f32)
                tk_f = min(FLASH_TK, t)
                f_t = _cumsum_lanes(jnp.transpose(logf, (0, 2, 1)), min(512, t))
                fk = jnp.transpose(f_t.reshape(b, FOX_H // 2, 2, t // tk_f, tk_f), (0, 1, 3, 2, 4))
                fq = f_t.reshape(b, FOX_H // 2, 2, t, 1)
                o_f = _flash("fox", qf, lane_slab(O_KF, 4 * LANES), lane_slab(O_VF, 4 * LANES), [fk, fq],
                             tq=min(512, t), tk=tk_f, out_dtype=bf16)
                acts = [a.reshape(m, a.shape[-1]) for a in (o_cmp, o_s, o_w)] + [po[O_GATE], o_f.reshape(m, -1)]
            else:
                c_cmp, c_slc, c_win, c_fox, c_logf, _ = caches
                outs.setdefault("win", []).append(jnp.concatenate([c_win[e], kvw_new], axis=1)[:, t:])
                kt_view = lambda c: jnp.transpose(c, (0, 2, 3, 4, 1))
                qn = per_seq(po[O_QN])
                o_cmp, o_s, o_w = _dec_nsa(page_table, qn, kt_view(c_cmp[e]), kt_view(c_slc[e]),
                                           kt_view(c_win[e]), per_seq(po[O_KVS]), per_seq(po[O_KVW]),
                                           w_pool[e], qpos0=qpos0, n_new=t)
                qf = po[O_QF].reshape(b, t, 1, FOX_H * HD)
                head_of_lane = np.arange(FOX_H * HD) // HD
                own = jnp.asarray(head_of_lane[None, :] == np.arange(FOX_H)[:, None], dtype=bf16)
                qbd = (qf * own[None, None]).reshape(b, t * FOX_H, FOX_H * HD)
                lf_new = jnp.pad(jnp.transpose(logf, (0, 2, 1)), ((0, 0), (0, 0), (0, LANES - t)))
                o_f = _dec_fox(page_table, qbd, kt_view(c_fox[e]), jnp.transpose(c_logf[e], (0, 2, 1)),
                               per_seq(po[O_KVF]), lf_new, n_new=t)
                acts = [a[:, :t].reshape(m, a.shape[-1]) for a in (o_cmp, o_s, o_w)]
                acts += [po[O_GATE], o_f[:, :t].reshape(m, -1)]
            x2d = _outproj(_outproj_even_body, x2d, gt1, tiles_per_mod, acts, wts["out_even"][e], tm,
                           "outproj_even")
        else:
            o = li // 2
            po = _proj(x2d, norm_mix_g[li], sc1, sh1, tiles_per_mod, wts["in_odd"][o], tables,
                       jnp.zeros((1, LANES), f32), _odd_plan(), ODD_OUT_DEFS, tm, t)
            outs.setdefault("diff", []).append(po[O_KVD].reshape(b, t, 2, DIFF_H, 2 * HD))
            lam_init = 0.8 - 0.6 * math.exp(-0.3 * li)
            sub_g = subln_g[o].reshape(1, 2 * HD)
            width = 2 * DIFF_H * HD
            if not decode:
                o_d = _flash("diff", po[O_QD].reshape(b, t, width), po[O_KD].reshape(b, t, width),
                             po[O_VD].reshape(b, t, width), [lam_qk[o], sub_g], tq=min(512, t), tk=min(FLASH_TK, t),
                             out_dtype=bf16, lam_init=lam_init)
            else:
                kvd = po[O_KVD].reshape(b, t, 2, DIFF_H, 2 * HD)
                rows_new = lambda a: a.reshape(b, t * DIFF_H, 2 * HD)
                o_d = _dec_diff(page_table, per_seq(po[O_QD]), caches[5][o], rows_new(kvd[:, :, 0]),
                                rows_new(kvd[:, :, 1]), lam_qk[o], sub_g, n_new=t, lam_init=lam_init)[:, :t]
            x2d = _outproj(_outproj_odd_body, x2d, gt1, tiles_per_mod, [o_d.reshape(m, d)],
                           wts["out_odd"][o], tm, "outproj_odd")
        x2d = _mlp(x2d, norm_ffn_g[li], sc2, sh2, gt2, mlp_tiles_per_mod, wts["ff_in"][li], wts["ff_out"][li],
                   final_g, li == depth - 1, tm_mlp, 1024)
    stack = lambda name: jnp.stack(outs[name], 0)
    return (x2d.reshape(b, t, d), stack("cmp"), stack("slc"), stack("win"), stack("fox"), stack("logf"),
            stack("diff"))


def kernel(x_prompt, x_sample, cache_nsa_cmp_kv, cache_nsa_slc_kv, cache_nsa_win_kv, cache_fox_kv,
           cache_fox_logf, cache_diff_kv, page_table, c_prompt, c_sample, w_in_even, b_f, w_pool,
           w_out_even, w_in_odd, lam_qk, subln_g, w_out_odd, norm_mix_g, norm_ffn_g, w_ada, b_ada,
           w_ff_in, w_ff_out, final_g):
    depth = w_ada.shape[0]
    nb_p, nb_s = c_prompt.shape[0], c_sample.shape[0]
    past_len = page_table.shape[1] * PAGE_SIZE

    perm_rows = lambda w: jnp.concatenate(
        [w[:NSA_H * HD].reshape(NSA_H, HD, -1)[NSA_PERM, :, :].reshape(NSA_H * HD, -1), w[NSA_H * HD:]], axis=0)
    wts = dict(
        in_even=[_even_weight(w_in_even[e]) for e in range(w_in_even.shape[0])],
        out_even=[perm_rows(w_out_even[e]).astype(bf16) for e in range(w_out_even.shape[0])],
        in_odd=[w_in_odd[o].astype(bf16) for o in range(w_in_odd.shape[0])],
        out_odd=[w_out_odd[o].astype(bf16) for o in range(w_out_odd.shape[0])],
        ff_in=[w_ff_in[l].astype(bf16) for l in range(depth)],
        ff_out=[w_ff_out[l].astype(bf16) for l in range(depth)],
    )
    rows = -(-(nb_p + nb_s) // SUBLANES) * SUBLANES
    c_all = jnp.pad(jnp.concatenate([c_prompt, c_sample], axis=0), ((0, rows - nb_p - nb_s), (0, 0)))
    mod_all = _ada(c_all, w_ada.astype(bf16), b_ada)
    mods_p = [mod_all[l, :nb_p] for l in range(depth)]
    mods_s = [mod_all[l, nb_p:nb_p + nb_s] for l in range(depth)]

    params = (b_f, w_pool, lam_qk, subln_g, norm_mix_g, norm_ffn_g, final_g)
    caches = (cache_nsa_cmp_kv, cache_nsa_slc_kv, cache_nsa_win_kv, cache_fox_kv, cache_fox_logf, cache_diff_kv)
    y_p, p_cmp, p_slc, p_win, p_fox, p_logf, p_diff = _trunk(x_prompt, mods_p, 0, None, None, wts, params)
    y_s, s_cmp, s_slc, s_win, s_fox, s_logf, s_diff = _trunk(x_sample, mods_s, past_len, page_table, caches,
                                                             wts, params)
    return (y_p, y_s, p_cmp, s_cmp, p_slc, s_slc, p_win, s_win, p_fox, s_fox, p_logf, s_logf, p_diff, s_diff)
```

```python
import functools
import math

import numpy as np
import jax
import jax.numpy as jnp
from jax import lax
from jax.experimental import pallas as pl
from jax.experimental.pallas import tpu as pltpu

HD = 64
ROPE_DIM = HD // 4
ROPE_HALF = ROPE_DIM // 2
ROPE_THETA = 500000.0
NSA_H = 8
NSA_KVH = 2
CMP_BLOCK = 64
CMP_SHIFT = 6
N_SEL = 16
WINDOW = 512
FOX_H = 8
DIFF_H = 8
EPS = 1e-6
PAGE_SIZE = 128

LANES = 128
SUBLANES = 8
VMEM_LIMIT_BYTES = 48 * 1024 * 1024

NEG_MASK = -1e30
NEG_BLOCK = -1e9
PAGES_PER_STEP = 4
FLASH_TK = 1024

NSA_PERM = (0, 4, 1, 5, 2, 6, 3, 7)

f32 = jnp.float32
bf16 = jnp.bfloat16


def _cparams(*sem):
    return pltpu.CompilerParams(dimension_semantics=sem, vmem_limit_bytes=VMEM_LIMIT_BYTES)


def _dot(a, b):
    return jnp.dot(a, b, preferred_element_type=f32)


def _dot_nt(a, b):
    return lax.dot_general(a, b, (((1,), (1,)), ((), ())), preferred_element_type=f32)


def _rms_mod(x, g, sc, sh):
    y = x * lax.rsqrt(jnp.mean(x * x, axis=-1, keepdims=True) + EPS)
    return (y * g) * (1.0 + sc) + sh


def _split_dot(x, w, parts):
    acc = None
    for _ in range(parts):
        piece = x.astype(bf16)
        term = _dot(piece, w)
        acc = term if acc is None else acc + term
        x = x - piece.astype(f32)
    return acc


def _split_dot_rhs(w, x, parts):
    acc = None
    for _ in range(parts):
        piece = x.astype(bf16)
        term = _dot(w, piece)
        acc = term if acc is None else acc + term
        x = x - piece.astype(f32)
    return acc


def _ada_body(c_ref, w_ref, b_ref, o_ref):
    c = c_ref[...]
    a = (c * jax.nn.sigmoid(c)).astype(bf16)
    o_ref[0] = _dot(a, w_ref[0]) + b_ref[0]


def _ada(c_all, w_ada_b, b_ada):
    depth, d, n = w_ada_b.shape
    rows = c_all.shape[0]
    tn = n // 4
    return pl.pallas_call(
        _ada_body,
        out_shape=jax.ShapeDtypeStruct((depth, rows, n), f32),
        grid=(depth, n // tn),
        in_specs=[pl.BlockSpec((rows, d), lambda l, j: (0, 0)),
                  pl.BlockSpec((1, d, tn), lambda l, j: (l, 0, j)),
                  pl.BlockSpec((1, 1, tn), lambda l, j: (l, 0, j))],
        out_specs=pl.BlockSpec((1, rows, tn), lambda l, j: (l, 0, j)),
        compiler_params=_cparams("parallel", "parallel"),
        name="ada_mod",
    )(c_all, w_ada_b, b_ada.reshape(depth, 1, n))


def _rope_group(a, cos, sa, sb):
    return a * cos + pltpu.roll(a, LANES - ROPE_HALF, 1) * sa + pltpu.roll(a, ROPE_HALF, 1) * sb


def _proj_body(plan, x_ref, g_ref, sc_ref, sh_ref, w_ref, cos_ref, sa_ref, sb_ref, bias_ref, *out_refs):
    h = _rms_mod(x_ref[...], g_ref[...], sc_ref[0], sh_ref[0]).astype(bf16)
    cos, sa, sb = cos_ref[...], sa_ref[...], sb_ref[...]
    for src, groups in plan:
        width = LANES * len(groups)
        acc = _dot(h, w_ref[:, src:src + width])
        for gi, (kind, scale, dests) in enumerate(groups):
            a = acc[:, gi * LANES:(gi + 1) * LANES]
            if kind == "rope":
                a = _rope_group(a, cos, sa, sb)
            elif kind == "sigmoid":
                a = jax.nn.sigmoid(a)
            elif kind == "logsig":
                z = a + bias_ref[...]
                a = jnp.minimum(z, 0.0) - jnp.log1p(jnp.exp(-jnp.abs(z)))
            if scale != 1.0:
                a = a * scale
            for oi, col in dests:
                if len(out_refs[oi].shape) == 3:
                    out_refs[oi][0, col:col + LANES, :] = a.T.astype(out_refs[oi].dtype)
                else:
                    out_refs[oi][:, col:col + LANES] = a.astype(out_refs[oi].dtype)


def _proj(x2d, g, sc, sh, tiles_per_mod, w_b, tables, bias, plan, out_defs, tm, seq_len):
    m, d = x2d.shape
    n = w_b.shape[1]
    cos, sa, sb = tables
    mod_rows = sc.shape[1]
    row = lambda i: (i, 0)
    const = lambda i: (0, 0)
    mod = lambda i: (i // tiles_per_mod, 0, 0)
    tiles_per_seq = seq_len // tm
    out_shape, out_specs = [], []
    for w, dt, position_minor in out_defs:
        if position_minor:
            out_shape.append(jax.ShapeDtypeStruct((m // seq_len, w, seq_len), dt))
            out_specs.append(pl.BlockSpec((1, w, tm), lambda i: (i // tiles_per_seq, 0, i % tiles_per_seq)))
        else:
            out_shape.append(jax.ShapeDtypeStruct((m, w), dt))
            out_specs.append(pl.BlockSpec((tm, w), row))
    return pl.pallas_call(
        functools.partial(_proj_body, plan),
        out_shape=out_shape,
        grid=(m // tm,),
        in_specs=[pl.BlockSpec((tm, d), row),
                  pl.BlockSpec((1, d), const),
                  pl.BlockSpec((1, mod_rows, d), mod),
                  pl.BlockSpec((1, mod_rows, d), mod),
                  pl.BlockSpec((d, n), const, pipeline_mode=pl.Buffered(1)),
                  pl.BlockSpec((tm, LANES), row),
                  pl.BlockSpec((tm, LANES), row),
                  pl.BlockSpec((tm, LANES), row),
                  pl.BlockSpec((1, LANES), const)],
        out_specs=out_specs,
        compiler_params=_cparams("parallel"),
        name="norm_mod_proj",
    )(x2d, g.reshape(1, d), sc, sh, w_b, cos, sa, sb, bias)


def _rope_tables(pos):
    inv = np.float32(ROPE_THETA) ** (-np.arange(ROPE_HALF, dtype=np.float32) / np.float32(ROPE_HALF))
    ang = (pos.astype(np.float32)[:, None] * inv[None, :]).astype(np.float64)
    n = pos.shape[0]
    cos = np.ones((n, HD), np.float64)
    sa = np.zeros((n, HD), np.float64)
    sb = np.zeros((n, HD), np.float64)
    cos[:, :ROPE_HALF] = np.cos(ang)
    cos[:, ROPE_HALF:ROPE_DIM] = np.cos(ang)
    sa[:, :ROPE_HALF] = -np.sin(ang)
    sb[:, ROPE_HALF:ROPE_DIM] = np.sin(ang)
    tile = lambda t: jnp.asarray(np.tile(t, (1, LANES // HD)).astype(np.float32))
    return tile(cos), tile(sa), tile(sb)


EV_QN, EV_KVC, EV_KVS, EV_KVW, EV_QF, EV_KVF, EV_GATE, EV_FF = 0, 512, 768, 1024, 1280, 1792, 2816, 4352


def _even_weight(w):
    d = w.shape[0]
    sizes = [NSA_H * HD, 2 * NSA_KVH * HD, 2 * NSA_KVH * HD, 2 * NSA_KVH * HD, 3 * NSA_H, FOX_H * HD,
             2 * FOX_H * HD, FOX_H]
    offs = np.concatenate([[0], np.cumsum(sizes)])
    q_n, kv_c, kv_s, kv_w, gates, q_f, kv_f, f_f = [w[:, offs[i]:offs[i + 1]] for i in range(8)]
    q_n = q_n.reshape(d, NSA_H, HD)[:, NSA_PERM, :].reshape(d, NSA_H * HD)
    gates = gates.reshape(d, NSA_H, 3)[:, NSA_PERM, :]
    gates = jnp.repeat(jnp.transpose(gates, (0, 2, 1))[..., None], HD, axis=-1).reshape(d, 3 * NSA_H * HD)
    f_f = jnp.pad(f_f, ((0, 0), (0, LANES - FOX_H)))
    return jnp.concatenate([q_n, kv_c, kv_s, kv_w, q_f, kv_f, gates, f_f], axis=1).astype(bf16)


(O_QN, O_KVC, O_KVS, O_KS, O_VS, O_KVW, O_KW, O_VW, O_QF, O_KVF, O_KF, O_VF, O_GATE, O_LOGF, O_KVC_ROWS) = range(15)


def _even_out_defs(position_minor):
    pm = position_minor
    return [(512, bf16, False), (256, f32, pm), (256, f32, pm), (128, bf16, False), (128, bf16, False),
            (256, f32, pm), (128, bf16, False), (128, bf16, False), (512, bf16, False), (1024, f32, pm),
            (512, bf16, False), (512, bf16, False), (1536, f32, False), (128, f32, False), (256, f32, False)]


def _even_plan():
    scale = HD ** -0.5
    plan = [(EV_QN, [("rope", scale, [(O_QN, LANES * i)]) for i in range(4)]),
            (EV_KVC, [("rope", 1.0, [(O_KVC, 0), (O_KVC_ROWS, 0)]),
                      ("plain", 1.0, [(O_KVC, LANES), (O_KVC_ROWS, LANES)])]),
            (EV_KVS, [("rope", 1.0, [(O_KVS, 0), (O_KS, 0)]), ("plain", 1.0, [(O_KVS, LANES), (O_VS, 0)])]),
            (EV_KVW, [("rope", 1.0, [(O_KVW, 0), (O_KW, 0)]), ("plain", 1.0, [(O_KVW, LANES), (O_VW, 0)])]),
            (EV_QF, [("plain", scale, [(O_QF, LANES * i)]) for i in range(4)]),
            (EV_KVF, [("plain", 1.0, [(O_KVF, LANES * i), (O_KF, LANES * i)]) for i in range(4)]),
            (EV_KVF + 512, [("plain", 1.0, [(O_KVF, 512 + LANES * i), (O_VF, LANES * i)]) for i in range(4)])]
    for j in range(3):
        plan.append((EV_GATE + 512 * j, [("sigmoid", 1.0, [(O_GATE, 512 * j + LANES * i)]) for i in range(4)]))
    plan.append((EV_FF, [("logsig", 1.0, [(O_LOGF, 0)])]))
    return plan


(O_QD, O_KVD, O_KD, O_VD) = range(4)
ODD_OUT_DEFS = [(1024, bf16, False), (2048, f32, False), (1024, bf16, False), (1024, bf16, False)]


def _odd_plan():
    scale = HD ** -0.5
    plan = []
    for c in range(2):
        plan.append((512 * c, [("rope", scale, [(O_QD, 512 * c + LANES * i)]) for i in range(4)]))
    for c in range(2):
        plan.append((1024 + 512 * c,
                     [("rope", 1.0, [(O_KVD, 512 * c + LANES * i), (O_KD, 512 * c + LANES * i)]) for i in range(4)]))
    for c in range(2):
        plan.append((2048 + 512 * c,
                     [("plain", 1.0, [(O_KVD, 1024 + 512 * c + LANES * i), (O_VD, 512 * c + LANES * i)])
                      for i in range(4)]))
    return plan


def _outproj_even_body(x_ref, gt_ref, oc_ref, os_ref, ow_ref, gates_ref, of_ref, w_ref, o_ref):
    nw = NSA_H * HD
    o_nsa = (gates_ref[:, 0:nw] * oc_ref[...] + gates_ref[:, nw:2 * nw] * os_ref[...]
             + gates_ref[:, 2 * nw:3 * nw] * ow_ref[...])
    a = jnp.concatenate([o_nsa.astype(bf16), of_ref[...]], axis=1)
    o_ref[...] = x_ref[...] + gt_ref[0] * _dot(a, w_ref[...])


def _outproj_odd_body(x_ref, gt_ref, od_ref, w_ref, o_ref):
    o_ref[...] = x_ref[...] + gt_ref[0] * _dot(od_ref[...], w_ref[...])


def _outproj(body, x2d, gt, tiles_per_mod, acts, w_b, tm, name):
    m, d = x2d.shape
    row = lambda i: (i, 0)
    mod = lambda i: (i // tiles_per_mod, 0, 0)
    return pl.pallas_call(
        body,
        out_shape=jax.ShapeDtypeStruct((m, d), f32),
        grid=(m // tm,),
        in_specs=[pl.BlockSpec((tm, d), row), pl.BlockSpec((1, gt.shape[1], d), mod)]
                 + [pl.BlockSpec((tm, a.shape[1]), row) for a in acts]
                 + [pl.BlockSpec(w_b.shape, lambda i: (0, 0), pipeline_mode=pl.Buffered(1))],
        out_specs=pl.BlockSpec((tm, d), row),
        compiler_params=_cparams("parallel"),
        name=name,
    )(x2d, gt, *acts, w_b)


def _mlp_body(final, x_ref, g_ref, sc_ref, sh_ref, gt_ref, w1_ref, w2_ref, fg_ref, o_ref, h_ref, acc_ref):
    f = pl.program_id(1)

    @pl.when(f == 0)
    def _():
        h_ref[...] = _rms_mod(x_ref[...], g_ref[...], sc_ref[0], sh_ref[0]).astype(bf16)
        acc_ref[...] = jnp.zeros_like(acc_ref)

    a = jnp.maximum(_dot(h_ref[...], w1_ref[...]), 0.0)
    acc_ref[...] += _dot((a * a).astype(bf16), w2_ref[...])

    @pl.when(f == pl.num_programs(1) - 1)
    def _():
        y = x_ref[...] + gt_ref[0] * acc_ref[...]
        if final:
            y = y * lax.rsqrt(jnp.mean(y * y, axis=-1, keepdims=True) + EPS) * fg_ref[...]
        o_ref[...] = y


def _mlp(x2d, g, sc, sh, gt, tiles_per_mod, w1_b, w2_b, final_g, final, tm, tf):
    m, d = x2d.shape
    dff = w1_b.shape[1]
    mod_rows = sc.shape[1]
    row = lambda i, f: (i, 0)
    const = lambda i, f: (0, 0)
    mod = lambda i, f: (i // tiles_per_mod, 0, 0)
    return pl.pallas_call(
        functools.partial(_mlp_body, final),
        out_shape=jax.ShapeDtypeStruct((m, d), f32),
        grid=(m // tm, dff // tf),
        in_specs=[pl.BlockSpec((tm, d), row), pl.BlockSpec((1, d), const),
                  pl.BlockSpec((1, mod_rows, d), mod), pl.BlockSpec((1, mod_rows, d), mod),
                  pl.BlockSpec((1, mod_rows, d), mod),
                  pl.BlockSpec((d, tf), lambda i, f: (0, f)), pl.BlockSpec((tf, d), lambda i, f: (f, 0)),
                  pl.BlockSpec((1, d), const)],
        out_specs=pl.BlockSpec((tm, d), row),
        scratch_shapes=[pltpu.VMEM((tm, d), bf16), pltpu.VMEM((tm, d), f32)],
        compiler_params=_cparams("parallel", "arbitrary"),
        name="relu2_mlp",
    )(x2d, g.reshape(1, d), sc, sh, gt, w1_b, w2_b, final_g.reshape(1, d))


def _compress_body(x_ref, wp_ref, o_ref):
    wp = wp_ref[...]
    rows = []
    for blk in range(x_ref.shape[1] // CMP_BLOCK):
        rows.append(jnp.sum(x_ref[0, blk * CMP_BLOCK:(blk + 1) * CMP_BLOCK, :] * wp, axis=0, keepdims=True))
    o_ref[0] = jnp.concatenate(rows, axis=0)


def _compress(kv, w_pool_e):
    b, t, c = kv.shape
    wp = jnp.repeat(jnp.transpose(w_pool_e, (1, 0, 2))[..., None], HD, axis=-1).reshape(CMP_BLOCK, c)
    rows_in = PAGES_PER_STEP * PAGE_SIZE
    rows_out = rows_in // CMP_BLOCK
    return pl.pallas_call(
        _compress_body,
        out_shape=jax.ShapeDtypeStruct((b, t // CMP_BLOCK, c), f32),
        grid=(b, t // rows_in),
        in_specs=[pl.BlockSpec((1, rows_in, c), lambda bi, g: (bi, g, 0)),
                  pl.BlockSpec((CMP_BLOCK, c), lambda bi, g: (0, 0))],
        out_specs=pl.BlockSpec((1, rows_out, c), lambda bi, g: (bi, g, 0)),
        compiler_params=_cparams("parallel", "parallel"),
        name="nsa_compress",
    )(kv, wp)


def _cumsum_body(x_ref, u_ref, o_ref, carry_ref):
    @pl.when(pl.program_id(1) == 0)
    def _():
        carry_ref[...] = jnp.zeros_like(carry_ref)

    cs = _split_dot(x_ref[0], u_ref[...], 3) + carry_ref[...]
    o_ref[0] = cs
    carry_ref[...] = cs[:, cs.shape[1] - 1:]


def _upper_ones(n):
    return jnp.asarray(np.triu(np.ones((n, n), np.float32)), dtype=bf16)


def _cumsum_lanes(x_t, tc):
    b, h, l = x_t.shape
    return pl.pallas_call(
        _cumsum_body,
        out_shape=jax.ShapeDtypeStruct((b, h, l), f32),
        grid=(b, l // tc),
        in_specs=[pl.BlockSpec((1, h, tc), lambda bi, c: (bi, 0, c)),
                  pl.BlockSpec((tc, tc), lambda bi, c: (0, 0))],
        out_specs=pl.BlockSpec((1, h, tc), lambda bi, c: (bi, 0, c)),
        scratch_shapes=[pltpu.VMEM((h, 1), f32)],
        compiler_params=_cparams("parallel", "arbitrary"),
        name="logf_cumsum",
    )(x_t, _upper_ones(tc))


def _stack_heads(q, qs_ref, n_slab, tq):
    lane = lax.broadcasted_iota(jnp.int32, (tq, LANES), 1)
    low = lane < HD
    for s in range(n_slab):
        slab = q[:, s * LANES:(s + 1) * LANES].astype(f32)
        qs_ref[(2 * s) * tq:(2 * s + 1) * tq, 0:LANES] = jnp.where(low, slab, 0.0).astype(qs_ref.dtype)
        qs_ref[(2 * s + 1) * tq:(2 * s + 2) * tq, 0:LANES] = jnp.where(low, 0.0, slab).astype(qs_ref.dtype)


def _unstack_heads(o, n_slab, tq):
    lane = lax.broadcasted_iota(jnp.int32, (tq, LANES), 1)
    low = lane < HD
    slabs = [jnp.where(low, o[(2 * s) * tq:(2 * s + 1) * tq], o[(2 * s + 1) * tq:(2 * s + 2) * tq])
             for s in range(n_slab)]
    return slabs[0] if n_slab == 1 else jnp.concatenate(slabs, axis=1)


def _masked_softmax(s, valid):
    sm = jnp.where(valid, s, NEG_MASK)
    m = jnp.max(sm, axis=1, keepdims=True)
    p = jnp.where(valid, jnp.exp(sm - m), 0.0)
    return p / jnp.maximum(jnp.sum(p, axis=1, keepdims=True), 1e-30)


def _select_blocks(p, tg, tq, nb_valid):
    n = lax.broadcasted_iota(jnp.int32, (1, LANES), 1)
    cur = jnp.right_shift(tg, CMP_SHIFT)
    n_cand = jnp.minimum(cur, nb_valid)
    out = []
    for g in range(NSA_KVH):
        imp = p[g * tq:(g + 1) * tq]
        for s_i in range(1, NSA_H // 2):
            imp = imp + p[(2 * s_i + g) * tq:(2 * s_i + g + 1) * tq]
        work = jnp.where(n < n_cand, imp, -1.0)
        sel = n == cur
        for it in range(N_SEL - 1):
            hit = n == jnp.argmax(work, axis=1, keepdims=True).astype(jnp.int32)
            sel = jnp.logical_or(sel, jnp.logical_and(hit, it < n_cand))
            work = jnp.where(hit, -2.0, work)
        out.append(jnp.where(sel, 0.0, NEG_BLOCK))
    return out


def _select_blocks_ranked(p, tg, tq, nb_valid):
    n = lax.broadcasted_iota(jnp.int32, (1, LANES), 1)
    n_col = lax.broadcasted_iota(jnp.int32, (LANES, 1), 0)
    cur = jnp.right_shift(tg, CMP_SHIFT)
    n_cand = jnp.minimum(cur, nb_valid)
    n_pick = jnp.minimum(n_cand, N_SEL - 1).astype(f32)
    out = []
    for g in range(NSA_KVH):
        imp = p[g * tq:(g + 1) * tq]
        for s_i in range(1, NSA_H // 2):
            imp = imp + p[(2 * s_i + g) * tq:(2 * s_i + g + 1) * tq]
        work = jnp.where(n < n_cand, imp, -1.0)
        work_t = jnp.concatenate([work, jnp.zeros((LANES - tq, LANES), f32)], axis=0).T
        ranks = []
        for r in range(tq):
            other = work_t[:, r:r + 1]
            mine = work[r:r + 1, :]
            ahead = jnp.logical_or(other > mine, jnp.logical_and(other == mine, n_col < n))
            ranks.append(jnp.sum(jnp.where(ahead, 1.0, 0.0), axis=0, keepdims=True))
        rank = jnp.concatenate(ranks, axis=0)
        sel = jnp.logical_or(n == cur, rank < n_pick)
        out.append(jnp.where(sel, 0.0, NEG_BLOCK))
    return out


def _flash_body(cfg, *refs):
    mode, n_slab, tq, tk, n_chunks = cfg["mode"], cfg["n_slab"], cfg["tq"], cfg["tk"], cfg["n_chunks"]
    window = cfg["window"]
    rows = 2 * n_slab * tq
    q_ref, k_ref, v_ref = refs[:3]
    extra = refs[3:-5]
    o_ref, qs_ref, s_ref, m_ref, acc_ref = refs[-5:]
    qi = pl.program_id(2)

    _stack_heads(q_ref[0], qs_ref, n_slab, tq)
    if mode == "slc":
        mb_ref = extra[0]
        for s in range(n_slab):
            for g in range(2):
                qs_ref[(2 * s + g) * tq:(2 * s + g + 1) * tq, LANES:2 * LANES] = mb_ref[0, g]
    m_ref[...] = jnp.full_like(m_ref, NEG_MASK)
    acc_ref[...] = jnp.zeros_like(acc_ref)

    q_lo = qi * tq
    qpos = q_lo + (lax.broadcasted_iota(jnp.int32, (rows, 1), 0) & (tq - 1))

    def scores(c):
        ks = pl.multiple_of(c * tk, tk)
        k = k_ref[0, pl.ds(ks, tk), :]
        if mode == "slc":
            blk = jnp.right_shift(ks + lax.broadcasted_iota(jnp.int32, (tk, LANES), 0), CMP_SHIFT)
            hot = jnp.where(blk == lax.broadcasted_iota(jnp.int32, (tk, LANES), 1), 1.0, 0.0).astype(bf16)
            k = jnp.concatenate([k, hot], axis=1)
        return _dot_nt(qs_ref[...], k)

    def update(s, c, masked):
        ks = pl.multiple_of(c * tk, tk)
        row_bias = 0.0
        if mode == "fox":
            fk_ref, fq_ref = extra
            fk = fk_ref[0, 0, c]
            s = jnp.concatenate([s[0:tq] - fk[0:1], s[tq:2 * tq] - fk[1:2]], axis=0)
            row_bias = jnp.concatenate([fq_ref[0, 0, 0], fq_ref[0, 0, 1]], axis=0)
        if masked:
            kpos = ks + lax.broadcasted_iota(jnp.int32, (1, tk), 1)
            mask = kpos <= qpos
            if window is not None:
                mask = jnp.logical_and(mask, (qpos - kpos) < window)
            s = jnp.where(mask, s, NEG_MASK)
        m_old = m_ref[...]
        m_new = jnp.maximum(m_old, jnp.max(s, axis=1, keepdims=True) + row_bias)
        alpha = jnp.exp(m_old - m_new)
        p = jnp.exp((s + (row_bias - m_new)).astype(bf16))
        v_ones = jnp.concatenate([v_ref[0, pl.ds(ks, tk), :], jnp.ones((tk, LANES), bf16)], axis=1)
        acc_ref[...] = alpha * acc_ref[...] + _dot(p, v_ones)
        m_ref[...] = m_new

    def issue(c, slot):
        s_ref[slot] = scores(c)

    if window is not None:
        c_lo = jnp.maximum(q_lo - window + 1, 0) // tk
        c_last = (q_lo + tq - 1) // tk
        issue(c_lo, 0)
        issue(c_last, 1)
        update(s_ref[0], c_lo, True)

        @pl.when(c_last > c_lo)
        def _():
            update(s_ref[1], c_last, True)
    else:
        n_full = (q_lo + 1) // tk

        issue(0, 0)

        def pair(j, carry):
            c = 2 * j
            issue(c + 1, 1)
            update(s_ref[0], c, False)
            issue(c + 2, 0)
            update(s_ref[1], c + 1, False)
            return carry

        lax.fori_loop(0, n_full // 2, pair, 0)

        @pl.when((n_full & 1) == 1)
        def _():
            issue(n_full, 1)
            update(s_ref[0], n_full - 1, False)
            update(s_ref[1], n_full, True)

        @pl.when((n_full & 1) == 0)
        def _():
            update(s_ref[0], n_full, True)

    acc = acc_ref[...]
    o = acc[:, :LANES] / jnp.maximum(acc[:, LANES:], 1e-30)
    if mode == "diff":
        lq_ref, sg_ref = extra
        o_ref[0] = _diff_combine(o[0:tq], o[tq:2 * tq], lq_ref[...], sg_ref[...], cfg["lam_init"]).astype(o_ref.dtype)
    else:
        o_ref[0] = _unstack_heads(o, n_slab, tq).astype(o_ref.dtype)


def _diff_combine(o1, o2, lp, sub_g, lam_init):
    lam = (jnp.exp(jnp.sum(lp[0:1] * lp[1:2], axis=1, keepdims=True))
           - jnp.exp(jnp.sum(lp[2:3] * lp[3:4], axis=1, keepdims=True)) + lam_init)
    d = o1 - lam * o2
    d = d * lax.rsqrt(jnp.mean(d * d, axis=-1, keepdims=True) + EPS) * sub_g
    return d * (1.0 - lam_init)


def _flash(mode, q, k, v, extra, *, tq, tk, out_dtype, lam_init=0.0):
    b, t_q, cq = q.shape
    t_k = k.shape[1]
    n_slab = 4 if mode in ("slc", "win") else 1
    n_grid_slab = cq // (LANES * n_slab)
    n_chunks = t_k // tk
    rows = 2 * n_slab * tq
    ck = 2 * LANES if mode == "slc" else LANES
    assert tk % tq == 0
    if mode == "win":
        assert tk == WINDOW
    score_slots = (2, rows, tk)
    cfg = dict(mode=mode, n_slab=n_slab, tq=tq, tk=tk, n_chunks=n_chunks,
               window=WINDOW if mode == "win" else None, lam_init=lam_init)
    qw = LANES * n_slab
    in_specs = [pl.BlockSpec((1, tq, qw), lambda bi, s, i: (bi, i, s)),
                pl.BlockSpec((1, t_k, LANES), lambda bi, s, i: (bi, 0, s)),
                pl.BlockSpec((1, t_k, LANES), lambda bi, s, i: (bi, 0, s))]
    if mode == "fox":
        in_specs += [pl.BlockSpec((1, 1, n_chunks, 2, tk), lambda bi, s, i: (bi, s, 0, 0, 0)),
                     pl.BlockSpec((1, 1, 2, tq, 1), lambda bi, s, i: (bi, s, 0, i, 0))]
    elif mode == "slc":
        in_specs += [pl.BlockSpec((1, 2, tq, LANES), lambda bi, s, i: (bi, 0, i, 0))]
    elif mode == "diff":
        in_specs += [pl.BlockSpec((4, HD), lambda bi, s, i: (0, 0)),
                     pl.BlockSpec((1, LANES), lambda bi, s, i: (0, 0))]
    return pl.pallas_call(
        functools.partial(_flash_body, cfg),
        out_shape=jax.ShapeDtypeStruct((b, t_q, cq), out_dtype),
        grid=(b, n_grid_slab, t_q // tq),
        in_specs=in_specs,
        out_specs=pl.BlockSpec((1, tq, qw), lambda bi, s, i: (bi, i, s)),
        scratch_shapes=[pltpu.VMEM((rows, ck), bf16), pltpu.VMEM(score_slots, f32), pltpu.VMEM((rows, 1), f32),
                        pltpu.VMEM((rows, 2 * LANES), f32)],
        compiler_params=_cparams("parallel", "parallel", "parallel"),
        name="flash_" + mode,
    )(q, k, v, *extra)


def _cmp_body(tq, nb_valid, q_ref, ck_ref, cv_ref, o_ref, mb_ref, qs_ref):
    n_slab = NSA_H // 2
    rows = 2 * n_slab * tq
    qi = pl.program_id(1)
    _stack_heads(q_ref[0], qs_ref, n_slab, tq)
    s = _dot_nt(qs_ref[...], ck_ref[0])
    t = qi * tq + (lax.broadcasted_iota(jnp.int32, (rows, 1), 0) & (tq - 1))
    n = lax.broadcasted_iota(jnp.int32, (1, LANES), 1)
    valid = jnp.logical_and((n + 1) * CMP_BLOCK - 1 <= t, n < nb_valid)
    p = _masked_softmax(s, valid)
    o_ref[0] = _unstack_heads(_dot(p.astype(bf16), cv_ref[0]), n_slab, tq)
    bias = _select_blocks(p, t[0:tq], tq, nb_valid)
    for g in range(NSA_KVH):
        mb_ref[0, g] = bias[g].astype(bf16)


def _cmp_topk(q, ck, cv, *, tq, nb_valid):
    b, t_q, cq = q.shape
    rows = NSA_H * tq
    return pl.pallas_call(
        functools.partial(_cmp_body, tq, nb_valid),
        out_shape=[jax.ShapeDtypeStruct((b, t_q, cq), f32), jax.ShapeDtypeStruct((b, NSA_KVH, t_q, LANES), bf16)],
        grid=(b, t_q // tq),
        in_specs=[pl.BlockSpec((1, tq, cq), lambda bi, i: (bi, i, 0)),
                  pl.BlockSpec((1, LANES, LANES), lambda bi, i: (bi, 0, 0)),
                  pl.BlockSpec((1, LANES, LANES), lambda bi, i: (bi, 0, 0))],
        out_specs=[pl.BlockSpec((1, tq, cq), lambda bi, i: (bi, i, 0)),
                   pl.BlockSpec((1, NSA_KVH, tq, LANES), lambda bi, i: (bi, 0, i, 0))],
        scratch_shapes=[pltpu.VMEM((rows, LANES), bf16)],
        compiler_params=_cparams("parallel", "parallel"),
        name="nsa_cmp_topk",
    )(q, ck, cv)


DEC_ROWS = SUBLANES
DEC_SEQS = 2


def _pad_keys(new, lo, width):
    x = new[:, lo:lo + width]
    return jnp.concatenate([x, jnp.zeros((LANES - x.shape[0], width), f32)], axis=0).astype(bf16)


def _new_key_mask(rows, n_new):
    qi = lax.broadcasted_iota(jnp.int32, (rows, 1), 0) & (DEC_ROWS - 1)
    kj = lax.broadcasted_iota(jnp.int32, (1, LANES), 1)
    return jnp.logical_and(kj <= qi, kj < n_new)


def _softmax2(s1, s2, mask2):
    s2 = jnp.where(mask2, s2, NEG_MASK)
    m = jnp.maximum(jnp.max(s1, axis=1, keepdims=True), jnp.max(s2, axis=1, keepdims=True))
    p1 = jnp.exp(s1 - m)
    p2 = jnp.where(mask2, jnp.exp(s2 - m), 0.0)
    den = jnp.sum(p1, axis=1, keepdims=True) + jnp.sum(p2, axis=1, keepdims=True)
    return p1, p2, jnp.maximum(den, 1e-30)


def _dec_nsa_body(cfg, pt_ref, *refs):
    n_pages, n_seq = cfg["n_pages"], cfg["n_seq"]
    (q_ref, win_ref, news_ref, neww_ref, wp_ref, e_ref, oht_ref,
     oc_ref, os_ref, ow_ref, qs_ref) = refs[2 * n_seq * n_pages:]
    for j in range(n_seq):
        _dec_nsa_one(cfg, j, refs[j * n_pages:(j + 1) * n_pages],
                     refs[(n_seq + j) * n_pages:(n_seq + j + 1) * n_pages], q_ref, win_ref, news_ref, neww_ref,
                     wp_ref, e_ref, oht_ref, oc_ref, os_ref, ow_ref, qs_ref.at[j])


def _dec_nsa_one(cfg, j, cmp_pages, slc_pages, q_ref, win_ref, news_ref, neww_ref, wp_ref, e_ref, oht_ref,
                 oc_ref, os_ref, ow_ref, qs_ref):
    n_pages, qpos0, n_new, w_buf = cfg["n_pages"], cfg["qpos0"], cfg["n_new"], cfg["w_buf"]
    tq, n_slab = DEC_ROWS, NSA_H // 2
    rows = NSA_H * tq
    gd = NSA_KVH * HD

    _stack_heads(q_ref[j], qs_ref, n_slab, tq)
    qs = qs_ref[...].astype(bf16)
    t = qpos0 + (lax.broadcasted_iota(jnp.int32, (rows, 1), 0) & (tq - 1))
    n = lax.broadcasted_iota(jnp.int32, (1, LANES), 1)
    mask_new = _new_key_mask(rows, n_new)

    def compressed(kv):
        xt = jnp.concatenate([p[0, kv].reshape(gd, PAGE_SIZE) for p in cmp_pages], axis=1)
        w = jnp.concatenate([wp_ref[kv]] * n_pages, axis=1)
        return _split_dot(xt * w, e_ref[...], 2).astype(bf16)

    ck_t, cv_t = compressed(0), compressed(1)
    nb_valid = n_pages * PAGE_SIZE // CMP_BLOCK
    valid = jnp.logical_and((n + 1) * CMP_BLOCK - 1 <= t, n < nb_valid)
    p = _masked_softmax(_dot(qs, ck_t), valid)
    oc_ref[j] = _unstack_heads(_dot_nt(p.astype(bf16), cv_t), n_slab, tq)

    bias = _select_blocks_ranked(p, t[0:tq], tq, nb_valid)
    qq = jnp.concatenate([qs, jnp.concatenate(bias * n_slab, axis=0).astype(bf16)], axis=1)
    ks_t = jnp.concatenate([p_[0, 0].reshape(gd, PAGE_SIZE) for p_ in slc_pages], axis=1).astype(bf16)
    vs_t = jnp.concatenate([p_[0, 1].reshape(gd, PAGE_SIZE) for p_ in slc_pages], axis=1).astype(bf16)
    s1 = _dot(qq, jnp.concatenate([ks_t, oht_ref[...]], axis=0))
    new_s = news_ref[j]
    s2 = _dot_nt(qs, _pad_keys(new_s, 0, gd))
    p1, p2, den = _softmax2(s1, s2, mask_new)
    o = _dot_nt(p1.astype(bf16), vs_t) + _dot(p2.astype(bf16), _pad_keys(new_s, gd, gd))
    os_ref[j] = _unstack_heads(o / den, n_slab, tq)

    kw_t = win_ref[j, 0].reshape(gd, w_buf).astype(bf16)
    vw_t = win_ref[j, 1].reshape(gd, w_buf).astype(bf16)
    kpos = qpos0 - w_buf + lax.broadcasted_iota(jnp.int32, (1, w_buf), 1)
    s1 = jnp.where((t - kpos) < WINDOW, _dot(qs, kw_t), NEG_MASK)
    new_w = neww_ref[j]
    s2 = _dot_nt(qs, _pad_keys(new_w, 0, gd))
    p1, p2, den = _softmax2(s1, s2, mask_new)
    o = _dot_nt(p1.astype(bf16), vw_t) + _dot(p2.astype(bf16), _pad_keys(new_w, gd, gd))
    ow_ref[j] = _unstack_heads(o / den, n_slab, tq)


def _dec_nsa(page_table, qn, cmp_pool, slc_pool, win_t, new_s, new_w, w_pool_e, *, qpos0, n_new):
    b, n_pages = page_table.shape
    w_buf = win_t.shape[-1]
    l_k = n_pages * PAGE_SIZE
    gd = NSA_KVH * HD
    n_seq = DEC_SEQS if b % DEC_SEQS == 0 else 1
    cfg = dict(n_pages=n_pages, n_seq=n_seq, qpos0=qpos0, n_new=n_new, w_buf=w_buf)
    wp = jnp.tile(jnp.repeat(jnp.transpose(w_pool_e, (0, 2, 1)), HD, axis=1), (1, 1, PAGE_SIZE // CMP_BLOCK))
    blk_of = np.arange(l_k) // CMP_BLOCK
    e_all = jnp.asarray(blk_of[:, None] == np.arange(LANES)[None, :], dtype=bf16)
    oh_t = jnp.asarray(np.arange(LANES)[:, None] == blk_of[None, :], dtype=bf16)
    page = lambda j, i: pl.BlockSpec((1, 2, NSA_KVH, HD, PAGE_SIZE),
                                     lambda bi, pt: (pt[n_seq * bi + j, i], 0, 0, 0, 0))
    pages = [page(j, i) for j in range(n_seq) for i in range(n_pages)]
    per_b = lambda shape: pl.BlockSpec((n_seq,) + shape, lambda bi, pt: (bi,) + (0,) * len(shape))
    const = lambda shape: pl.BlockSpec(shape, lambda bi, pt: (0,) * len(shape))
    grid_spec = pltpu.PrefetchScalarGridSpec(
        num_scalar_prefetch=1, grid=(b // n_seq,),
        in_specs=pages * 2
                 + [per_b((DEC_ROWS, NSA_H * HD)), per_b((2, NSA_KVH, HD, w_buf)), per_b((DEC_ROWS, 2 * gd)),
                    per_b((DEC_ROWS, 2 * gd)), const((2, gd, PAGE_SIZE)), const((l_k, LANES)), const((LANES, l_k))],
        out_specs=[per_b((DEC_ROWS, NSA_H * HD))] * 3,
        scratch_shapes=[pltpu.VMEM((n_seq, NSA_H * DEC_ROWS, LANES), f32)])
    return pl.pallas_call(
        functools.partial(_dec_nsa_body, cfg), grid_spec=grid_spec,
        out_shape=[jax.ShapeDtypeStruct((b, DEC_ROWS, NSA_H * HD), f32)] * 3,
        compiler_params=_cparams("parallel"),
        name="decode_nsa",
    )(page_table, *([cmp_pool] * (n_seq * n_pages)), *([slc_pool] * (n_seq * n_pages)), qn, win_t, new_s, new_w,
      wp, e_all, oh_t)


def _dec_fox_body(cfg, pt_ref, *refs):
    n_pages, n_seq = cfg["n_pages"], cfg["n_seq"]
    tail = refs[2 * n_seq * n_pages:]
    for j in range(n_seq):
        _dec_fox_one(cfg, j, refs[j * n_pages:(j + 1) * n_pages],
                     refs[(n_seq + j) * n_pages:(n_seq + j + 1) * n_pages], *tail)


def _dec_fox_one(cfg, j, kv_pages, lf_pages, qbd_ref, new_ref, lfn_ref, u_ref, prev_ref, o_ref):
    n_pages, n_new = cfg["n_pages"], cfg["n_new"]
    hd_all = FOX_H * HD
    qbd = qbd_ref[j]
    u = u_ref[...]

    local = _split_dot(jnp.concatenate([p[0] for p in lf_pages], axis=0), u, 3)
    totals = jnp.broadcast_to(local[:, PAGE_SIZE - 1:], local.shape)
    f_all = local + _split_dot_rhs(prev_ref[...], totals, 3)
    f_k = jnp.concatenate([f_all[i * FOX_H:(i + 1) * FOX_H] for i in range(n_pages)], axis=1)
    carry = f_all[(n_pages - 1) * FOX_H:, PAGE_SIZE - 1:]
    f_new = _split_dot(lfn_ref[j], u, 3)

    k_t = jnp.concatenate([p[0, 0].reshape(hd_all, PAGE_SIZE) for p in kv_pages], axis=1).astype(bf16)
    v_t = jnp.concatenate([p[0, 1].reshape(hd_all, PAGE_SIZE) for p in kv_pages], axis=1).astype(bf16)
    new = new_ref[j]
    s1 = _dot(qbd, k_t)
    s2 = _dot_nt(qbd, _pad_keys(new, 0, hd_all))
    kj = lax.broadcasted_iota(jnp.int32, (1, LANES), 1)
    s1_q, s2_q, m2_q = [], [], []
    for qi in range(n_new):
        f_q = carry + f_new[:, qi:qi + 1]
        s1_q.append(s1[qi * FOX_H:(qi + 1) * FOX_H] + f_q - f_k)
        s2_q.append(s2[qi * FOX_H:(qi + 1) * FOX_H] + f_q - (carry + f_new))
        m2_q.append(jnp.broadcast_to(kj <= qi, (FOX_H, LANES)))
    p1, p2, den = _softmax2(jnp.concatenate(s1_q, axis=0), jnp.concatenate(s2_q, axis=0),
                            jnp.concatenate(m2_q, axis=0))
    o = (_dot_nt(p1.astype(bf16), v_t) + _dot(p2.astype(bf16), _pad_keys(new, hd_all, hd_all))) / den
    head = lax.broadcasted_iota(jnp.int32, (FOX_H, hd_all), 0)
    own = jnp.right_shift(lax.broadcasted_iota(jnp.int32, (FOX_H, hd_all), 1), CMP_SHIFT) == head
    out = [jnp.sum(jnp.where(own, o[qi * FOX_H:(qi + 1) * FOX_H], 0.0), axis=0, keepdims=True)
           for qi in range(n_new)]
    out.append(jnp.zeros((DEC_ROWS - n_new, hd_all), f32))
    o_ref[j] = jnp.concatenate(out, axis=0).astype(o_ref.dtype)


def _dec_fox(page_table, qbd, kv_pool, lf_pool, new_kv, lf_new, *, n_new):
    b, n_pages = page_table.shape
    hd_all = FOX_H * HD
    n_seq = DEC_SEQS if b % DEC_SEQS == 0 else 1
    cfg = dict(n_pages=n_pages, n_seq=n_seq, n_new=n_new)
    kv_page = lambda j, i: pl.BlockSpec((1, 2, FOX_H, HD, PAGE_SIZE),
                                        lambda bi, pt: (pt[n_seq * bi + j, i], 0, 0, 0, 0))
    lf_page = lambda j, i: pl.BlockSpec((1, FOX_H, PAGE_SIZE), lambda bi, pt: (pt[n_seq * bi + j, i], 0, 0))
    per_b = lambda shape: pl.BlockSpec((n_seq,) + shape, lambda bi, pt: (bi,) + (0,) * len(shape))
    seq_pages = [(j, i) for j in range(n_seq) for i in range(n_pages)]
    grid_spec = pltpu.PrefetchScalarGridSpec(
        num_scalar_prefetch=1, grid=(b // n_seq,),
        in_specs=[kv_page(j, i) for j, i in seq_pages] + [lf_page(j, i) for j, i in seq_pages]
                 + [per_b((FOX_H * n_new, hd_all)), per_b((DEC_ROWS, 2 * hd_all)), per_b((FOX_H, LANES)),
                    pl.BlockSpec((PAGE_SIZE, PAGE_SIZE), lambda bi, pt: (0, 0)),
                    pl.BlockSpec((FOX_H * n_pages, FOX_H * n_pages), lambda bi, pt: (0, 0))],
        out_specs=per_b((DEC_ROWS, hd_all)))
    r = np.arange(FOX_H * n_pages)
    prev = jnp.asarray((r[:, None] % FOX_H == r[None, :] % FOX_H) & (r[None, :] // FOX_H < r[:, None] // FOX_H),
                       dtype=bf16)
    return pl.pallas_call(
        functools.partial(_dec_fox_body, cfg), grid_spec=grid_spec,
        out_shape=jax.ShapeDtypeStruct((b, DEC_ROWS, hd_all), bf16),
        compiler_params=_cparams("parallel"),
        name="decode_fox",
    )(page_table, *([kv_pool] * (n_seq * n_pages)), *([lf_pool] * (n_seq * n_pages)), qbd, new_kv, lf_new,
      _upper_ones(PAGE_SIZE), prev)


def _dec_diff_body(cfg, pt_ref, *refs):
    n_pages, n_new, lam_init = cfg["n_pages"], cfg["n_new"], cfg["lam_init"]
    pages = refs[:n_pages]
    q_ref, knew_ref, vnew_ref, lq_ref, sg_ref, o_ref, qs_ref = refs[n_pages:]
    tq = DEC_ROWS
    rows = 2 * DIFF_H * tq
    cols = PAGE_SIZE * DIFF_H
    _stack_heads(q_ref[0], qs_ref, DIFF_H, tq)
    row = lax.broadcasted_iota(jnp.int32, (rows, 1), 0)
    row_head = jnp.right_shift(row, 4)
    qs = qs_ref[...].astype(bf16)
    col = lax.broadcasted_iota(jnp.int32, (1, cols), 1)
    own = (col & (DIFF_H - 1)) == row_head

    def part(s, mask, v_rows):
        s = jnp.where(mask, s, NEG_MASK)
        m = jnp.max(s, axis=1, keepdims=True)
        p = jnp.exp(s - m)
        return m, jnp.sum(p, axis=1, keepdims=True), _dot(p.astype(bf16), v_rows)

    parts = []
    for p in pages:
        k_rows = p[0, :, 0].reshape(cols, 2 * HD).astype(bf16)
        v_rows = p[0, :, 1].reshape(cols, 2 * HD).astype(bf16)
        parts.append(part(_dot_nt(qs, k_rows), own, v_rows))
    pad = jnp.zeros((LANES - knew_ref.shape[1], 2 * HD), f32)
    k_new = jnp.concatenate([knew_ref[0], pad], axis=0).astype(bf16)
    v_new = jnp.concatenate([vnew_ref[0], pad], axis=0).astype(bf16)
    lane = lax.broadcasted_iota(jnp.int32, (1, LANES), 1)
    mask_new = jnp.logical_and(jnp.logical_and((lane & (DIFF_H - 1)) == row_head,
                                               jnp.right_shift(lane, 3) <= (row & (tq - 1))),
                               lane < n_new * DIFF_H)
    parts.append(part(_dot_nt(qs, k_new), mask_new, v_new))
    m_all = functools.reduce(jnp.maximum, [m for m, _, _ in parts])
    scale = [jnp.exp(m - m_all) for m, _, _ in parts]
    l_fin = functools.reduce(lambda a, b_: a + b_, [w * l for w, (_, l, _) in zip(scale, parts)])
    acc = functools.reduce(lambda a, b_: a + b_, [w * a for w, (_, _, a) in zip(scale, parts)])
    o = acc / jnp.maximum(l_fin, 1e-30)
    heads = [_diff_combine(o[(2 * h) * tq:(2 * h + 1) * tq], o[(2 * h + 1) * tq:(2 * h + 2) * tq],
                           lq_ref[...], sg_ref[...], lam_init) for h in range(DIFF_H)]
    o_ref[0] = jnp.concatenate(heads, axis=1).astype(o_ref.dtype)


def _dec_diff(page_table, qd, pool, k_new, v_new, lam_qk_o, sub_g, *, n_new, lam_init):
    b, n_pages = page_table.shape
    cfg = dict(n_pages=n_pages, n_new=n_new, lam_init=lam_init)
    page = lambda i: pl.BlockSpec((1, PAGE_SIZE, 2, DIFF_H, 2 * HD), lambda bi, pt: (pt[bi, i], 0, 0, 0, 0))
    per_b = lambda shape: pl.BlockSpec((1,) + shape, lambda bi, pt: (bi,) + (0,) * len(shape))
    const = lambda shape: pl.BlockSpec(shape, lambda bi, pt: (0,) * len(shape))
    width = 2 * DIFF_H * HD
    grid_spec = pltpu.PrefetchScalarGridSpec(
        num_scalar_prefetch=1, grid=(b,),
        in_specs=[page(i) for i in range(n_pages)]
                 + [per_b((DEC_ROWS, width)), per_b((DIFF_H * n_new, 2 * HD)), per_b((DIFF_H * n_new, 2 * HD)),
                    const((4, HD)), const((1, 2 * HD))],
        out_specs=per_b((DEC_ROWS, width)),
        scratch_shapes=[pltpu.VMEM((2 * DIFF_H * DEC_ROWS, LANES), f32)])
    return pl.pallas_call(
        functools.partial(_dec_diff_body, cfg), grid_spec=grid_spec,
        out_shape=jax.ShapeDtypeStruct((b, DEC_ROWS, width), bf16),
        compiler_params=_cparams("parallel"),
        name="decode_diff",
    )(page_table, *([pool] * n_pages), qd, k_new, v_new, lam_qk_o, sub_g)


def _pad_rows(a, n):
    return jnp.pad(a, ((0, 0), (0, n - a.shape[1])) + ((0, 0),) * (a.ndim - 2))


def _trunk(x, mods, qpos0, page_table, caches, wts, params):
    (b_f, w_pool, lam_qk, subln_g, norm_mix_g, norm_ffn_g, final_g) = params
    b, t, d = x.shape
    m = b * t
    decode = caches is not None
    depth = len(mods)
    tm = min(256, m)
    x2d = x.reshape(m, d)
    pos = qpos0 + np.tile(np.arange(t), b)
    tables = _rope_tables(pos)

    if decode:
        tiles_per_mod = 1
        tm_mlp, mlp_tiles_per_mod = tm, 1
        expand = lambda v: jnp.repeat(v, t, axis=0).reshape(m // tm, tm, d)
    else:
        tiles_per_mod = t // tm
        tm_mlp = min(1024, t)
        mlp_tiles_per_mod = t // tm_mlp
        expand = lambda v: v.reshape(b, 1, d)

    per_seq = lambda a, rows=DEC_ROWS: _pad_rows(a.reshape(b, t, a.shape[-1]), rows)
    outs = {}
    for li in range(depth):
        sh1, sc1, gt1, sh2, sc2, gt2 = [expand(v) for v in jnp.split(mods[li], 6, axis=-1)]
        if li % 2 == 0:
            e = li // 2
            bias = jnp.pad(b_f[e].reshape(1, FOX_H), ((0, 0), (0, LANES - FOX_H)))
            po = _proj(x2d, norm_mix_g[li], sc1, sh1, tiles_per_mod, wts["in_even"][e], tables, bias,
                       _even_plan(), _even_out_defs(not decode), tm, t)
            logf = po[O_LOGF][:, :FOX_H].reshape(b, t, FOX_H)
            if decode:
                kv_leaf = lambda oi, heads: po[oi].reshape(b, t, 2, heads, HD)
            else:
                kv_leaf = lambda oi, heads: jnp.transpose(po[oi].reshape(b, 2, heads, HD, t), (0, 4, 1, 2, 3))
            outs.setdefault("cmp", []).append(kv_leaf(O_KVC, NSA_KVH))
            outs.setdefault("slc", []).append(kv_leaf(O_KVS, NSA_KVH))
            outs.setdefault("fox", []).append(kv_leaf(O_KVF, FOX_H))
            outs.setdefault("logf", []).append(logf)
            kvw_new = kv_leaf(O_KVW, NSA_KVH)
            if not decode:
                outs.setdefault("win", []).append(kvw_new[:, t - min(WINDOW, t):])
                qn, qf = po[O_QN].reshape(b, t, NSA_H * HD), po[O_QF].reshape(b, t, FOX_H * HD)
                cmp = _pad_rows(_compress(po[O_KVC_ROWS].reshape(b, t, 4 * HD), w_pool[e]), LANES)
                ck, cv = cmp[:, :, :LANES].astype(bf16), cmp[:, :, LANES:].astype(bf16)
                o_cmp, mb = _cmp_topk(qn, ck, cv, tq=min(128, t), nb_valid=t // CMP_BLOCK)
                lane_slab = lambda oi, w: po[oi].reshape(b, t, w)
                o_s = _flash("slc", qn, lane_slab(O_KS, LANES), lane_slab(O_VS, LANES), [mb],
                             tq=min(128, t), tk=min(FLASH_TK, t), out_dtype=f32)
                o_w = _flash("win", qn, lane_slab(O_KW, LANES), lane_slab(O_VW, LANES), [],
                             tq=min(256, t), tk=min(512, t), out_dtype=f32)
                tk_f = min(FLASH_TK, t)
                f_t = _cumsum_lanes(jnp.transpose(logf, (0, 2, 1)), min(512, t))
                fk = jnp.transpose(f_t.reshape(b, FOX_H // 2, 2, t // tk_f, tk_f), (0, 1, 3, 2, 4))
                fq = f_t.reshape(b, FOX_H // 2, 2, t, 1)
                o_f = _flash("fox", qf, lane_slab(O_KF, 4 * LANES), lane_slab(O_VF, 4 * LANES), [fk, fq],
                             tq=min(512, t), tk=tk_f, out_dtype=bf16)
                acts = [a.reshape(m, a.shape[-1]) for a in (o_cmp, o_s, o_w)] + [po[O_GATE], o_f.reshape(m, -1)]
            else:
                c_cmp, c_slc, c_win, c_fox, c_logf, _ = caches
                outs.setdefault("win", []).append(jnp.concatenate([c_win[e], kvw_new], axis=1)[:, t:])
                kt_view = lambda c: jnp.transpose(c, (0, 2, 3, 4, 1))
                qn = per_seq(po[O_QN])
                o_cmp, o_s, o_w = _dec_nsa(page_table, qn, kt_view(c_cmp[e]), kt_view(c_slc[e]),
                                           kt_view(c_win[e]), per_seq(po[O_KVS]), per_seq(po[O_KVW]),
                                           w_pool[e], qpos0=qpos0, n_new=t)
                qf = po[O_QF].reshape(b, t, 1, FOX_H * HD)
                head_of_lane = np.arange(FOX_H * HD) // HD
                own = jnp.asarray(head_of_lane[None, :] == np.arange(FOX_H)[:, None], dtype=bf16)
                qbd = (qf * own[None, None]).reshape(b, t * FOX_H, FOX_H * HD)
                lf_new = jnp.pad(jnp.transpose(logf, (0, 2, 1)), ((0, 0), (0, 0), (0, LANES - t)))
                o_f = _dec_fox(page_table, qbd, kt_view(c_fox[e]), jnp.transpose(c_logf[e], (0, 2, 1)),
                               per_seq(po[O_KVF]), lf_new, n_new=t)
                acts = [a[:, :t].reshape(m, a.shape[-1]) for a in (o_cmp, o_s, o_w)]
                acts += [po[O_GATE], o_f[:, :t].reshape(m, -1)]
            x2d = _outproj(_outproj_even_body, x2d, gt1, tiles_per_mod, acts, wts["out_even"][e], tm,
                           "outproj_even")
        else:
            o = li // 2
            po = _proj(x2d, norm_mix_g[li], sc1, sh1, tiles_per_mod, wts["in_odd"][o], tables,
                       jnp.zeros((1, LANES), f32), _odd_plan(), ODD_OUT_DEFS, tm, t)
            outs.setdefault("diff", []).append(po[O_KVD].reshape(b, t, 2, DIFF_H, 2 * HD))
            lam_init = 0.8 - 0.6 * math.exp(-0.3 * li)
            sub_g = subln_g[o].reshape(1, 2 * HD)
            width = 2 * DIFF_H * HD
            if not decode:
                o_d = _flash("diff", po[O_QD].reshape(b, t, width), po[O_KD].reshape(b, t, width),
                             po[O_VD].reshape(b, t, width), [lam_qk[o], sub_g], tq=min(512, t), tk=min(FLASH_TK, t),
                             out_dtype=bf16, lam_init=lam_init)
            else:
                kvd = po[O_KVD].reshape(b, t, 2, DIFF_H, 2 * HD)
                rows_new = lambda a: a.reshape(b, t * DIFF_H, 2 * HD)
                o_d = _dec_diff(page_table, per_seq(po[O_QD]), caches[5][o], rows_new(kvd[:, :, 0]),
                                rows_new(kvd[:, :, 1]), lam_qk[o], sub_g, n_new=t, lam_init=lam_init)[:, :t]
            x2d = _outproj(_outproj_odd_body, x2d, gt1, tiles_per_mod, [o_d.reshape(m, d)],
                           wts["out_odd"][o], tm, "outproj_odd")
        x2d = _mlp(x2d, norm_ffn_g[li], sc2, sh2, gt2, mlp_tiles_per_mod, wts["ff_in"][li], wts["ff_out"][li],
                   final_g, li == depth - 1, tm_mlp, 1024)
    stack = lambda name: jnp.stack(outs[name], 0)
    return (x2d.reshape(b, t, d), stack("cmp"), stack("slc"), stack("win"), stack("fox"), stack("logf"),
            stack("diff"))


def kernel(x_prompt, x_sample, cache_nsa_cmp_kv, cache_nsa_slc_kv, cache_nsa_win_kv, cache_fox_kv,
           cache_fox_logf, cache_diff_kv, page_table, c_prompt, c_sample, w_in_even, b_f, w_pool,
           w_out_even, w_in_odd, lam_qk, subln_g, w_out_odd, norm_mix_g, norm_ffn_g, w_ada, b_ada,
           w_ff_in, w_ff_out, final_g):
    depth = w_ada.shape[0]
    nb_p, nb_s = c_prompt.shape[0], c_sample.shape[0]
    past_len = page_table.shape[1] * PAGE_SIZE

    perm_rows = lambda w: jnp.concatenate(
        [w[:NSA_H * HD].reshape(NSA_H, HD, -1)[NSA_PERM, :, :].reshape(NSA_H * HD, -1), w[NSA_H * HD:]], axis=0)
    wts = dict(
        in_even=[_even_weight(w_in_even[e]) for e in range(w_in_even.shape[0])],
        out_even=[perm_rows(w_out_even[e]).astype(bf16) for e in range(w_out_even.shape[0])],
        in_odd=[w_in_odd[o].astype(bf16) for o in range(w_in_odd.shape[0])],
        out_odd=[w_out_odd[o].astype(bf16) for o in range(w_out_odd.shape[0])],
        ff_in=[w_ff_in[l].astype(bf16) for l in range(depth)],
        ff_out=[w_ff_out[l].astype(bf16) for l in range(depth)],
    )
    rows = -(-(nb_p + nb_s) // SUBLANES) * SUBLANES
    c_all = jnp.pad(jnp.concatenate([c_prompt, c_sample], axis=0), ((0, rows - nb_p - nb_s), (0, 0)))
    mod_all = _ada(c_all, w_ada.astype(bf16), b_ada)
    mods_p = [mod_all[l, :nb_p] for l in range(depth)]
    mods_s = [mod_all[l, nb_p:nb_p + nb_s] for l in range(depth)]

    params = (b_f, w_pool, lam_qk, subln_g, norm_mix_g, norm_ffn_g, final_g)
    caches = (cache_nsa_cmp_kv, cache_nsa_slc_kv, cache_nsa_win_kv, cache_fox_kv, cache_fox_logf, cache_diff_kv)
    y_p, p_cmp, p_slc, p_win, p_fox, p_logf, p_diff = _trunk(x_prompt, mods_p, 0, None, None, wts, params)
    y_s, s_cmp, s_slc, s_win, s_fox, s_logf, s_diff = _trunk(x_sample, mods_s, past_len, page_table, caches,
                                                             wts, params)
    return (y_p, y_s, p_cmp, s_cmp, p_slc, s_slc, p_win, s_win, p_fox, s_fox, p_logf, s_logf, p_diff, s_diff)
```

```python
import functools
import math

import numpy as np
import jax
import jax.numpy as jnp
from jax import lax
from jax.experimental import pallas as pl
from jax.experimental.pallas import tpu as pltpu

HD = 64
ROPE_DIM = HD // 4
ROPE_HALF = ROPE_DIM // 2
ROPE_THETA = 500000.0
NSA_H = 8
NSA_KVH = 2
CMP_BLOCK = 64
CMP_SHIFT = 6
N_SEL = 16
WINDOW = 512
FOX_H = 8
DIFF_H = 8
EPS = 1e-6
PAGE_SIZE = 128

LANES = 128
SUBLANES = 8
VMEM_LIMIT_BYTES = 48 * 1024 * 1024

NEG_MASK = -1e30
NEG_BLOCK = -1e9
PAGES_PER_STEP = 4
FLASH_TK = 1024

NSA_PERM = (0, 4, 1, 5, 2, 6, 3, 7)

f32 = jnp.float32
bf16 = jnp.bfloat16


def _cparams(*sem):
    return pltpu.CompilerParams(dimension_semantics=sem, vmem_limit_bytes=VMEM_LIMIT_BYTES)


def _dot(a, b):
    return jnp.dot(a, b, preferred_element_type=f32)


def _dot_nt(a, b):
    return lax.dot_general(a, b, (((1,), (1,)), ((), ())), preferred_element_type=f32)


def _rms_mod(x, g, sc, sh):
    y = x * lax.rsqrt(jnp.mean(x * x, axis=-1, keepdims=True) + EPS)
    return (y * g) * (1.0 + sc) + sh


def _split_dot(x, w, parts):
    acc = None
    for _ in range(parts):
        piece = x.astype(bf16)
        term = _dot(piece, w)
        acc = term if acc is None else acc + term
        x = x - piece.astype(f32)
    return acc


def _split_dot_rhs(w, x, parts):
    acc = None
    for _ in range(parts):
        piece = x.astype(bf16)
        term = _dot(w, piece)
        acc = term if acc is None else acc + term
        x = x - piece.astype(f32)
    return acc


def _ada_body(c_ref, w_ref, b_ref, o_ref):
    c = c_ref[...]
    a = (c * jax.nn.sigmoid(c)).astype(bf16)
    o_ref[0] = _dot(a, w_ref[0]) + b_ref[0]


def _ada(c_all, w_ada_b, b_ada):
    depth, d, n = w_ada_b.shape
    rows = c_all.shape[0]
    tn = n // 4
    return pl.pallas_call(
        _ada_body,
        out_shape=jax.ShapeDtypeStruct((depth, rows, n), f32),
        grid=(depth, n // tn),
        in_specs=[pl.BlockSpec((rows, d), lambda l, j: (0, 0)),
                  pl.BlockSpec((1, d, tn), lambda l, j: (l, 0, j)),
                  pl.BlockSpec((1, 1, tn), lambda l, j: (l, 0, j))],
        out_specs=pl.BlockSpec((1, rows, tn), lambda l, j: (l, 0, j)),
        compiler_params=_cparams("parallel", "parallel"),
        name="ada_mod",
    )(c_all, w_ada_b, b_ada.reshape(depth, 1, n))


def _rope_group(a, cos, sa, sb):
    return a * cos + pltpu.roll(a, LANES - ROPE_HALF, 1) * sa + pltpu.roll(a, ROPE_HALF, 1) * sb


def _proj_body(plan, x_ref, g_ref, sc_ref, sh_ref, w_ref, cos_ref, sa_ref, sb_ref, bias_ref, *out_refs):
    h = _rms_mod(x_ref[...], g_ref[...], sc_ref[0], sh_ref[0]).astype(bf16)
    cos, sa, sb = cos_ref[...], sa_ref[...], sb_ref[...]
    for src, groups in plan:
        width = LANES * len(groups)
        acc = _dot(h, w_ref[:, src:src + width])
        for gi, (kind, scale, dests) in enumerate(groups):
            a = acc[:, gi * LANES:(gi + 1) * LANES]
            if kind == "rope":
                a = _rope_group(a, cos, sa, sb)
            elif kind == "sigmoid":
                a = jax.nn.sigmoid(a)
            elif kind == "logsig":
                z = a + bias_ref[...]
                a = jnp.minimum(z, 0.0) - jnp.log1p(jnp.exp(-jnp.abs(z)))
            if scale != 1.0:
                a = a * scale
            for oi, col in dests:
                if len(out_refs[oi].shape) == 3:
                    out_refs[oi][0, col:col + LANES, :] = a.T.astype(out_refs[oi].dtype)
                else:
                    out_refs[oi][:, col:col + LANES] = a.astype(out_refs[oi].dtype)


def _proj(x2d, g, sc, sh, tiles_per_mod, w_b, tables, bias, plan, out_defs, tm, seq_len):
    m, d = x2d.shape
    n = w_b.shape[1]
    cos, sa, sb = tables
    mod_rows = sc.shape[1]
    row = lambda i: (i, 0)
    const = lambda i: (0, 0)
    mod = lambda i: (i // tiles_per_mod, 0, 0)
    tiles_per_seq = seq_len // tm
    out_shape, out_specs = [], []
    for w, dt, position_minor in out_defs:
        if position_minor:
            out_shape.append(jax.ShapeDtypeStruct((m // seq_len, w, seq_len), dt))
            out_specs.append(pl.BlockSpec((1, w, tm), lambda i: (i // tiles_per_seq, 0, i % tiles_per_seq)))
        else:
            out_shape.append(jax.ShapeDtypeStruct((m, w), dt))
            out_specs.append(pl.BlockSpec((tm, w), row))
    return pl.pallas_call(
        functools.partial(_proj_body, plan),
        out_shape=out_shape,
        grid=(m // tm,),
        in_specs=[pl.BlockSpec((tm, d), row),
                  pl.BlockSpec((1, d), const),
                  pl.BlockSpec((1, mod_rows, d), mod),
                  pl.BlockSpec((1, mod_rows, d), mod),
                  pl.BlockSpec((d, n), const, pipeline_mode=pl.Buffered(1)),
                  pl.BlockSpec((tm, LANES), row),
                  pl.BlockSpec((tm, LANES), row),
                  pl.BlockSpec((tm, LANES), row),
                  pl.BlockSpec((1, LANES), const)],
        out_specs=out_specs,
        compiler_params=_cparams("parallel"),
        name="norm_mod_proj",
    )(x2d, g.reshape(1, d), sc, sh, w_b, cos, sa, sb, bias)


def _rope_tables(pos):
    inv = np.float32(ROPE_THETA) ** (-np.arange(ROPE_HALF, dtype=np.float32) / np.float32(ROPE_HALF))
    ang = (pos.astype(np.float32)[:, None] * inv[None, :]).astype(np.float64)
    n = pos.shape[0]
    cos = np.ones((n, HD), np.float64)
    sa = np.zeros((n, HD), np.float64)
    sb = np.zeros((n, HD), np.float64)
    cos[:, :ROPE_HALF] = np.cos(ang)
    cos[:, ROPE_HALF:ROPE_DIM] = np.cos(ang)
    sa[:, :ROPE_HALF] = -np.sin(ang)
    sb[:, ROPE_HALF:ROPE_DIM] = np.sin(ang)
    tile = lambda t: jnp.asarray(np.tile(t, (1, LANES // HD)).astype(np.float32))
    return tile(cos), tile(sa), tile(sb)


EV_QN, EV_KVC, EV_KVS, EV_KVW, EV_QF, EV_KVF, EV_GATE, EV_FF = 0, 512, 768, 1024, 1280, 1792, 2816, 4352


def _even_weight(w):
    d = w.shape[0]
    sizes = [NSA_H * HD, 2 * NSA_KVH * HD, 2 * NSA_KVH * HD, 2 * NSA_KVH * HD, 3 * NSA_H, FOX_H * HD,
             2 * FOX_H * HD, FOX_H]
    offs = np.concatenate([[0], np.cumsum(sizes)])
    q_n, kv_c, kv_s, kv_w, gates, q_f, kv_f, f_f = [w[:, offs[i]:offs[i + 1]] for i in range(8)]
    q_n = q_n.reshape(d, NSA_H, HD)[:, NSA_PERM, :].reshape(d, NSA_H * HD)
    gates = gates.reshape(d, NSA_H, 3)[:, NSA_PERM, :]
    gates = jnp.repeat(jnp.transpose(gates, (0, 2, 1))[..., None], HD, axis=-1).reshape(d, 3 * NSA_H * HD)
    f_f = jnp.pad(f_f, ((0, 0), (0, LANES - FOX_H)))
    return jnp.concatenate([q_n, kv_c, kv_s, kv_w, q_f, kv_f, gates, f_f], axis=1).astype(bf16)


(O_QN, O_KVC, O_KVS, O_KS, O_VS, O_KVW, O_KW, O_VW, O_QF, O_KVF, O_KF, O_VF, O_GATE, O_LOGF, O_KVC_ROWS) = range(15)


def _even_out_defs(position_minor):
    pm = position_minor
    return [(512, bf16, False), (256, f32, pm), (256, f32, pm), (128, bf16, False), (128, bf16, False),
            (256, f32, pm), (128, bf16, False), (128, bf16, False), (512, bf16, False), (1024, f32, pm),
            (512, bf16, False), (512, bf16, False), (1536, f32, False), (128, f32, False), (256, f32, False)]


def _even_plan():
    scale = HD ** -0.5
    plan = [(EV_QN, [("rope", scale, [(O_QN, LANES * i)]) for i in range(4)]),
            (EV_KVC, [("rope", 1.0, [(O_KVC, 0), (O_KVC_ROWS, 0)]),
                      ("plain", 1.0, [(O_KVC, LANES), (O_KVC_ROWS, LANES)])]),
            (EV_KVS, [("rope", 1.0, [(O_KVS, 0), (O_KS, 0)]), ("plain", 1.0, [(O_KVS, LANES), (O_VS, 0)])]),
            (EV_KVW, [("rope", 1.0, [(O_KVW, 0), (O_KW, 0)]), ("plain", 1.0, [(O_KVW, LANES), (O_VW, 0)])]),
            (EV_QF, [("plain", scale, [(O_QF, LANES * i)]) for i in range(4)]),
            (EV_KVF, [("plain", 1.0, [(O_KVF, LANES * i), (O_KF, LANES * i)]) for i in range(4)]),
            (EV_KVF + 512, [("plain", 1.0, [(O_KVF, 512 + LANES * i), (O_VF, LANES * i)]) for i in range(4)])]
    for j in range(3):
        plan.append((EV_GATE + 512 * j, [("sigmoid", 1.0, [(O_GATE, 512 * j + LANES * i)]) for i in range(4)]))
    plan.append((EV_FF, [("logsig", 1.0, [(O_LOGF, 0)])]))
    return plan


(O_QD, O_KVD, O_KD, O_VD) = range(4)
ODD_OUT_DEFS = [(1024, bf16, False), (2048, f32, False), (1024, bf16, False), (1024, bf16, False)]


def _odd_plan():
    scale = HD ** -0.5
    plan = []
    for c in range(2):
        plan.append((512 * c, [("rope", scale, [(O_QD, 512 * c + LANES * i)]) for i in range(4)]))
    for c in range(2):
        plan.append((1024 + 512 * c,
                     [("rope", 1.0, [(O_KVD, 512 * c + LANES * i), (O_KD, 512 * c + LANES * i)]) for i in range(4)]))
    for c in range(2):
        plan.append((2048 + 512 * c,
                     [("plain", 1.0, [(O_KVD, 1024 + 512 * c + LANES * i), (O_VD, 512 * c + LANES * i)])
                      for i in range(4)]))
    return plan


def _outproj_even_body(x_ref, gt_ref, oc_ref, os_ref, ow_ref, gates_ref, of_ref, w_ref, o_ref):
    nw = NSA_H * HD
    o_nsa = (gates_ref[:, 0:nw] * oc_ref[...] + gates_ref[:, nw:2 * nw] * os_ref[...]
             + gates_ref[:, 2 * nw:3 * nw] * ow_ref[...])
    a = jnp.concatenate([o_nsa.astype(bf16), of_ref[...]], axis=1)
    o_ref[...] = x_ref[...] + gt_ref[0] * _dot(a, w_ref[...])


def _outproj_odd_body(x_ref, gt_ref, od_ref, w_ref, o_ref):
    o_ref[...] = x_ref[...] + gt_ref[0] * _dot(od_ref[...], w_ref[...])


def _outproj(body, x2d, gt, tiles_per_mod, acts, w_b, tm, name):
    m, d = x2d.shape
    row = lambda i: (i, 0)
    mod = lambda i: (i // tiles_per_mod, 0, 0)
    return pl.pallas_call(
        body,
        out_shape=jax.ShapeDtypeStruct((m, d), f32),
        grid=(m // tm,),
        in_specs=[pl.BlockSpec((tm, d), row), pl.BlockSpec((1, gt.shape[1], d), mod)]
                 + [pl.BlockSpec((tm, a.shape[1]), row) for a in acts]
                 + [pl.BlockSpec(w_b.shape, lambda i: (0, 0), pipeline_mode=pl.Buffered(1))],
        out_specs=pl.BlockSpec((tm, d), row),
        compiler_params=_cparams("parallel"),
        name=name,
    )(x2d, gt, *acts, w_b)


def _mlp_body(final, x_ref, g_ref, sc_ref, sh_ref, gt_ref, w1_ref, w2_ref, fg_ref, o_ref, h_ref, acc_ref):
    f = pl.program_id(1)

    @pl.when(f == 0)
    def _():
        h_ref[...] = _rms_mod(x_ref[...], g_ref[...], sc_ref[0], sh_ref[0]).astype(bf16)
        acc_ref[...] = jnp.zeros_like(acc_ref)

    a = jnp.maximum(_dot(h_ref[...], w1_ref[...]), 0.0)
    acc_ref[...] += _dot((a * a).astype(bf16), w2_ref[...])

    @pl.when(f == pl.num_programs(1) - 1)
    def _():
        y = x_ref[...] + gt_ref[0] * acc_ref[...]
        if final:
            y = y * lax.rsqrt(jnp.mean(y * y, axis=-1, keepdims=True) + EPS) * fg_ref[...]
        o_ref[...] = y


def _mlp(x2d, g, sc, sh, gt, tiles_per_mod, w1_b, w2_b, final_g, final, tm, tf):
    m, d = x2d.shape
    dff = w1_b.shape[1]
    mod_rows = sc.shape[1]
    row = lambda i, f: (i, 0)
    const = lambda i, f: (0, 0)
    mod = lambda i, f: (i // tiles_per_mod, 0, 0)
    return pl.pallas_call(
        functools.partial(_mlp_body, final),
        out_shape=jax.ShapeDtypeStruct((m, d), f32),
        grid=(m // tm, dff // tf),
        in_specs=[pl.BlockSpec((tm, d), row), pl.BlockSpec((1, d), const),
                  pl.BlockSpec((1, mod_rows, d), mod), pl.BlockSpec((1, mod_rows, d), mod),
                  pl.BlockSpec((1, mod_rows, d), mod),
                  pl.BlockSpec((d, tf), lambda i, f: (0, f)), pl.BlockSpec((tf, d), lambda i, f: (f, 0)),
                  pl.BlockSpec((1, d), const)],
        out_specs=pl.BlockSpec((tm, d), row),
        scratch_shapes=[pltpu.VMEM((tm, d), bf16), pltpu.VMEM((tm, d), f32)],
        compiler_params=_cparams("parallel", "arbitrary"),
        name="relu2_mlp",
    )(x2d, g.reshape(1, d), sc, sh, gt, w1_b, w2_b, final_g.reshape(1, d))


def _compress_body(x_ref, wp_ref, o_ref):
    wp = wp_ref[...]
    rows = []
    for blk in range(x_ref.shape[1] // CMP_BLOCK):
        rows.append(jnp.sum(x_ref[0, blk * CMP_BLOCK:(blk + 1) * CMP_BLOCK, :] * wp, axis=0, keepdims=True))
    o_ref[0] = jnp.concatenate(rows, axis=0)


def _compress(kv, w_pool_e):
    b, t, c = kv.shape
    wp = jnp.repeat(jnp.transpose(w_pool_e, (1, 0, 2))[..., None], HD, axis=-1).reshape(CMP_BLOCK, c)
    rows_in = PAGES_PER_STEP * PAGE_SIZE
    rows_out = rows_in // CMP_BLOCK
    return pl.pallas_call(
        _compress_body,
        out_shape=jax.ShapeDtypeStruct((b, t // CMP_BLOCK, c), f32),
        grid=(b, t // rows_in),
        in_specs=[pl.BlockSpec((1, rows_in, c), lambda bi, g: (bi, g, 0)),
                  pl.BlockSpec((CMP_BLOCK, c), lambda bi, g: (0, 0))],
        out_specs=pl.BlockSpec((1, rows_out, c), lambda bi, g: (bi, g, 0)),
        compiler_params=_cparams("parallel", "parallel"),
        name="nsa_compress",
    )(kv, wp)


def _cumsum_body(x_ref, u_ref, o_ref, carry_ref):
    @pl.when(pl.program_id(1) == 0)
    def _():
        carry_ref[...] = jnp.zeros_like(carry_ref)

    cs = _split_dot(x_ref[0], u_ref[...], 3) + carry_ref[...]
    o_ref[0] = cs
    carry_ref[...] = cs[:, cs.shape[1] - 1:]


def _upper_ones(n):
    return jnp.asarray(np.triu(np.ones((n, n), np.float32)), dtype=bf16)


def _cumsum_lanes(x_t, tc):
    b, h, l = x_t.shape
    return pl.pallas_call(
        _cumsum_body,
        out_shape=jax.ShapeDtypeStruct((b, h, l), f32),
        grid=(b, l // tc),
        in_specs=[pl.BlockSpec((1, h, tc), lambda bi, c: (bi, 0, c)),
                  pl.BlockSpec((tc, tc), lambda bi, c: (0, 0))],
        out_specs=pl.BlockSpec((1, h, tc), lambda bi, c: (bi, 0, c)),
        scratch_shapes=[pltpu.VMEM((h, 1), f32)],
        compiler_params=_cparams("parallel", "arbitrary"),
        name="logf_cumsum",
    )(x_t, _upper_ones(tc))


def _stack_heads(q, qs_ref, n_slab, tq):
    lane = lax.broadcasted_iota(jnp.int32, (tq, LANES), 1)
    low = lane < HD
    for s in range(n_slab):
        slab = q[:, s * LANES:(s + 1) * LANES].astype(f32)
        qs_ref[(2 * s) * tq:(2 * s + 1) * tq, 0:LANES] = jnp.where(low, slab, 0.0).astype(qs_ref.dtype)
        qs_ref[(2 * s + 1) * tq:(2 * s + 2) * tq, 0:LANES] = jnp.where(low, 0.0, slab).astype(qs_ref.dtype)


def _unstack_heads(o, n_slab, tq):
    lane = lax.broadcasted_iota(jnp.int32, (tq, LANES), 1)
    low = lane < HD
    slabs = [jnp.where(low, o[(2 * s) * tq:(2 * s + 1) * tq], o[(2 * s + 1) * tq:(2 * s + 2) * tq])
             for s in range(n_slab)]
    return slabs[0] if n_slab == 1 else jnp.concatenate(slabs, axis=1)


def _masked_softmax(s, valid):
    sm = jnp.where(valid, s, NEG_MASK)
    m = jnp.max(sm, axis=1, keepdims=True)
    p = jnp.where(valid, jnp.exp(sm - m), 0.0)
    return p / jnp.maximum(jnp.sum(p, axis=1, keepdims=True), 1e-30)


def _select_blocks(p, tg, tq, nb_valid):
    n = lax.broadcasted_iota(jnp.int32, (1, LANES), 1)
    cur = jnp.right_shift(tg, CMP_SHIFT)
    n_cand = jnp.minimum(cur, nb_valid)
    out = []
    for g in range(NSA_KVH):
        imp = p[g * tq:(g + 1) * tq]
        for s_i in range(1, NSA_H // 2):
            imp = imp + p[(2 * s_i + g) * tq:(2 * s_i + g + 1) * tq]
        work = jnp.where(n < n_cand, imp, -1.0)
        sel = n == cur
        for it in range(N_SEL - 1):
            hit = n == jnp.argmax(work, axis=1, keepdims=True).astype(jnp.int32)
            sel = jnp.logical_or(sel, jnp.logical_and(hit, it < n_cand))
            work = jnp.where(hit, -2.0, work)
        out.append(jnp.where(sel, 0.0, NEG_BLOCK))
    return out


def _select_blocks_ranked(p, tg, tq, nb_valid):
    n = lax.broadcasted_iota(jnp.int32, (1, LANES), 1)
    n_col = lax.broadcasted_iota(jnp.int32, (LANES, 1), 0)
    cur = jnp.right_shift(tg, CMP_SHIFT)
    n_cand = jnp.minimum(cur, nb_valid)
    n_pick = jnp.minimum(n_cand, N_SEL - 1).astype(f32)
    out = []
    for g in range(NSA_KVH):
        imp = p[g * tq:(g + 1) * tq]
        for s_i in range(1, NSA_H // 2):
            imp = imp + p[(2 * s_i + g) * tq:(2 * s_i + g + 1) * tq]
        work = jnp.where(n < n_cand, imp, -1.0)
        work_t = jnp.concatenate([work, jnp.zeros((LANES - tq, LANES), f32)], axis=0).T
        ranks = []
        for r in range(tq):
            other = work_t[:, r:r + 1]
            mine = work[r:r + 1, :]
            ahead = jnp.logical_or(other > mine, jnp.logical_and(other == mine, n_col < n))
            ranks.append(jnp.sum(jnp.where(ahead, 1.0, 0.0), axis=0, keepdims=True))
        rank = jnp.concatenate(ranks, axis=0)
        sel = jnp.logical_or(n == cur, rank < n_pick)
        out.append(jnp.where(sel, 0.0, NEG_BLOCK))
    return out


def _flash_body(cfg, *refs):
    mode, n_slab, tq, tk, n_chunks = cfg["mode"], cfg["n_slab"], cfg["tq"], cfg["tk"], cfg["n_chunks"]
    window = cfg["window"]
    rows = 2 * n_slab * tq
    q_ref, k_ref, v_ref = refs[:3]
    extra = refs[3:-5]
    o_ref, qs_ref, s_ref, m_ref, acc_ref = refs[-5:]
    qi = pl.program_id(2)

    _stack_heads(q_ref[0], qs_ref, n_slab, tq)
    if mode == "slc":
        mb_ref = extra[0]
        for s in range(n_slab):
            for g in range(2):
                qs_ref[(2 * s + g) * tq:(2 * s + g + 1) * tq, LANES:2 * LANES] = mb_ref[0, g]
    m_ref[...] = jnp.full_like(m_ref, NEG_MASK)
    acc_ref[...] = jnp.zeros_like(acc_ref)

    q_lo = qi * tq
    qpos = q_lo + (lax.broadcasted_iota(jnp.int32, (rows, 1), 0) & (tq - 1))

    def scores(c):
        ks = pl.multiple_of(c * tk, tk)
        k = k_ref[0, pl.ds(ks, tk), :]
        if mode == "slc":
            blk = jnp.right_shift(ks + lax.broadcasted_iota(jnp.int32, (tk, LANES), 0), CMP_SHIFT)
            hot = jnp.where(blk == lax.broadcasted_iota(jnp.int32, (tk, LANES), 1), 1.0, 0.0).astype(bf16)
            k = jnp.concatenate([k, hot], axis=1)
        return _dot_nt(qs_ref[...], k)

    def update(slot, c, masked, width=tk):
        ks = pl.multiple_of(c * tk, tk)
        s = s_ref[slot, :, 0:width]
        row_bias = 0.0
        if mode == "fox":
            fk_ref, fq_ref = extra
            fk = fk_ref[0, 0, c][:, 0:width]
            s = jnp.concatenate([s[0:tq] - fk[0:1], s[tq:2 * tq] - fk[1:2]], axis=0)
            row_bias = jnp.concatenate([fq_ref[0, 0, 0], fq_ref[0, 0, 1]], axis=0)
        if masked:
            kpos = ks + lax.broadcasted_iota(jnp.int32, (1, width), 1)
            mask = kpos <= qpos
            if window is not None:
                mask = jnp.logical_and(mask, (qpos - kpos) < window)
            s = jnp.where(mask, s, NEG_MASK)
        m_old = m_ref[...]
        m_new = jnp.maximum(m_old, jnp.max(s, axis=1, keepdims=True) + row_bias)
        alpha = jnp.exp(m_old - m_new)
        p = jnp.exp((s + (row_bias - m_new)).astype(bf16))
        v_ones = jnp.concatenate([v_ref[0, pl.ds(ks, width), :], jnp.ones((width, LANES), bf16)], axis=1)
        acc_ref[...] = alpha * acc_ref[...] + _dot(p, v_ones)
        m_ref[...] = m_new

    def issue(c, slot):
        s_ref[slot] = scores(c)

    if window is not None:
        c_lo = jnp.maximum(q_lo - window + 1, 0) // tk
        c_last = (q_lo + tq - 1) // tk
        issue(c_lo, 0)
        issue(c_last, 1)
        update(0, c_lo, True)

        @pl.when(c_last > c_lo)
        def _():
            update(1, c_last, True)
    else:
        n_full = (q_lo + 1) // tk
        visible = q_lo + tq - n_full * tk
        widths = [w for w in (tk // 4, tk // 2, tk) if w >= max(tq, 2 * LANES)]

        def diagonal(slot):
            for idx, w in enumerate(widths):
                lower = widths[idx - 1] if idx else 0

                @pl.when(jnp.logical_and(visible > lower, visible <= w))
                def _():
                    update(slot, n_full, True, w)

        issue(0, 0)

        def pair(j, carry):
            c = 2 * j
            issue(c + 1, 1)
            update(0, c, False)
            issue(c + 2, 0)
            update(1, c + 1, False)
            return carry

        lax.fori_loop(0, n_full // 2, pair, 0)

        @pl.when((n_full & 1) == 1)
        def _():
            issue(n_full, 1)
            update(0, n_full - 1, False)
            diagonal(1)

        @pl.when((n_full & 1) == 0)
        def _():
            diagonal(0)

    acc = acc_ref[...]
    o = acc[:, :LANES] / jnp.maximum(acc[:, LANES:], 1e-30)
    if mode == "diff":
        lq_ref, sg_ref = extra
        o_ref[0] = _diff_combine(o[0:tq], o[tq:2 * tq], lq_ref[...], sg_ref[...], cfg["lam_init"]).astype(o_ref.dtype)
    else:
        o_ref[0] = _unstack_heads(o, n_slab, tq).astype(o_ref.dtype)


def _diff_combine(o1, o2, lp, sub_g, lam_init):
    lam = (jnp.exp(jnp.sum(lp[0:1] * lp[1:2], axis=1, keepdims=True))
           - jnp.exp(jnp.sum(lp[2:3] * lp[3:4], axis=1, keepdims=True)) + lam_init)
    d = o1 - lam * o2
    d = d * lax.rsqrt(jnp.mean(d * d, axis=-1, keepdims=True) + EPS) * sub_g
    return d * (1.0 - lam_init)


def _flash(mode, q, k, v, extra, *, tq, tk, out_dtype, lam_init=0.0):
    b, t_q, cq = q.shape
    t_k = k.shape[1]
    n_slab = 4 if mode in ("slc", "win") else 1
    n_grid_slab = cq // (LANES * n_slab)
    n_chunks = t_k // tk
    rows = 2 * n_slab * tq
    ck = 2 * LANES if mode == "slc" else LANES
    assert tk % tq == 0
    if mode == "win":
        assert tk == WINDOW
    score_slots = (2, rows, tk)
    cfg = dict(mode=mode, n_slab=n_slab, tq=tq, tk=tk, n_chunks=n_chunks,
               window=WINDOW if mode == "win" else None, lam_init=lam_init)
    qw = LANES * n_slab
    in_specs = [pl.BlockSpec((1, tq, qw), lambda bi, s, i: (bi, i, s)),
                pl.BlockSpec((1, t_k, LANES), lambda bi, s, i: (bi, 0, s)),
                pl.BlockSpec((1, t_k, LANES), lambda bi, s, i: (bi, 0, s))]
    if mode == "fox":
        in_specs += [pl.BlockSpec((1, 1, n_chunks, 2, tk), lambda bi, s, i: (bi, s, 0, 0, 0)),
                     pl.BlockSpec((1, 1, 2, tq, 1), lambda bi, s, i: (bi, s, 0, i, 0))]
    elif mode == "slc":
        in_specs += [pl.BlockSpec((1, 2, tq, LANES), lambda bi, s, i: (bi, 0, i, 0))]
    elif mode == "diff":
        in_specs += [pl.BlockSpec((4, HD), lambda bi, s, i: (0, 0)),
                     pl.BlockSpec((1, LANES), lambda bi, s, i: (0, 0))]
    return pl.pallas_call(
        functools.partial(_flash_body, cfg),
        out_shape=jax.ShapeDtypeStruct((b, t_q, cq), out_dtype),
        grid=(b, n_grid_slab, t_q // tq),
        in_specs=in_specs,
        out_specs=pl.BlockSpec((1, tq, qw), lambda bi, s, i: (bi, i, s)),
        scratch_shapes=[pltpu.VMEM((rows, ck), bf16), pltpu.VMEM(score_slots, f32), pltpu.VMEM((rows, 1), f32),
                        pltpu.VMEM((rows, 2 * LANES), f32)],
        compiler_params=_cparams("parallel", "parallel", "parallel"),
        name="flash_" + mode,
    )(q, k, v, *extra)


def _cmp_body(tq, nb_valid, q_ref, ck_ref, cv_ref, o_ref, mb_ref, qs_ref):
    n_slab = NSA_H // 2
    rows = 2 * n_slab * tq
    qi = pl.program_id(1)
    _stack_heads(q_ref[0], qs_ref, n_slab, tq)
    s = _dot_nt(qs_ref[...], ck_ref[0])
    t = qi * tq + (lax.broadcasted_iota(jnp.int32, (rows, 1), 0) & (tq - 1))
    n = lax.broadcasted_iota(jnp.int32, (1, LANES), 1)
    valid = jnp.logical_and((n + 1) * CMP_BLOCK - 1 <= t, n < nb_valid)
    p = _masked_softmax(s, valid)
    o_ref[0] = _unstack_heads(_dot(p.astype(bf16), cv_ref[0]), n_slab, tq)
    bias = _select_blocks(p, t[0:tq], tq, nb_valid)
    for g in range(NSA_KVH):
        mb_ref[0, g] = bias[g].astype(bf16)


def _cmp_topk(q, ck, cv, *, tq, nb_valid):
    b, t_q, cq = q.shape
    rows = NSA_H * tq
    return pl.pallas_call(
        functools.partial(_cmp_body, tq, nb_valid),
        out_shape=[jax.ShapeDtypeStruct((b, t_q, cq), f32), jax.ShapeDtypeStruct((b, NSA_KVH, t_q, LANES), bf16)],
        grid=(b, t_q // tq),
        in_specs=[pl.BlockSpec((1, tq, cq), lambda bi, i: (bi, i, 0)),
                  pl.BlockSpec((1, LANES, LANES), lambda bi, i: (bi, 0, 0)),
                  pl.BlockSpec((1, LANES, LANES), lambda bi, i: (bi, 0, 0))],
        out_specs=[pl.BlockSpec((1, tq, cq), lambda bi, i: (bi, i, 0)),
                   pl.BlockSpec((1, NSA_KVH, tq, LANES), lambda bi, i: (bi, 0, i, 0))],
        scratch_shapes=[pltpu.VMEM((rows, LANES), bf16)],
        compiler_params=_cparams("parallel", "parallel"),
        name="nsa_cmp_topk",
    )(q, ck, cv)


DEC_ROWS = SUBLANES
DEC_SEQS = 2


def _pad_keys(new, lo, width):
    x = new[:, lo:lo + width]
    return jnp.concatenate([x, jnp.zeros((LANES - x.shape[0], width), f32)], axis=0).astype(bf16)


def _new_key_mask(rows, n_new):
    qi = lax.broadcasted_iota(jnp.int32, (rows, 1), 0) & (DEC_ROWS - 1)
    kj = lax.broadcasted_iota(jnp.int32, (1, LANES), 1)
    return jnp.logical_and(kj <= qi, kj < n_new)


def _softmax2(s1, s2, mask2):
    s2 = jnp.where(mask2, s2, NEG_MASK)
    m = jnp.maximum(jnp.max(s1, axis=1, keepdims=True), jnp.max(s2, axis=1, keepdims=True))
    p1 = jnp.exp(s1 - m)
    p2 = jnp.where(mask2, jnp.exp(s2 - m), 0.0)
    den = jnp.sum(p1, axis=1, keepdims=True) + jnp.sum(p2, axis=1, keepdims=True)
    return p1, p2, jnp.maximum(den, 1e-30)


def _dec_nsa_body(cfg, pt_ref, *refs):
    n_pages, n_seq = cfg["n_pages"], cfg["n_seq"]
    (q_ref, win_ref, news_ref, neww_ref, wp_ref, e_ref, oht_ref,
     oc_ref, os_ref, ow_ref, qs_ref) = refs[2 * n_seq * n_pages:]
    for j in range(n_seq):
        _dec_nsa_one(cfg, j, refs[j * n_pages:(j + 1) * n_pages],
                     refs[(n_seq + j) * n_pages:(n_seq + j + 1) * n_pages], q_ref, win_ref, news_ref, neww_ref,
                     wp_ref, e_ref, oht_ref, oc_ref, os_ref, ow_ref, qs_ref.at[j])


def _dec_nsa_one(cfg, j, cmp_pages, slc_pages, q_ref, win_ref, news_ref, neww_ref, wp_ref, e_ref, oht_ref,
                 oc_ref, os_ref, ow_ref, qs_ref):
    n_pages, qpos0, n_new, w_buf = cfg["n_pages"], cfg["qpos0"], cfg["n_new"], cfg["w_buf"]
    tq, n_slab = DEC_ROWS, NSA_H // 2
    rows = NSA_H * tq
    gd = NSA_KVH * HD

    _stack_heads(q_ref[j], qs_ref, n_slab, tq)
    qs = qs_ref[...].astype(bf16)
    t = qpos0 + (lax.broadcasted_iota(jnp.int32, (rows, 1), 0) & (tq - 1))
    n = lax.broadcasted_iota(jnp.int32, (1, LANES), 1)
    mask_new = _new_key_mask(rows, n_new)

    def compressed(kv):
        xt = jnp.concatenate([p[0, kv].reshape(gd, PAGE_SIZE) for p in cmp_pages], axis=1)
        w = jnp.concatenate([wp_ref[kv]] * n_pages, axis=1)
        return _split_dot(xt * w, e_ref[...], 2).astype(bf16)

    ck_t, cv_t = compressed(0), compressed(1)
    nb_valid = n_pages * PAGE_SIZE // CMP_BLOCK
    valid = jnp.logical_and((n + 1) * CMP_BLOCK - 1 <= t, n < nb_valid)
    p = _masked_softmax(_dot(qs, ck_t), valid)
    oc_ref[j] = _unstack_heads(_dot_nt(p.astype(bf16), cv_t), n_slab, tq)

    bias = _select_blocks_ranked(p, t[0:tq], tq, nb_valid)
    qq = jnp.concatenate([qs, jnp.concatenate(bias * n_slab, axis=0).astype(bf16)], axis=1)
    ks_t = jnp.concatenate([p_[0, 0].reshape(gd, PAGE_SIZE) for p_ in slc_pages], axis=1).astype(bf16)
    vs_t = jnp.concatenate([p_[0, 1].reshape(gd, PAGE_SIZE) for p_ in slc_pages], axis=1).astype(bf16)
    s1 = _dot(qq, jnp.concatenate([ks_t, oht_ref[...]], axis=0))
    new_s = news_ref[j]
    s2 = _dot_nt(qs, _pad_keys(new_s, 0, gd))
    p1, p2, den = _softmax2(s1, s2, mask_new)
    o = _dot_nt(p1.astype(bf16), vs_t) + _dot(p2.astype(bf16), _pad_keys(new_s, gd, gd))
    os_ref[j] = _unstack_heads(o / den, n_slab, tq)

    kw_t = win_ref[j, 0].reshape(gd, w_buf).astype(bf16)
    vw_t = win_ref[j, 1].reshape(gd, w_buf).astype(bf16)
    kpos = qpos0 - w_buf + lax.broadcasted_iota(jnp.int32, (1, w_buf), 1)
    s1 = jnp.where((t - kpos) < WINDOW, _dot(qs, kw_t), NEG_MASK)
    new_w = neww_ref[j]
    s2 = _dot_nt(qs, _pad_keys(new_w, 0, gd))
    p1, p2, den = _softmax2(s1, s2, mask_new)
    o = _dot_nt(p1.astype(bf16), vw_t) + _dot(p2.astype(bf16), _pad_keys(new_w, gd, gd))
    ow_ref[j] = _unstack_heads(o / den, n_slab, tq)


def _dec_nsa(page_table, qn, cmp_pool, slc_pool, win_t, new_s, new_w, w_pool_e, *, qpos0, n_new):
    b, n_pages = page_table.shape
    w_buf = win_t.shape[-1]
    l_k = n_pages * PAGE_SIZE
    gd = NSA_KVH * HD
    n_seq = DEC_SEQS if b % DEC_SEQS == 0 else 1
    cfg = dict(n_pages=n_pages, n_seq=n_seq, qpos0=qpos0, n_new=n_new, w_buf=w_buf)
    wp = jnp.tile(jnp.repeat(jnp.transpose(w_pool_e, (0, 2, 1)), HD, axis=1), (1, 1, PAGE_SIZE // CMP_BLOCK))
    blk_of = np.arange(l_k) // CMP_BLOCK
    e_all = jnp.asarray(blk_of[:, None] == np.arange(LANES)[None, :], dtype=bf16)
    oh_t = jnp.asarray(np.arange(LANES)[:, None] == blk_of[None, :], dtype=bf16)
    page = lambda j, i: pl.BlockSpec((1, 2, NSA_KVH, HD, PAGE_SIZE),
                                     lambda bi, pt: (pt[n_seq * bi + j, i], 0, 0, 0, 0))
    pages = [page(j, i) for j in range(n_seq) for i in range(n_pages)]
    per_b = lambda shape: pl.BlockSpec((n_seq,) + shape, lambda bi, pt: (bi,) + (0,) * len(shape))
    const = lambda shape: pl.BlockSpec(shape, lambda bi, pt: (0,) * len(shape))
    grid_spec = pltpu.PrefetchScalarGridSpec(
        num_scalar_prefetch=1, grid=(b // n_seq,),
        in_specs=pages * 2
                 + [per_b((DEC_ROWS, NSA_H * HD)), per_b((2, NSA_KVH, HD, w_buf)), per_b((DEC_ROWS, 2 * gd)),
                    per_b((DEC_ROWS, 2 * gd)), const((2, gd, PAGE_SIZE)), const((l_k, LANES)), const((LANES, l_k))],
        out_specs=[per_b((DEC_ROWS, NSA_H * HD))] * 3,
        scratch_shapes=[pltpu.VMEM((n_seq, NSA_H * DEC_ROWS, LANES), f32)])
    return pl.pallas_call(
        functools.partial(_dec_nsa_body, cfg), grid_spec=grid_spec,
        out_shape=[jax.ShapeDtypeStruct((b, DEC_ROWS, NSA_H * HD), f32)] * 3,
        compiler_params=_cparams("parallel"),
        name="decode_nsa",
    )(page_table, *([cmp_pool] * (n_seq * n_pages)), *([slc_pool] * (n_seq * n_pages)), qn, win_t, new_s, new_w,
      wp, e_all, oh_t)


def _dec_fox_body(cfg, pt_ref, *refs):
    n_pages, n_seq = cfg["n_pages"], cfg["n_seq"]
    tail = refs[2 * n_seq * n_pages:]
    for j in range(n_seq):
        _dec_fox_one(cfg, j, refs[j * n_pages:(j + 1) * n_pages],
                     refs[(n_seq + j) * n_pages:(n_seq + j + 1) * n_pages], *tail)


def _dec_fox_one(cfg, j, kv_pages, lf_pages, qbd_ref, new_ref, lfn_ref, u_ref, prev_ref, o_ref):
    n_pages, n_new = cfg["n_pages"], cfg["n_new"]
    hd_all = FOX_H * HD
    qbd = qbd_ref[j]
    u = u_ref[...]

    local = _split_dot(jnp.concatenate([p[0] for p in lf_pages], axis=0), u, 3)
    totals = jnp.broadcast_to(local[:, PAGE_SIZE - 1:], local.shape)
    f_all = local + _split_dot_rhs(prev_ref[...], totals, 3)
    f_k = jnp.concatenate([f_all[i * FOX_H:(i + 1) * FOX_H] for i in range(n_pages)], axis=1)
    carry = f_all[(n_pages - 1) * FOX_H:, PAGE_SIZE - 1:]
    f_new = _split_dot(lfn_ref[j], u, 3)

    k_t = jnp.concatenate([p[0, 0].reshape(hd_all, PAGE_SIZE) for p in kv_pages], axis=1).astype(bf16)
    v_t = jnp.concatenate([p[0, 1].reshape(hd_all, PAGE_SIZE) for p in kv_pages], axis=1).astype(bf16)
    new = new_ref[j]
    s1 = _dot(qbd, k_t)
    s2 = _dot_nt(qbd, _pad_keys(new, 0, hd_all))
    kj = lax.broadcasted_iota(jnp.int32, (1, LANES), 1)
    s1_q, s2_q, m2_q = [], [], []
    for qi in range(n_new):
        f_q = carry + f_new[:, qi:qi + 1]
        s1_q.append(s1[qi * FOX_H:(qi + 1) * FOX_H] + f_q - f_k)
        s2_q.append(s2[qi * FOX_H:(qi + 1) * FOX_H] + f_q - (carry + f_new))
        m2_q.append(jnp.broadcast_to(kj <= qi, (FOX_H, LANES)))
    p1, p2, den = _softmax2(jnp.concatenate(s1_q, axis=0), jnp.concatenate(s2_q, axis=0),
                            jnp.concatenate(m2_q, axis=0))
    o = (_dot_nt(p1.astype(bf16), v_t) + _dot(p2.astype(bf16), _pad_keys(new, hd_all, hd_all))) / den
    head = lax.broadcasted_iota(jnp.int32, (FOX_H, hd_all), 0)
    own = jnp.right_shift(lax.broadcasted_iota(jnp.int32, (FOX_H, hd_all), 1), CMP_SHIFT) == head
    out = [jnp.sum(jnp.where(own, o[qi * FOX_H:(qi + 1) * FOX_H], 0.0), axis=0, keepdims=True)
           for qi in range(n_new)]
    out.append(jnp.zeros((DEC_ROWS - n_new, hd_all), f32))
    o_ref[j] = jnp.concatenate(out, axis=0).astype(o_ref.dtype)


def _dec_fox(page_table, qbd, kv_pool, lf_pool, new_kv, lf_new, *, n_new):
    b, n_pages = page_table.shape
    hd_all = FOX_H * HD
    n_seq = DEC_SEQS if b % DEC_SEQS == 0 else 1
    cfg = dict(n_pages=n_pages, n_seq=n_seq, n_new=n_new)
    kv_page = lambda j, i: pl.BlockSpec((1, 2, FOX_H, HD, PAGE_SIZE),
                                        lambda bi, pt: (pt[n_seq * bi + j, i], 0, 0, 0, 0))
    lf_page = lambda j, i: pl.BlockSpec((1, FOX_H, PAGE_SIZE), lambda bi, pt: (pt[n_seq * bi + j, i], 0, 0))
    per_b = lambda shape: pl.BlockSpec((n_seq,) + shape, lambda bi, pt: (bi,) + (0,) * len(shape))
    seq_pages = [(j, i) for j in range(n_seq) for i in range(n_pages)]
    grid_spec = pltpu.PrefetchScalarGridSpec(
        num_scalar_prefetch=1, grid=(b // n_seq,),
        in_specs=[kv_page(j, i) for j, i in seq_pages] + [lf_page(j, i) for j, i in seq_pages]
                 + [per_b((FOX_H * n_new, hd_all)), per_b((DEC_ROWS, 2 * hd_all)), per_b((FOX_H, LANES)),
                    pl.BlockSpec((PAGE_SIZE, PAGE_SIZE), lambda bi, pt: (0, 0)),
                    pl.BlockSpec((FOX_H * n_pages, FOX_H * n_pages), lambda bi, pt: (0, 0))],
        out_specs=per_b((DEC_ROWS, hd_all)))
    r = np.arange(FOX_H * n_pages)
    prev = jnp.asarray((r[:, None] % FOX_H == r[None, :] % FOX_H) & (r[None, :] // FOX_H < r[:, None] // FOX_H),
                       dtype=bf16)
    return pl.pallas_call(
        functools.partial(_dec_fox_body, cfg), grid_spec=grid_spec,
        out_shape=jax.ShapeDtypeStruct((b, DEC_ROWS, hd_all), bf16),
        compiler_params=_cparams("parallel"),
        name="decode_fox",
    )(page_table, *([kv_pool] * (n_seq * n_pages)), *([lf_pool] * (n_seq * n_pages)), qbd, new_kv, lf_new,
      _upper_ones(PAGE_SIZE), prev)


def _dec_diff_body(cfg, pt_ref, *refs):
    n_pages, n_new, lam_init = cfg["n_pages"], cfg["n_new"], cfg["lam_init"]
    pages = refs[:n_pages]
    q_ref, knew_ref, vnew_ref, lq_ref, sg_ref, o_ref, qs_ref = refs[n_pages:]
    tq = DEC_ROWS
    rows = 2 * DIFF_H * tq
    cols = PAGE_SIZE * DIFF_H
    _stack_heads(q_ref[0], qs_ref, DIFF_H, tq)
    row = lax.broadcasted_iota(jnp.int32, (rows, 1), 0)
    row_head = jnp.right_shift(row, 4)
    qs = qs_ref[...].astype(bf16)
    col = lax.broadcasted_iota(jnp.int32, (1, cols), 1)
    own = (col & (DIFF_H - 1)) == row_head

    def part(s, mask, v_rows):
        s = jnp.where(mask, s, NEG_MASK)
        m = jnp.max(s, axis=1, keepdims=True)
        p = jnp.exp(s - m)
        return m, jnp.sum(p, axis=1, keepdims=True), _dot(p.astype(bf16), v_rows)

    parts = []
    for p in pages:
        k_rows = p[0, :, 0].reshape(cols, 2 * HD).astype(bf16)
        v_rows = p[0, :, 1].reshape(cols, 2 * HD).astype(bf16)
        parts.append(part(_dot_nt(qs, k_rows), own, v_rows))
    pad = jnp.zeros((LANES - knew_ref.shape[1], 2 * HD), f32)
    k_new = jnp.concatenate([knew_ref[0], pad], axis=0).astype(bf16)
    v_new = jnp.concatenate([vnew_ref[0], pad], axis=0).astype(bf16)
    lane = lax.broadcasted_iota(jnp.int32, (1, LANES), 1)
    mask_new = jnp.logical_and(jnp.logical_and((lane & (DIFF_H - 1)) == row_head,
                                               jnp.right_shift(lane, 3) <= (row & (tq - 1))),
                               lane < n_new * DIFF_H)
    parts.append(part(_dot_nt(qs, k_new), mask_new, v_new))
    m_all = functools.reduce(jnp.maximum, [m for m, _, _ in parts])
    scale = [jnp.exp(m - m_all) for m, _, _ in parts]
    l_fin = functools.reduce(lambda a, b_: a + b_, [w * l for w, (_, l, _) in zip(scale, parts)])
    acc = functools.reduce(lambda a, b_: a + b_, [w * a for w, (_, _, a) in zip(scale, parts)])
    o = acc / jnp.maximum(l_fin, 1e-30)
    heads = [_diff_combine(o[(2 * h) * tq:(2 * h + 1) * tq], o[(2 * h + 1) * tq:(2 * h + 2) * tq],
                           lq_ref[...], sg_ref[...], lam_init) for h in range(DIFF_H)]
    o_ref[0] = jnp.concatenate(heads, axis=1).astype(o_ref.dtype)


def _dec_diff(page_table, qd, pool, k_new, v_new, lam_qk_o, sub_g, *, n_new, lam_init):
    b, n_pages = page_table.shape
    cfg = dict(n_pages=n_pages, n_new=n_new, lam_init=lam_init)
    page = lambda i: pl.BlockSpec((1, PAGE_SIZE, 2, DIFF_H, 2 * HD), lambda bi, pt: (pt[bi, i], 0, 0, 0, 0))
    per_b = lambda shape: pl.BlockSpec((1,) + shape, lambda bi, pt: (bi,) + (0,) * len(shape))
    const = lambda shape: pl.BlockSpec(shape, lambda bi, pt: (0,) * len(shape))
    width = 2 * DIFF_H * HD
    grid_spec = pltpu.PrefetchScalarGridSpec(
        num_scalar_prefetch=1, grid=(b,),
        in_specs=[page(i) for i in range(n_pages)]
                 + [per_b((DEC_ROWS, width)), per_b((DIFF_H * n_new, 2 * HD)), per_b((DIFF_H * n_new, 2 * HD)),
                    const((4, HD)), const((1, 2 * HD))],
        out_specs=per_b((DEC_ROWS, width)),
        scratch_shapes=[pltpu.VMEM((2 * DIFF_H * DEC_ROWS, LANES), f32)])
    return pl.pallas_call(
        functools.partial(_dec_diff_body, cfg), grid_spec=grid_spec,
        out_shape=jax.ShapeDtypeStruct((b, DEC_ROWS, width), bf16),
        compiler_params=_cparams("parallel"),
        name="decode_diff",
    )(page_table, *([pool] * n_pages), qd, k_new, v_new, lam_qk_o, sub_g)


def _pad_rows(a, n):
    return jnp.pad(a, ((0, 0), (0, n - a.shape[1])) + ((0, 0),) * (a.ndim - 2))


def _trunk(x, mods, qpos0, page_table, caches, wts, params):
    (b_f, w_pool, lam_qk, subln_g, norm_mix_g, norm_ffn_g, final_g) = params
    b, t, d = x.shape
    m = b * t
    decode = caches is not None
    depth = len(mods)
    tm = min(256, m)
    x2d = x.reshape(m, d)
    pos = qpos0 + np.tile(np.arange(t), b)
    tables = _rope_tables(pos)

    if decode:
        tiles_per_mod = 1
        tm_mlp, mlp_tiles_per_mod = tm, 1
        expand = lambda v: jnp.repeat(v, t, axis=0).reshape(m // tm, tm, d)
    else:
        tiles_per_mod = t // tm
        tm_mlp = min(1024, t)
        mlp_tiles_per_mod = t // tm_mlp
        expand = lambda v: v.reshape(b, 1, d)

    per_seq = lambda a, rows=DEC_ROWS: _pad_rows(a.reshape(b, t, a.shape[-1]), rows)
    outs = {}
    for li in range(depth):
        sh1, sc1, gt1, sh2, sc2, gt2 = [expand(v) for v in jnp.split(mods[li], 6, axis=-1)]
        if li % 2 == 0:
            e = li // 2
            bias = jnp.pad(b_f[e].reshape(1, FOX_H), ((0, 0), (0, LANES - FOX_H)))
            po = _proj(x2d, norm_mix_g[li], sc1, sh1, tiles_per_mod, wts["in_even"][e], tables, bias,
                       _even_plan(), _even_out_defs(not decode), tm, t)
            logf = po[O_LOGF][:, :FOX_H].reshape(b, t, FOX_H)
            if decode:
                kv_leaf = lambda oi, heads: po[oi].reshape(b, t, 2, heads, HD)
            else:
                kv_leaf = lambda oi, heads: jnp.transpose(po[oi].reshape(b, 2, heads, HD, t), (0, 4, 1, 2, 3))
            outs.setdefault("cmp", []).append(kv_leaf(O_KVC, NSA_KVH))
            outs.setdefault("slc", []).append(kv_leaf(O_KVS, NSA_KVH))
            outs.setdefault("fox", []).append(kv_leaf(O_KVF, FOX_H))
            outs.setdefault("logf", []).append(logf)
            kvw_new = kv_leaf(O_KVW, NSA_KVH)
            if not decode:
                outs.setdefault("win", []).append(kvw_new[:, t - min(WINDOW, t):])
                qn, qf = po[O_QN].reshape(b, t, NSA_H * HD), po[O_QF].reshape(b, t, FOX_H * HD)
                cmp = _pad_rows(_compress(po[O_KVC_ROWS].reshape(b, t, 4 * HD), w_pool[e]), LANES)
                ck, cv = cmp[:, :, :LANES].astype(bf16), cmp[:, :, LANES:].astype(bf16)
                o_cmp, mb = _cmp_topk(qn, ck, cv, tq=min(256, t), nb_valid=t // CMP_BLOCK)
                lane_slab = lambda oi, w: po[oi].reshape(b, t, w)
                o_s = _flash("slc", qn, lane_slab(O_KS, LANES), lane_slab(O_VS, LANES), [mb],
                             tq=min(128, t), tk=min(FLASH_TK, t), out_dtype=f32)
                o_w = _flash("win", qn, lane_slab(O_KW, LANES), lane_slab(O_VW, LANES), [],
                             tq=min(256, t), tk=min(512, t), out_dtype=f32)
                tk_f = min(FLASH_TK, t)
                f_t = _cumsum_lanes(jnp.transpose(logf, (0, 2, 1)), min(512, t))
                fk = jnp.transpose(f_t.reshape(b, FOX_H // 2, 2, t // tk_f, tk_f), (0, 1, 3, 2, 4))
                fq = f_t.reshape(b, FOX_H // 2, 2, t, 1)
                o_f = _flash("fox", qf, lane_slab(O_KF, 4 * LANES), lane_slab(O_VF, 4 * LANES), [fk, fq],
                             tq=min(512, t), tk=tk_f, out_dtype=bf16)
                acts = [a.reshape(m, a.shape[-1]) for a in (o_cmp, o_s, o_w)] + [po[O_GATE], o_f.reshape(m, -1)]
            else:
                c_cmp, c_slc, c_win, c_fox, c_logf, _ = caches
                outs.setdefault("win", []).append(jnp.concatenate([c_win[e], kvw_new], axis=1)[:, t:])
                kt_view = lambda c: jnp.transpose(c, (0, 2, 3, 4, 1))
                qn = per_seq(po[O_QN])
                o_cmp, o_s, o_w = _dec_nsa(page_table, qn, kt_view(c_cmp[e]), kt_view(c_slc[e]),
                                           kt_view(c_win[e]), per_seq(po[O_KVS]), per_seq(po[O_KVW]),
                                           w_pool[e], qpos0=qpos0, n_new=t)
                qf = po[O_QF].reshape(b, t, 1, FOX_H * HD)
                head_of_lane = np.arange(FOX_H * HD) // HD
                own = jnp.asarray(head_of_lane[None, :] == np.arange(FOX_H)[:, None], dtype=bf16)
                qbd = (qf * own[None, None]).reshape(b, t * FOX_H, FOX_H * HD)
                lf_new = jnp.pad(jnp.transpose(logf, (0, 2, 1)), ((0, 0), (0, 0), (0, LANES - t)))
                o_f = _dec_fox(page_table, qbd, kt_view(c_fox[e]), jnp.transpose(c_logf[e], (0, 2, 1)),
                               per_seq(po[O_KVF]), lf_new, n_new=t)
                acts = [a[:, :t].reshape(m, a.shape[-1]) for a in (o_cmp, o_s, o_w)]
                acts += [po[O_GATE], o_f[:, :t].reshape(m, -1)]
            x2d = _outproj(_outproj_even_body, x2d, gt1, tiles_per_mod, acts, wts["out_even"][e], tm,
                           "outproj_even")
        else:
            o = li // 2
            po = _proj(x2d, norm_mix_g[li], sc1, sh1, tiles_per_mod, wts["in_odd"][o], tables,
                       jnp.zeros((1, LANES), f32), _odd_plan(), ODD_OUT_DEFS, tm, t)
            outs.setdefault("diff", []).append(po[O_KVD].reshape(b, t, 2, DIFF_H, 2 * HD))
            lam_init = 0.8 - 0.6 * math.exp(-0.3 * li)
            sub_g = subln_g[o].reshape(1, 2 * HD)
            width = 2 * DIFF_H * HD
            if not decode:
                o_d = _flash("diff", po[O_QD].reshape(b, t, width), po[O_KD].reshape(b, t, width),
                             po[O_VD].reshape(b, t, width), [lam_qk[o], sub_g], tq=min(512, t), tk=min(FLASH_TK, t),
                             out_dtype=bf16, lam_init=lam_init)
            else:
                kvd = po[O_KVD].reshape(b, t, 2, DIFF_H, 2 * HD)
                rows_new = lambda a: a.reshape(b, t * DIFF_H, 2 * HD)
                o_d = _dec_diff(page_table, per_seq(po[O_QD]), caches[5][o], rows_new(kvd[:, :, 0]),
                                rows_new(kvd[:, :, 1]), lam_qk[o], sub_g, n_new=t, lam_init=lam_init)[:, :t]
            x2d = _outproj(_outproj_odd_body, x2d, gt1, tiles_per_mod, [o_d.reshape(m, d)],
                           wts["out_odd"][o], tm, "outproj_odd")
        x2d = _mlp(x2d, norm_ffn_g[li], sc2, sh2, gt2, mlp_tiles_per_mod, wts["ff_in"][li], wts["ff_out"][li],
                   final_g, li == depth - 1, tm_mlp, 1024)
    stack = lambda name: jnp.stack(outs[name], 0)
    return (x2d.reshape(b, t, d), stack("cmp"), stack("slc"), stack("win"), stack("fox"), stack("logf"),
            stack("diff"))


def kernel(x_prompt, x_sample, cache_nsa_cmp_kv, cache_nsa_slc_kv, cache_nsa_win_kv, cache_fox_kv,
           cache_fox_logf, cache_diff_kv, page_table, c_prompt, c_sample, w_in_even, b_f, w_pool,
           w_out_even, w_in_odd, lam_qk, subln_g, w_out_odd, norm_mix_g, norm_ffn_g, w_ada, b_ada,
           w_ff_in, w_ff_out, final_g):
    depth = w_ada.shape[0]
    nb_p, nb_s = c_prompt.shape[0], c_sample.shape[0]
    past_len = page_table.shape[1] * PAGE_SIZE

    perm_rows = lambda w: jnp.concatenate(
        [w[:NSA_H * HD].reshape(NSA_H, HD, -1)[NSA_PERM, :, :].reshape(NSA_H * HD, -1), w[NSA_H * HD:]], axis=0)
    wts = dict(
        in_even=[_even_weight(w_in_even[e]) for e in range(w_in_even.shape[0])],
        out_even=[perm_rows(w_out_even[e]).astype(bf16) for e in range(w_out_even.shape[0])],
        in_odd=[w_in_odd[o].astype(bf16) for o in range(w_in_odd.shape[0])],
        out_odd=[w_out_odd[o].astype(bf16) for o in range(w_out_odd.shape[0])],
        ff_in=[w_ff_in[l].astype(bf16) for l in range(depth)],
        ff_out=[w_ff_out[l].astype(bf16) for l in range(depth)],
    )
    rows = -(-(nb_p + nb_s) // SUBLANES) * SUBLANES
    c_all = jnp.pad(jnp.concatenate([c_prompt, c_sample], axis=0), ((0, rows - nb_p - nb_s), (0, 0)))
    mod_all = _ada(c_all, w_ada.astype(bf16), b_ada)
    mods_p = [mod_all[l, :nb_p] for l in range(depth)]
    mods_s = [mod_all[l, nb_p:nb_p + nb_s] for l in range(depth)]

    params = (b_f, w_pool, lam_qk, subln_g, norm_mix_g, norm_ffn_g, final_g)
    caches = (cache_nsa_cmp_kv, cache_nsa_slc_kv, cache_nsa_win_kv, cache_fox_kv, cache_fox_logf, cache_diff_kv)
    y_p, p_cmp, p_slc, p_win, p_fox, p_logf, p_diff = _trunk(x_prompt, mods_p, 0, None, None, wts, params)
    y_s, s_cmp, s_slc, s_win, s_fox, s_logf, s_diff = _trunk(x_sample, mods_s, past_len, page_table, caches,
                                                             wts, params)
    return (y_p, y_s, p_cmp, s_cmp, p_slc, s_slc, p_win, s_win, p_fox, s_fox, p_logf, s_logf, p_diff, s_diff)
```

```python
import functools
import math

import numpy as np
import jax
import jax.numpy as jnp
from jax import lax
from jax.experimental import pallas as pl
from jax.experimental.pallas import tpu as pltpu

HD = 64
ROPE_DIM = HD // 4
ROPE_HALF = ROPE_DIM // 2
ROPE_THETA = 500000.0
NSA_H = 8
NSA_KVH = 2
CMP_BLOCK = 64
CMP_SHIFT = 6
N_SEL = 16
WINDOW = 512
FOX_H = 8
DIFF_H = 8
EPS = 1e-6
PAGE_SIZE = 128

LANES = 128
SUBLANES = 8
VMEM_LIMIT_BYTES = 48 * 1024 * 1024

NEG_MASK = -1e30
NEG_BLOCK = -1e9
PAGES_PER_STEP = 4
FLASH_TK = 1024

NSA_PERM = (0, 4, 1, 5, 2, 6, 3, 7)

f32 = jnp.float32
bf16 = jnp.bfloat16


def _cparams(*sem):
    return pltpu.CompilerParams(dimension_semantics=sem, vmem_limit_bytes=VMEM_LIMIT_BYTES)


def _dot(a, b):
    return jnp.dot(a, b, preferred_element_type=f32)


def _dot_nt(a, b):
    return lax.dot_general(a, b, (((1,), (1,)), ((), ())), preferred_element_type=f32)


def _rms_mod(x, g, sc, sh):
    y = x * lax.rsqrt(jnp.mean(x * x, axis=-1, keepdims=True) + EPS)
    return (y * g) * (1.0 + sc) + sh


def _split_dot(x, w, parts):
    acc = None
    for _ in range(parts):
        piece = x.astype(bf16)
        term = _dot(piece, w)
        acc = term if acc is None else acc + term
        x = x - piece.astype(f32)
    return acc


def _split_dot_rhs(w, x, parts):
    acc = None
    for _ in range(parts):
        piece = x.astype(bf16)
        term = _dot(w, piece)
        acc = term if acc is None else acc + term
        x = x - piece.astype(f32)
    return acc


def _ada_body(c_ref, w_ref, b_ref, o_ref):
    c = c_ref[...]
    a = (c * jax.nn.sigmoid(c)).astype(bf16)
    o_ref[0] = _dot(a, w_ref[0].astype(bf16)) + b_ref[0]


def _ada(c_all, w_ada_b, b_ada):
    depth, d, n = w_ada_b.shape
    rows = c_all.shape[0]
    tn = n // 4
    return pl.pallas_call(
        _ada_body,
        out_shape=jax.ShapeDtypeStruct((depth, rows, n), f32),
        grid=(depth, n // tn),
        in_specs=[pl.BlockSpec((rows, d), lambda l, j: (0, 0)),
                  pl.BlockSpec((1, d, tn), lambda l, j: (l, 0, j)),
                  pl.BlockSpec((1, 1, tn), lambda l, j: (l, 0, j))],
        out_specs=pl.BlockSpec((1, rows, tn), lambda l, j: (l, 0, j)),
        compiler_params=_cparams("parallel", "parallel"),
        name="ada_mod",
    )(c_all, w_ada_b, b_ada.reshape(depth, 1, n))


def _rope_group(a, cos, sa, sb):
    return a * cos + pltpu.roll(a, LANES - ROPE_HALF, 1) * sa + pltpu.roll(a, ROPE_HALF, 1) * sb


def _proj_body(plan, x_ref, g_ref, sc_ref, sh_ref, w_ref, cos_ref, sa_ref, sb_ref, bias_ref, *out_refs):
    h = _rms_mod(x_ref[...], g_ref[...], sc_ref[0], sh_ref[0]).astype(bf16)
    cos, sa, sb = cos_ref[...], sa_ref[...], sb_ref[...]
    for src, groups in plan:
        width = LANES * len(groups)
        acc = _dot(h, w_ref[:, src:src + width])
        for gi, (kind, scale, dests) in enumerate(groups):
            a = acc[:, gi * LANES:(gi + 1) * LANES]
            if kind == "rope":
                a = _rope_group(a, cos, sa, sb)
            elif kind == "sigmoid":
                a = jax.nn.sigmoid(a)
            elif kind == "logsig":
                z = a + bias_ref[...]
                a = jnp.minimum(z, 0.0) - jnp.log1p(jnp.exp(-jnp.abs(z)))
            if scale != 1.0:
                a = a * scale
            for oi, col in dests:
                if len(out_refs[oi].shape) == 3:
                    out_refs[oi][0, col:col + LANES, :] = a.T.astype(out_refs[oi].dtype)
                else:
                    out_refs[oi][:, col:col + LANES] = a.astype(out_refs[oi].dtype)


def _proj(x2d, g, sc, sh, tiles_per_mod, w_b, tables, bias, plan, out_defs, tm, seq_len):
    m, d = x2d.shape
    n = w_b.shape[1]
    cos, sa, sb = tables
    mod_rows = sc.shape[1]
    row = lambda i: (i, 0)
    const = lambda i: (0, 0)
    mod = lambda i: (i // tiles_per_mod, 0, 0)
    tiles_per_seq = seq_len // tm
    out_shape, out_specs = [], []
    for w, dt, position_minor in out_defs:
        if position_minor:
            out_shape.append(jax.ShapeDtypeStruct((m // seq_len, w, seq_len), dt))
            out_specs.append(pl.BlockSpec((1, w, tm), lambda i: (i // tiles_per_seq, 0, i % tiles_per_seq)))
        else:
            out_shape.append(jax.ShapeDtypeStruct((m, w), dt))
            out_specs.append(pl.BlockSpec((tm, w), row))
    return pl.pallas_call(
        functools.partial(_proj_body, plan),
        out_shape=out_shape,
        grid=(m // tm,),
        in_specs=[pl.BlockSpec((tm, d), row),
                  pl.BlockSpec((1, d), const),
                  pl.BlockSpec((1, mod_rows, d), mod),
                  pl.BlockSpec((1, mod_rows, d), mod),
                  pl.BlockSpec((d, n), const, pipeline_mode=pl.Buffered(1)),
                  pl.BlockSpec((tm, LANES), row),
                  pl.BlockSpec((tm, LANES), row),
                  pl.BlockSpec((tm, LANES), row),
                  pl.BlockSpec((1, LANES), const)],
        out_specs=out_specs,
        compiler_params=_cparams("parallel"),
        name="norm_mod_proj",
    )(x2d, g.reshape(1, d), sc, sh, w_b, cos, sa, sb, bias)


def _rope_tables(pos):
    inv = np.float32(ROPE_THETA) ** (-np.arange(ROPE_HALF, dtype=np.float32) / np.float32(ROPE_HALF))
    ang = (pos.astype(np.float32)[:, None] * inv[None, :]).astype(np.float64)
    n = pos.shape[0]
    cos = np.ones((n, HD), np.float64)
    sa = np.zeros((n, HD), np.float64)
    sb = np.zeros((n, HD), np.float64)
    cos[:, :ROPE_HALF] = np.cos(ang)
    cos[:, ROPE_HALF:ROPE_DIM] = np.cos(ang)
    sa[:, :ROPE_HALF] = -np.sin(ang)
    sb[:, ROPE_HALF:ROPE_DIM] = np.sin(ang)
    tile = lambda t: jnp.asarray(np.tile(t, (1, LANES // HD)).astype(np.float32))
    return tile(cos), tile(sa), tile(sb)


EV_QN, EV_KVC, EV_KVS, EV_KVW, EV_QF, EV_KVF, EV_GATE, EV_FF = 0, 512, 768, 1024, 1280, 1792, 2816, 4352


def _even_weight(w):
    d = w.shape[0]
    sizes = [NSA_H * HD, 2 * NSA_KVH * HD, 2 * NSA_KVH * HD, 2 * NSA_KVH * HD, 3 * NSA_H, FOX_H * HD,
             2 * FOX_H * HD, FOX_H]
    offs = np.concatenate([[0], np.cumsum(sizes)])
    q_n, kv_c, kv_s, kv_w, gates, q_f, kv_f, f_f = [w[:, offs[i]:offs[i + 1]] for i in range(8)]
    q_n = q_n.reshape(d, NSA_H, HD)[:, NSA_PERM, :].reshape(d, NSA_H * HD)
    gates = gates.reshape(d, NSA_H, 3)[:, NSA_PERM, :]
    gates = jnp.repeat(jnp.transpose(gates, (0, 2, 1))[..., None], HD, axis=-1).reshape(d, 3 * NSA_H * HD)
    f_f = jnp.pad(f_f, ((0, 0), (0, LANES - FOX_H)))
    return jnp.concatenate([q_n, kv_c, kv_s, kv_w, q_f, kv_f, gates, f_f], axis=1).astype(bf16)


(O_QN, O_KVC, O_KVS, O_KS, O_VS, O_KVW, O_KW, O_VW, O_QF, O_KVF, O_KF, O_VF, O_GATE, O_LOGF, O_KVC_ROWS) = range(15)


def _even_out_defs(position_minor):
    pm = position_minor
    return [(512, bf16, False), (256, f32, pm), (256, f32, pm), (128, bf16, False), (128, bf16, False),
            (256, f32, pm), (128, bf16, False), (128, bf16, False), (512, bf16, False), (1024, f32, pm),
            (512, bf16, False), (512, bf16, False), (1536, f32, False), (128, f32, False), (256, f32, False)]


def _even_plan():
    scale = HD ** -0.5
    plan = [(EV_QN, [("rope", scale, [(O_QN, LANES * i)]) for i in range(4)]),
            (EV_KVC, [("rope", 1.0, [(O_KVC, 0), (O_KVC_ROWS, 0)]),
                      ("plain", 1.0, [(O_KVC, LANES), (O_KVC_ROWS, LANES)])]),
            (EV_KVS, [("rope", 1.0, [(O_KVS, 0), (O_KS, 0)]), ("plain", 1.0, [(O_KVS, LANES), (O_VS, 0)])]),
            (EV_KVW, [("rope", 1.0, [(O_KVW, 0), (O_KW, 0)]), ("plain", 1.0, [(O_KVW, LANES), (O_VW, 0)])]),
            (EV_QF, [("plain", scale, [(O_QF, LANES * i)]) for i in range(4)]),
            (EV_KVF, [("plain", 1.0, [(O_KVF, LANES * i), (O_KF, LANES * i)]) for i in range(4)]),
            (EV_KVF + 512, [("plain", 1.0, [(O_KVF, 512 + LANES * i), (O_VF, LANES * i)]) for i in range(4)])]
    for j in range(3):
        plan.append((EV_GATE + 512 * j, [("sigmoid", 1.0, [(O_GATE, 512 * j + LANES * i)]) for i in range(4)]))
    plan.append((EV_FF, [("logsig", 1.0, [(O_LOGF, 0)])]))
    return plan


(O_QD, O_KVD, O_KD, O_VD) = range(4)
ODD_OUT_DEFS = [(1024, bf16, False), (2048, f32, False), (1024, bf16, False), (1024, bf16, False)]


def _odd_plan():
    scale = HD ** -0.5
    plan = []
    for c in range(2):
        plan.append((512 * c, [("rope", scale, [(O_QD, 512 * c + LANES * i)]) for i in range(4)]))
    for c in range(2):
        plan.append((1024 + 512 * c,
                     [("rope", 1.0, [(O_KVD, 512 * c + LANES * i), (O_KD, 512 * c + LANES * i)]) for i in range(4)]))
    for c in range(2):
        plan.append((2048 + 512 * c,
                     [("plain", 1.0, [(O_KVD, 1024 + 512 * c + LANES * i), (O_VD, 512 * c + LANES * i)])
                      for i in range(4)]))
    return plan


def _outproj_even_body(x_ref, gt_ref, oc_ref, os_ref, ow_ref, gates_ref, of_ref, w_ref, o_ref):
    nw = NSA_H * HD
    o_nsa = (gates_ref[:, 0:nw] * oc_ref[...] + gates_ref[:, nw:2 * nw] * os_ref[...]
             + gates_ref[:, 2 * nw:3 * nw] * ow_ref[...])
    a = jnp.concatenate([o_nsa.astype(bf16), of_ref[...]], axis=1)
    o_ref[...] = x_ref[...] + gt_ref[0] * _dot(a, w_ref[...])


def _outproj_odd_body(x_ref, gt_ref, od_ref, w_ref, o_ref):
    o_ref[...] = x_ref[...] + gt_ref[0] * _dot(od_ref[...], w_ref[...])


def _outproj(body, x2d, gt, tiles_per_mod, acts, w_b, tm, name):
    m, d = x2d.shape
    row = lambda i: (i, 0)
    mod = lambda i: (i // tiles_per_mod, 0, 0)
    return pl.pallas_call(
        body,
        out_shape=jax.ShapeDtypeStruct((m, d), f32),
        grid=(m // tm,),
        in_specs=[pl.BlockSpec((tm, d), row), pl.BlockSpec((1, gt.shape[1], d), mod)]
                 + [pl.BlockSpec((tm, a.shape[1]), row) for a in acts]
                 + [pl.BlockSpec(w_b.shape, lambda i: (0, 0), pipeline_mode=pl.Buffered(1))],
        out_specs=pl.BlockSpec((tm, d), row),
        compiler_params=_cparams("parallel"),
        name=name,
    )(x2d, gt, *acts, w_b)


MLP_SLICE = 512


def _mlp_body(final, x_ref, g_ref, sc_ref, sh_ref, gt_ref, w1_ref, w2_ref, fg_ref, o_ref):
    x = x_ref[...]
    h = _rms_mod(x, g_ref[...], sc_ref[0], sh_ref[0]).astype(bf16)
    ff = None
    for lo in range(0, w1_ref.shape[1], MLP_SLICE):
        a = jnp.maximum(_dot(h, w1_ref[:, lo:lo + MLP_SLICE]), 0.0)
        part = _dot((a * a).astype(bf16), w2_ref[lo:lo + MLP_SLICE, :])
        ff = part if ff is None else ff + part
    y = x + gt_ref[0] * ff
    if final:
        y = y * lax.rsqrt(jnp.mean(y * y, axis=-1, keepdims=True) + EPS) * fg_ref[...]
    o_ref[...] = y


def _mlp(x2d, g, sc, sh, gt, tiles_per_mod, w1_b, w2_b, final_g, final, tm):
    m, d = x2d.shape
    dff = w1_b.shape[1]
    mod_rows = sc.shape[1]
    row = lambda i: (i, 0)
    const = lambda i: (0, 0)
    mod = lambda i: (i // tiles_per_mod, 0, 0)
    return pl.pallas_call(
        functools.partial(_mlp_body, final),
        out_shape=jax.ShapeDtypeStruct((m, d), f32),
        grid=(m // tm,),
        in_specs=[pl.BlockSpec((tm, d), row), pl.BlockSpec((1, d), const),
                  pl.BlockSpec((1, mod_rows, d), mod), pl.BlockSpec((1, mod_rows, d), mod),
                  pl.BlockSpec((1, mod_rows, d), mod),
                  pl.BlockSpec((d, dff), const, pipeline_mode=pl.Buffered(1)),
                  pl.BlockSpec((dff, d), const, pipeline_mode=pl.Buffered(1)),
                  pl.BlockSpec((1, d), const)],
        out_specs=pl.BlockSpec((tm, d), row),
        compiler_params=_cparams("parallel"),
        name="relu2_mlp",
    )(x2d, g.reshape(1, d), sc, sh, gt, w1_b, w2_b, final_g.reshape(1, d))


def _compress_body(x_ref, wp_ref, o_ref):
    wp = wp_ref[...]
    rows = []
    for blk in range(x_ref.shape[1] // CMP_BLOCK):
        rows.append(jnp.sum(x_ref[0, blk * CMP_BLOCK:(blk + 1) * CMP_BLOCK, :] * wp, axis=0, keepdims=True))
    o_ref[0] = jnp.concatenate(rows, axis=0)


def _compress(kv, w_pool_e):
    b, t, c = kv.shape
    wp = jnp.repeat(jnp.transpose(w_pool_e, (1, 0, 2))[..., None], HD, axis=-1).reshape(CMP_BLOCK, c)
    rows_in = PAGES_PER_STEP * PAGE_SIZE
    rows_out = rows_in // CMP_BLOCK
    return pl.pallas_call(
        _compress_body,
        out_shape=jax.ShapeDtypeStruct((b, t // CMP_BLOCK, c), f32),
        grid=(b, t // rows_in),
        in_specs=[pl.BlockSpec((1, rows_in, c), lambda bi, g: (bi, g, 0)),
                  pl.BlockSpec((CMP_BLOCK, c), lambda bi, g: (0, 0))],
        out_specs=pl.BlockSpec((1, rows_out, c), lambda bi, g: (bi, g, 0)),
        compiler_params=_cparams("parallel", "parallel"),
        name="nsa_compress",
    )(kv, wp)


def _cumsum_body(x_ref, u_ref, o_ref, carry_ref):
    @pl.when(pl.program_id(1) == 0)
    def _():
        carry_ref[...] = jnp.zeros_like(carry_ref)

    cs = _split_dot(x_ref[0], u_ref[...], 3) + carry_ref[...]
    o_ref[0] = cs
    carry_ref[...] = cs[:, cs.shape[1] - 1:]


def _upper_ones(n):
    return jnp.asarray(np.triu(np.ones((n, n), np.float32)), dtype=bf16)


def _cumsum_lanes(x_t, tc):
    b, h, l = x_t.shape
    return pl.pallas_call(
        _cumsum_body,
        out_shape=jax.ShapeDtypeStruct((b, h, l), f32),
        grid=(b, l // tc),
        in_specs=[pl.BlockSpec((1, h, tc), lambda bi, c: (bi, 0, c)),
                  pl.BlockSpec((tc, tc), lambda bi, c: (0, 0))],
        out_specs=pl.BlockSpec((1, h, tc), lambda bi, c: (bi, 0, c)),
        scratch_shapes=[pltpu.VMEM((h, 1), f32)],
        compiler_params=_cparams("parallel", "arbitrary"),
        name="logf_cumsum",
    )(x_t, _upper_ones(tc))


def _stack_heads(q, qs_ref, n_slab, tq):
    lane = lax.broadcasted_iota(jnp.int32, (tq, LANES), 1)
    low = lane < HD
    for s in range(n_slab):
        slab = q[:, s * LANES:(s + 1) * LANES].astype(f32)
        qs_ref[(2 * s) * tq:(2 * s + 1) * tq, 0:LANES] = jnp.where(low, slab, 0.0).astype(qs_ref.dtype)
        qs_ref[(2 * s + 1) * tq:(2 * s + 2) * tq, 0:LANES] = jnp.where(low, 0.0, slab).astype(qs_ref.dtype)


def _unstack_heads(o, n_slab, tq):
    lane = lax.broadcasted_iota(jnp.int32, (tq, LANES), 1)
    low = lane < HD
    slabs = [jnp.where(low, o[(2 * s) * tq:(2 * s + 1) * tq], o[(2 * s + 1) * tq:(2 * s + 2) * tq])
             for s in range(n_slab)]
    return slabs[0] if n_slab == 1 else jnp.concatenate(slabs, axis=1)


def _masked_softmax(s, valid):
    sm = jnp.where(valid, s, NEG_MASK)
    m = jnp.max(sm, axis=1, keepdims=True)
    p = jnp.where(valid, jnp.exp(sm - m), 0.0)
    return p / jnp.maximum(jnp.sum(p, axis=1, keepdims=True), 1e-30)


def _select_blocks(p, tg, tq, nb_valid):
    n = lax.broadcasted_iota(jnp.int32, (1, LANES), 1)
    cur = jnp.right_shift(tg, CMP_SHIFT)
    n_cand = jnp.minimum(cur, nb_valid)
    out = []
    for g in range(NSA_KVH):
        imp = p[g * tq:(g + 1) * tq]
        for s_i in range(1, NSA_H // 2):
            imp = imp + p[(2 * s_i + g) * tq:(2 * s_i + g + 1) * tq]
        work = jnp.where(n < n_cand, imp, -1.0)
        sel = n == cur
        for it in range(N_SEL - 1):
            hit = n == jnp.argmax(work, axis=1, keepdims=True).astype(jnp.int32)
            sel = jnp.logical_or(sel, jnp.logical_and(hit, it < n_cand))
            work = jnp.where(hit, -2.0, work)
        out.append(jnp.where(sel, 0.0, NEG_BLOCK))
    return out


def _select_blocks_ranked(p, tg, tq, nb_valid):
    n = lax.broadcasted_iota(jnp.int32, (1, LANES), 1)
    n_col = lax.broadcasted_iota(jnp.int32, (LANES, 1), 0)
    cur = jnp.right_shift(tg, CMP_SHIFT)
    n_cand = jnp.minimum(cur, nb_valid)
    n_pick = jnp.minimum(n_cand, N_SEL - 1).astype(f32)
    out = []
    for g in range(NSA_KVH):
        imp = p[g * tq:(g + 1) * tq]
        for s_i in range(1, NSA_H // 2):
            imp = imp + p[(2 * s_i + g) * tq:(2 * s_i + g + 1) * tq]
        work = jnp.where(n < n_cand, imp, -1.0)
        work_t = jnp.concatenate([work, jnp.zeros((LANES - tq, LANES), f32)], axis=0).T
        ranks = []
        for r in range(tq):
            other = work_t[:, r:r + 1]
            mine = work[r:r + 1, :]
            ahead = jnp.logical_or(other > mine, jnp.logical_and(other == mine, n_col < n))
            ranks.append(jnp.sum(jnp.where(ahead, 1.0, 0.0), axis=0, keepdims=True))
        rank = jnp.concatenate(ranks, axis=0)
        sel = jnp.logical_or(n == cur, rank < n_pick)
        out.append(jnp.where(sel, 0.0, NEG_BLOCK))
    return out


def _flash_body(cfg, *refs):
    mode, n_slab, tq, tk, n_chunks = cfg["mode"], cfg["n_slab"], cfg["tq"], cfg["tk"], cfg["n_chunks"]
    window = cfg["window"]
    rows = 2 * n_slab * tq
    q_ref, k_ref, v_ref = refs[:3]
    extra = refs[3:-5]
    o_ref, qs_ref, s_ref, m_ref, acc_ref = refs[-5:]
    qi = pl.program_id(2)

    _stack_heads(q_ref[0], qs_ref, n_slab, tq)
    if mode == "slc":
        mb_ref = extra[0]
        for s in range(n_slab):
            for g in range(2):
                qs_ref[(2 * s + g) * tq:(2 * s + g + 1) * tq, LANES:2 * LANES] = mb_ref[0, g]
    m_ref[...] = jnp.full_like(m_ref, NEG_MASK)
    acc_ref[...] = jnp.zeros_like(acc_ref)

    q_lo = qi * tq
    qpos = q_lo + (lax.broadcasted_iota(jnp.int32, (rows, 1), 0) & (tq - 1))

    def scores(c):
        ks = pl.multiple_of(c * tk, tk)
        k = k_ref[0, pl.ds(ks, tk), :]
        if mode == "slc":
            blk = jnp.right_shift(ks + lax.broadcasted_iota(jnp.int32, (tk, LANES), 0), CMP_SHIFT)
            hot = jnp.where(blk == lax.broadcasted_iota(jnp.int32, (tk, LANES), 1), 1.0, 0.0).astype(bf16)
            k = jnp.concatenate([k, hot], axis=1)
        return _dot_nt(qs_ref[...], k)

    def update(slot, c, masked, width=tk):
        ks = pl.multiple_of(c * tk, tk)
        s = s_ref[slot, :, 0:width]
        row_bias = 0.0
        if mode == "fox":
            fk_ref, fq_ref = extra
            fk = fk_ref[0, 0, c][:, 0:width]
            s = jnp.concatenate([s[0:tq] - fk[0:1], s[tq:2 * tq] - fk[1:2]], axis=0)
            row_bias = jnp.concatenate([fq_ref[0, 0, 0], fq_ref[0, 0, 1]], axis=0)
        if masked:
            kpos = ks + lax.broadcasted_iota(jnp.int32, (1, width), 1)
            mask = kpos <= qpos if masked == "causal" else kpos > qpos - window
            s = jnp.where(mask, s, NEG_MASK)
        m_old = m_ref[...]
        m_new = jnp.maximum(m_old, jnp.max(s, axis=1, keepdims=True) + row_bias)
        alpha = jnp.exp(m_old - m_new)
        p = jnp.exp((s + (row_bias - m_new)).astype(bf16))
        v_ones = jnp.concatenate([v_ref[0, pl.ds(ks, width), :], jnp.ones((width, LANES), bf16)], axis=1)
        acc_ref[...] = alpha * acc_ref[...] + _dot(p, v_ones)
        m_ref[...] = m_new

    def issue(c, slot):
        s_ref[slot] = scores(c)

    if window is not None:
        c_last = q_lo // tk

        @pl.when(c_last > 0)
        def _():
            issue(c_last - 1, 0)
            issue(c_last, 1)
            update(0, c_last - 1, "window")
            update(1, c_last, "causal")

        @pl.when(c_last == 0)
        def _():
            issue(0, 1)
            update(1, 0, "causal")
    else:
        n_full = (q_lo + 1) // tk
        visible = q_lo + tq - n_full * tk
        widths = [w for w in (tk // 4, tk // 2, tk) if w >= max(tq, 2 * LANES)]

        def diagonal(slot):
            for idx, w in enumerate(widths):
                lower = widths[idx - 1] if idx else 0

                @pl.when(jnp.logical_and(visible > lower, visible <= w))
                def _():
                    update(slot, n_full, "causal", w)

        issue(0, 0)

        def pair(j, carry):
            c = 2 * j
            issue(c + 1, 1)
            update(0, c, False)
            issue(c + 2, 0)
            update(1, c + 1, False)
            return carry

        lax.fori_loop(0, n_full // 2, pair, 0)

        @pl.when((n_full & 1) == 1)
        def _():
            issue(n_full, 1)
            update(0, n_full - 1, False)
            diagonal(1)

        @pl.when((n_full & 1) == 0)
        def _():
            diagonal(0)

    acc = acc_ref[...]
    o = acc[:, :LANES] / jnp.maximum(acc[:, LANES:], 1e-30)
    if mode == "diff":
        lq_ref, sg_ref = extra
        o_ref[0] = _diff_combine(o[0:tq], o[tq:2 * tq], lq_ref[...], sg_ref[...], cfg["lam_init"]).astype(o_ref.dtype)
    else:
        o_ref[0] = _unstack_heads(o, n_slab, tq).astype(o_ref.dtype)


def _diff_combine(o1, o2, lp, sub_g, lam_init):
    lam = (jnp.exp(jnp.sum(lp[0:1] * lp[1:2], axis=1, keepdims=True))
           - jnp.exp(jnp.sum(lp[2:3] * lp[3:4], axis=1, keepdims=True)) + lam_init)
    d = o1 - lam * o2
    d = d * lax.rsqrt(jnp.mean(d * d, axis=-1, keepdims=True) + EPS) * sub_g
    return d * (1.0 - lam_init)


def _flash(mode, q, k, v, extra, *, tq, tk, out_dtype, lam_init=0.0):
    b, t_q, cq = q.shape
    t_k = k.shape[1]
    n_slab = 4 if mode in ("slc", "win") else 1
    n_grid_slab = cq // (LANES * n_slab)
    n_chunks = t_k // tk
    rows = 2 * n_slab * tq
    ck = 2 * LANES if mode == "slc" else LANES
    assert tk % tq == 0
    if mode == "win":
        assert tk == WINDOW
    score_slots = (2, rows, tk)
    cfg = dict(mode=mode, n_slab=n_slab, tq=tq, tk=tk, n_chunks=n_chunks,
               window=WINDOW if mode == "win" else None, lam_init=lam_init)
    qw = LANES * n_slab
    in_specs = [pl.BlockSpec((1, tq, qw), lambda bi, s, i: (bi, i, s)),
                pl.BlockSpec((1, t_k, LANES), lambda bi, s, i: (bi, 0, s)),
                pl.BlockSpec((1, t_k, LANES), lambda bi, s, i: (bi, 0, s))]
    if mode == "fox":
        in_specs += [pl.BlockSpec((1, 1, n_chunks, 2, tk), lambda bi, s, i: (bi, s, 0, 0, 0)),
                     pl.BlockSpec((1, 1, 2, tq, 1), lambda bi, s, i: (bi, s, 0, i, 0))]
    elif mode == "slc":
        in_specs += [pl.BlockSpec((1, 2, tq, LANES), lambda bi, s, i: (bi, 0, i, 0))]
    elif mode == "diff":
        in_specs += [pl.BlockSpec((4, HD), lambda bi, s, i: (0, 0)),
                     pl.BlockSpec((1, LANES), lambda bi, s, i: (0, 0))]
    return pl.pallas_call(
        functools.partial(_flash_body, cfg),
        out_shape=jax.ShapeDtypeStruct((b, t_q, cq), out_dtype),
        grid=(b, n_grid_slab, t_q // tq),
        in_specs=in_specs,
        out_specs=pl.BlockSpec((1, tq, qw), lambda bi, s, i: (bi, i, s)),
        scratch_shapes=[pltpu.VMEM((rows, ck), bf16), pltpu.VMEM(score_slots, f32), pltpu.VMEM((rows, 1), f32),
                        pltpu.VMEM((rows, 2 * LANES), f32)],
        compiler_params=_cparams("parallel", "parallel", "parallel"),
        name="flash_" + mode,
    )(q, k, v, *extra)


def _cmp_body(tq, nb_valid, q_ref, ck_ref, cv_ref, o_ref, mb_ref, qs_ref):
    n_slab = NSA_H // 2
    rows = 2 * n_slab * tq
    qi = pl.program_id(1)
    _stack_heads(q_ref[0], qs_ref, n_slab, tq)
    s = _dot_nt(qs_ref[...], ck_ref[0])
    t = qi * tq + (lax.broadcasted_iota(jnp.int32, (rows, 1), 0) & (tq - 1))
    n = lax.broadcasted_iota(jnp.int32, (1, LANES), 1)
    valid = jnp.logical_and((n + 1) * CMP_BLOCK - 1 <= t, n < nb_valid)
    p = _masked_softmax(s, valid)
    o_ref[0] = _unstack_heads(_dot(p.astype(bf16), cv_ref[0]), n_slab, tq)
    bias = _select_blocks(p, t[0:tq], tq, nb_valid)
    for g in range(NSA_KVH):
        mb_ref[0, g] = bias[g].astype(bf16)


def _cmp_topk(q, ck, cv, *, tq, nb_valid):
    b, t_q, cq = q.shape
    rows = NSA_H * tq
    return pl.pallas_call(
        functools.partial(_cmp_body, tq, nb_valid),
        out_shape=[jax.ShapeDtypeStruct((b, t_q, cq), f32), jax.ShapeDtypeStruct((b, NSA_KVH, t_q, LANES), bf16)],
        grid=(b, t_q // tq),
        in_specs=[pl.BlockSpec((1, tq, cq), lambda bi, i: (bi, i, 0)),
                  pl.BlockSpec((1, LANES, LANES), lambda bi, i: (bi, 0, 0)),
                  pl.BlockSpec((1, LANES, LANES), lambda bi, i: (bi, 0, 0))],
        out_specs=[pl.BlockSpec((1, tq, cq), lambda bi, i: (bi, i, 0)),
                   pl.BlockSpec((1, NSA_KVH, tq, LANES), lambda bi, i: (bi, 0, i, 0))],
        scratch_shapes=[pltpu.VMEM((rows, LANES), bf16)],
        compiler_params=_cparams("parallel", "parallel"),
        name="nsa_cmp_topk",
    )(q, ck, cv)


DEC_ROWS = SUBLANES
DEC_SEQS = 2


def _pad_keys(new, lo, width):
    x = new[:, lo:lo + width]
    return jnp.concatenate([x, jnp.zeros((LANES - x.shape[0], width), f32)], axis=0).astype(bf16)


def _new_key_mask(rows, n_new):
    qi = lax.broadcasted_iota(jnp.int32, (rows, 1), 0) & (DEC_ROWS - 1)
    kj = lax.broadcasted_iota(jnp.int32, (1, LANES), 1)
    return jnp.logical_and(kj <= qi, kj < n_new)


def _softmax2(s1, s2, mask2):
    s2 = jnp.where(mask2, s2, NEG_MASK)
    m = jnp.maximum(jnp.max(s1, axis=1, keepdims=True), jnp.max(s2, axis=1, keepdims=True))
    p1 = jnp.exp(s1 - m)
    p2 = jnp.where(mask2, jnp.exp(s2 - m), 0.0)
    den = jnp.sum(p1, axis=1, keepdims=True) + jnp.sum(p2, axis=1, keepdims=True)
    return p1, p2, jnp.maximum(den, 1e-30)


def _dec_nsa_body(cfg, pt_ref, *refs):
    n_pages, n_seq = cfg["n_pages"], cfg["n_seq"]
    (q_ref, win_ref, news_ref, neww_ref, wp_ref, e_ref, oht_ref,
     oc_ref, os_ref, ow_ref, qs_ref) = refs[2 * n_seq * n_pages:]
    for j in range(n_seq):
        _dec_nsa_one(cfg, j, refs[j * n_pages:(j + 1) * n_pages],
                     refs[(n_seq + j) * n_pages:(n_seq + j + 1) * n_pages], q_ref, win_ref, news_ref, neww_ref,
                     wp_ref, e_ref, oht_ref, oc_ref, os_ref, ow_ref, qs_ref.at[j])


def _dec_nsa_one(cfg, j, cmp_pages, slc_pages, q_ref, win_ref, news_ref, neww_ref, wp_ref, e_ref, oht_ref,
                 oc_ref, os_ref, ow_ref, qs_ref):
    n_pages, qpos0, n_new, w_buf = cfg["n_pages"], cfg["qpos0"], cfg["n_new"], cfg["w_buf"]
    tq, n_slab = DEC_ROWS, NSA_H // 2
    rows = NSA_H * tq
    gd = NSA_KVH * HD

    _stack_heads(q_ref[j], qs_ref, n_slab, tq)
    qs = qs_ref[...].astype(bf16)
    t = qpos0 + (lax.broadcasted_iota(jnp.int32, (rows, 1), 0) & (tq - 1))
    n = lax.broadcasted_iota(jnp.int32, (1, LANES), 1)
    mask_new = _new_key_mask(rows, n_new)

    def compressed(kv):
        xt = jnp.concatenate([p[0, kv].reshape(gd, PAGE_SIZE) for p in cmp_pages], axis=1)
        w = jnp.concatenate([wp_ref[kv]] * n_pages, axis=1)
        return _split_dot(xt * w, e_ref[...], 2).astype(bf16)

    ck_t, cv_t = compressed(0), compressed(1)
    nb_valid = n_pages * PAGE_SIZE // CMP_BLOCK
    valid = jnp.logical_and((n + 1) * CMP_BLOCK - 1 <= t, n < nb_valid)
    p = _masked_softmax(_dot(qs, ck_t), valid)
    oc_ref[j] = _unstack_heads(_dot_nt(p.astype(bf16), cv_t), n_slab, tq)

    bias = _select_blocks_ranked(p, t[0:tq], tq, nb_valid)
    qq = jnp.concatenate([qs, jnp.concatenate(bias * n_slab, axis=0).astype(bf16)], axis=1)
    ks_t = jnp.concatenate([p_[0, 0].reshape(gd, PAGE_SIZE) for p_ in slc_pages], axis=1).astype(bf16)
    vs_t = jnp.concatenate([p_[0, 1].reshape(gd, PAGE_SIZE) for p_ in slc_pages], axis=1).astype(bf16)
    s1 = _dot(qq, jnp.concatenate([ks_t, oht_ref[...]], axis=0))
    new_s = news_ref[j]
    s2 = _dot_nt(qs, _pad_keys(new_s, 0, gd))
    p1, p2, den = _softmax2(s1, s2, mask_new)
    o = _dot_nt(p1.astype(bf16), vs_t) + _dot(p2.astype(bf16), _pad_keys(new_s, gd, gd))
    os_ref[j] = _unstack_heads(o / den, n_slab, tq)

    kw_t = win_ref[j, 0].reshape(gd, w_buf).astype(bf16)
    vw_t = win_ref[j, 1].reshape(gd, w_buf).astype(bf16)
    kpos = qpos0 - w_buf + lax.broadcasted_iota(jnp.int32, (1, w_buf), 1)
    s1 = jnp.where((t - kpos) < WINDOW, _dot(qs, kw_t), NEG_MASK)
    new_w = neww_ref[j]
    s2 = _dot_nt(qs, _pad_keys(new_w, 0, gd))
    p1, p2, den = _softmax2(s1, s2, mask_new)
    o = _dot_nt(p1.astype(bf16), vw_t) + _dot(p2.astype(bf16), _pad_keys(new_w, gd, gd))
    ow_ref[j] = _unstack_heads(o / den, n_slab, tq)


def _dec_nsa(page_table, qn, cmp_pool, slc_pool, win_t, new_s, new_w, w_pool_e, *, qpos0, n_new):
    b, n_pages = page_table.shape
    w_buf = win_t.shape[-1]
    l_k = n_pages * PAGE_SIZE
    gd = NSA_KVH * HD
    n_seq = DEC_SEQS if b % DEC_SEQS == 0 else 1
    cfg = dict(n_pages=n_pages, n_seq=n_seq, qpos0=qpos0, n_new=n_new, w_buf=w_buf)
    wp = jnp.tile(jnp.repeat(jnp.transpose(w_pool_e, (0, 2, 1)), HD, axis=1), (1, 1, PAGE_SIZE // CMP_BLOCK))
    blk_of = np.arange(l_k) // CMP_BLOCK
    e_all = jnp.asarray(blk_of[:, None] == np.arange(LANES)[None, :], dtype=bf16)
    oh_t = jnp.asarray(np.arange(LANES)[:, None] == blk_of[None, :], dtype=bf16)
    page = lambda j, i: pl.BlockSpec((1, 2, NSA_KVH, HD, PAGE_SIZE),
                                     lambda bi, pt: (pt[n_seq * bi + j, i], 0, 0, 0, 0))
    pages = [page(j, i) for j in range(n_seq) for i in range(n_pages)]
    per_b = lambda shape: pl.BlockSpec((n_seq,) + shape, lambda bi, pt: (bi,) + (0,) * len(shape))
    const = lambda shape: pl.BlockSpec(shape, lambda bi, pt: (0,) * len(shape))
    grid_spec = pltpu.PrefetchScalarGridSpec(
        num_scalar_prefetch=1, grid=(b // n_seq,),
        in_specs=pages * 2
                 + [per_b((DEC_ROWS, NSA_H * HD)), per_b((2, NSA_KVH, HD, w_buf)), per_b((DEC_ROWS, 2 * gd)),
                    per_b((DEC_ROWS, 2 * gd)), const((2, gd, PAGE_SIZE)), const((l_k, LANES)), const((LANES, l_k))],
        out_specs=[per_b((DEC_ROWS, NSA_H * HD))] * 3,
        scratch_shapes=[pltpu.VMEM((n_seq, NSA_H * DEC_ROWS, LANES), f32)])
    return pl.pallas_call(
        functools.partial(_dec_nsa_body, cfg), grid_spec=grid_spec,
        out_shape=[jax.ShapeDtypeStruct((b, DEC_ROWS, NSA_H * HD), f32)] * 3,
        compiler_params=_cparams("parallel"),
        name="decode_nsa",
    )(page_table, *([cmp_pool] * (n_seq * n_pages)), *([slc_pool] * (n_seq * n_pages)), qn, win_t, new_s, new_w,
      wp, e_all, oh_t)


def _dec_fox_body(cfg, pt_ref, *refs):
    n_pages, n_seq = cfg["n_pages"], cfg["n_seq"]
    tail = refs[2 * n_seq * n_pages:]
    for j in range(n_seq):
        _dec_fox_one(cfg, j, refs[j * n_pages:(j + 1) * n_pages],
                     refs[(n_seq + j) * n_pages:(n_seq + j + 1) * n_pages], *tail)


def _dec_fox_one(cfg, j, kv_pages, lf_pages, qbd_ref, new_ref, lfn_ref, u_ref, prev_ref, o_ref):
    n_pages, n_new = cfg["n_pages"], cfg["n_new"]
    hd_all = FOX_H * HD
    qbd = qbd_ref[j]
    u = u_ref[...]

    local = _split_dot(jnp.concatenate([p[0] for p in lf_pages], axis=0), u, 3)
    totals = jnp.broadcast_to(local[:, PAGE_SIZE - 1:], local.shape)
    f_all = local + _split_dot_rhs(prev_ref[...], totals, 3)
    f_k = jnp.concatenate([f_all[i * FOX_H:(i + 1) * FOX_H] for i in range(n_pages)], axis=1)
    carry = f_all[(n_pages - 1) * FOX_H:, PAGE_SIZE - 1:]
    f_new = _split_dot(lfn_ref[j], u, 3)

    k_t = jnp.concatenate([p[0, 0].reshape(hd_all, PAGE_SIZE) for p in kv_pages], axis=1).astype(bf16)
    v_t = jnp.concatenate([p[0, 1].reshape(hd_all, PAGE_SIZE) for p in kv_pages], axis=1).astype(bf16)
    new = new_ref[j]
    s1 = _dot(qbd, k_t)
    s2 = _dot_nt(qbd, _pad_keys(new, 0, hd_all))
    kj = lax.broadcasted_iota(jnp.int32, (1, LANES), 1)
    s1_q, s2_q, m2_q = [], [], []
    for qi in range(n_new):
        f_q = carry + f_new[:, qi:qi + 1]
        s1_q.append(s1[qi * FOX_H:(qi + 1) * FOX_H] + f_q - f_k)
        s2_q.append(s2[qi * FOX_H:(qi + 1) * FOX_H] + f_q - (carry + f_new))
        m2_q.append(jnp.broadcast_to(kj <= qi, (FOX_H, LANES)))
    p1, p2, den = _softmax2(jnp.concatenate(s1_q, axis=0), jnp.concatenate(s2_q, axis=0),
                            jnp.concatenate(m2_q, axis=0))
    o = (_dot_nt(p1.astype(bf16), v_t) + _dot(p2.astype(bf16), _pad_keys(new, hd_all, hd_all))) / den
    head = lax.broadcasted_iota(jnp.int32, (FOX_H, hd_all), 0)
    own = jnp.right_shift(lax.broadcasted_iota(jnp.int32, (FOX_H, hd_all), 1), CMP_SHIFT) == head
    out = [jnp.sum(jnp.where(own, o[qi * FOX_H:(qi + 1) * FOX_H], 0.0), axis=0, keepdims=True)
           for qi in range(n_new)]
    out.append(jnp.zeros((DEC_ROWS - n_new, hd_all), f32))
    o_ref[j] = jnp.concatenate(out, axis=0).astype(o_ref.dtype)


def _dec_fox(page_table, qbd, kv_pool, lf_pool, new_kv, lf_new, *, n_new):
    b, n_pages = page_table.shape
    hd_all = FOX_H * HD
    n_seq = DEC_SEQS if b % DEC_SEQS == 0 else 1
    cfg = dict(n_pages=n_pages, n_seq=n_seq, n_new=n_new)
    kv_page = lambda j, i: pl.BlockSpec((1, 2, FOX_H, HD, PAGE_SIZE),
                                        lambda bi, pt: (pt[n_seq * bi + j, i], 0, 0, 0, 0))
    lf_page = lambda j, i: pl.BlockSpec((1, FOX_H, PAGE_SIZE), lambda bi, pt: (pt[n_seq * bi + j, i], 0, 0))
    per_b = lambda shape: pl.BlockSpec((n_seq,) + shape, lambda bi, pt: (bi,) + (0,) * len(shape))
    seq_pages = [(j, i) for j in range(n_seq) for i in range(n_pages)]
    grid_spec = pltpu.PrefetchScalarGridSpec(
        num_scalar_prefetch=1, grid=(b // n_seq,),
        in_specs=[kv_page(j, i) for j, i in seq_pages] + [lf_page(j, i) for j, i in seq_pages]
                 + [per_b((FOX_H * n_new, hd_all)), per_b((DEC_ROWS, 2 * hd_all)), per_b((FOX_H, LANES)),
                    pl.BlockSpec((PAGE_SIZE, PAGE_SIZE), lambda bi, pt: (0, 0)),
                    pl.BlockSpec((FOX_H * n_pages, FOX_H * n_pages), lambda bi, pt: (0, 0))],
        out_specs=per_b((DEC_ROWS, hd_all)))
    r = np.arange(FOX_H * n_pages)
    prev = jnp.asarray((r[:, None] % FOX_H == r[None, :] % FOX_H) & (r[None, :] // FOX_H < r[:, None] // FOX_H),
                       dtype=bf16)
    return pl.pallas_call(
        functools.partial(_dec_fox_body, cfg), grid_spec=grid_spec,
        out_shape=jax.ShapeDtypeStruct((b, DEC_ROWS, hd_all), bf16),
        compiler_params=_cparams("parallel"),
        name="decode_fox",
    )(page_table, *([kv_pool] * (n_seq * n_pages)), *([lf_pool] * (n_seq * n_pages)), qbd, new_kv, lf_new,
      _upper_ones(PAGE_SIZE), prev)


def _dec_diff_body(cfg, pt_ref, *refs):
    n_pages, n_new, lam_init = cfg["n_pages"], cfg["n_new"], cfg["lam_init"]
    pages = refs[:n_pages]
    q_ref, knew_ref, vnew_ref, lq_ref, sg_ref, o_ref, qs_ref = refs[n_pages:]
    tq = DEC_ROWS
    rows = 2 * DIFF_H * tq
    cols = PAGE_SIZE * DIFF_H
    _stack_heads(q_ref[0], qs_ref, DIFF_H, tq)
    row = lax.broadcasted_iota(jnp.int32, (rows, 1), 0)
    row_head = jnp.right_shift(row, 4)
    qs = qs_ref[...].astype(bf16)
    col = lax.broadcasted_iota(jnp.int32, (1, cols), 1)
    own = (col & (DIFF_H - 1)) == row_head

    def part(s, mask, v_rows):
        s = jnp.where(mask, s, NEG_MASK)
        m = jnp.max(s, axis=1, keepdims=True)
        p = jnp.exp(s - m)
        return m, jnp.sum(p, axis=1, keepdims=True), _dot(p.astype(bf16), v_rows)

    parts = []
    for p in pages:
        k_rows = p[0, :, 0].reshape(cols, 2 * HD).astype(bf16)
        v_rows = p[0, :, 1].reshape(cols, 2 * HD).astype(bf16)
        parts.append(part(_dot_nt(qs, k_rows), own, v_rows))
    pad = jnp.zeros((LANES - knew_ref.shape[1], 2 * HD), f32)
    k_new = jnp.concatenate([knew_ref[0], pad], axis=0).astype(bf16)
    v_new = jnp.concatenate([vnew_ref[0], pad], axis=0).astype(bf16)
    lane = lax.broadcasted_iota(jnp.int32, (1, LANES), 1)
    mask_new = jnp.logical_and(jnp.logical_and((lane & (DIFF_H - 1)) == row_head,
                                               jnp.right_shift(lane, 3) <= (row & (tq - 1))),
                               lane < n_new * DIFF_H)
    parts.append(part(_dot_nt(qs, k_new), mask_new, v_new))
    m_all = functools.reduce(jnp.maximum, [m for m, _, _ in parts])
    scale = [jnp.exp(m - m_all) for m, _, _ in parts]
    l_fin = functools.reduce(lambda a, b_: a + b_, [w * l for w, (_, l, _) in zip(scale, parts)])
    acc = functools.reduce(lambda a, b_: a + b_, [w * a for w, (_, _, a) in zip(scale, parts)])
    o = acc / jnp.maximum(l_fin, 1e-30)
    heads = [_diff_combine(o[(2 * h) * tq:(2 * h + 1) * tq], o[(2 * h + 1) * tq:(2 * h + 2) * tq],
                           lq_ref[...], sg_ref[...], lam_init) for h in range(DIFF_H)]
    o_ref[0] = jnp.concatenate(heads, axis=1).astype(o_ref.dtype)


def _dec_diff(page_table, qd, pool, k_new, v_new, lam_qk_o, sub_g, *, n_new, lam_init):
    b, n_pages = page_table.shape
    cfg = dict(n_pages=n_pages, n_new=n_new, lam_init=lam_init)
    page = lambda i: pl.BlockSpec((1, PAGE_SIZE, 2, DIFF_H, 2 * HD), lambda bi, pt: (pt[bi, i], 0, 0, 0, 0))
    per_b = lambda shape: pl.BlockSpec((1,) + shape, lambda bi, pt: (bi,) + (0,) * len(shape))
    const = lambda shape: pl.BlockSpec(shape, lambda bi, pt: (0,) * len(shape))
    width = 2 * DIFF_H * HD
    grid_spec = pltpu.PrefetchScalarGridSpec(
        num_scalar_prefetch=1, grid=(b,),
        in_specs=[page(i) for i in range(n_pages)]
                 + [per_b((DEC_ROWS, width)), per_b((DIFF_H * n_new, 2 * HD)), per_b((DIFF_H * n_new, 2 * HD)),
                    const((4, HD)), const((1, 2 * HD))],
        out_specs=per_b((DEC_ROWS, width)),
        scratch_shapes=[pltpu.VMEM((2 * DIFF_H * DEC_ROWS, LANES), f32)])
    return pl.pallas_call(
        functools.partial(_dec_diff_body, cfg), grid_spec=grid_spec,
        out_shape=jax.ShapeDtypeStruct((b, DEC_ROWS, width), bf16),
        compiler_params=_cparams("parallel"),
        name="decode_diff",
    )(page_table, *([pool] * n_pages), qd, k_new, v_new, lam_qk_o, sub_g)


def _pad_rows(a, n):
    return jnp.pad(a, ((0, 0), (0, n - a.shape[1])) + ((0, 0),) * (a.ndim - 2))


def _trunk(x, mods, qpos0, page_table, caches, wts, params):
    (b_f, w_pool, lam_qk, subln_g, norm_mix_g, norm_ffn_g, final_g) = params
    b, t, d = x.shape
    m = b * t
    decode = caches is not None
    depth = len(mods)
    tm = min(256, m)
    x2d = x.reshape(m, d)
    pos = qpos0 + np.tile(np.arange(t), b)
    tables = _rope_tables(pos)

    if decode:
        tiles_per_mod = 1
        tm_mlp, mlp_tiles_per_mod = tm, 1
        expand = lambda v: jnp.repeat(v, t, axis=0).reshape(m // tm, tm, d)
    else:
        tiles_per_mod = t // tm
        tm_mlp = min(1024, t)
        mlp_tiles_per_mod = t // tm_mlp
        expand = lambda v: v.reshape(b, 1, d)

    per_seq = lambda a, rows=DEC_ROWS: _pad_rows(a.reshape(b, t, a.shape[-1]), rows)
    outs = {}
    for li in range(depth):
        sh1, sc1, gt1, sh2, sc2, gt2 = [expand(v) for v in jnp.split(mods[li], 6, axis=-1)]
        if li % 2 == 0:
            e = li // 2
            bias = jnp.pad(b_f[e].reshape(1, FOX_H), ((0, 0), (0, LANES - FOX_H)))
            po = _proj(x2d, norm_mix_g[li], sc1, sh1, tiles_per_mod, wts["in_even"][e], tables, bias,
                       _even_plan(), _even_out_defs(not decode), tm, t)
            logf = po[O_LOGF][:, :FOX_H].reshape(b, t, FOX_H)
            if decode:
                kv_leaf = lambda oi, heads: po[oi].reshape(b, t, 2, heads, HD)
            else:
                kv_leaf = lambda oi, heads: jnp.transpose(po[oi].reshape(b, 2, heads, HD, t), (0, 4, 1, 2, 3))
            outs.setdefault("cmp", []).append(kv_leaf(O_KVC, NSA_KVH))
            outs.setdefault("slc", []).append(kv_leaf(O_KVS, NSA_KVH))
            outs.setdefault("fox", []).append(kv_leaf(O_KVF, FOX_H))
            outs.setdefault("logf", []).append(logf)
            kvw_new = kv_leaf(O_KVW, NSA_KVH)
            if not decode:
                outs.setdefault("win", []).append(kvw_new[:, t - min(WINDOW, t):])
                qn, qf = po[O_QN].reshape(b, t, NSA_H * HD), po[O_QF].reshape(b, t, FOX_H * HD)
                cmp = _pad_rows(_compress(po[O_KVC_ROWS].reshape(b, t, 4 * HD), w_pool[e]), LANES)
                ck, cv = cmp[:, :, :LANES].astype(bf16), cmp[:, :, LANES:].astype(bf16)
                o_cmp, mb = _cmp_topk(qn, ck, cv, tq=min(256, t), nb_valid=t // CMP_BLOCK)
                lane_slab = lambda oi, w: po[oi].reshape(b, t, w)
                o_s = _flash("slc", qn, lane_slab(O_KS, LANES), lane_slab(O_VS, LANES), [mb],
                             tq=min(128, t), tk=min(FLASH_TK, t), out_dtype=f32)
                o_w = _flash("win", qn, lane_slab(O_KW, LANES), lane_slab(O_VW, LANES), [],
                             tq=min(256, t), tk=min(512, t), out_dtype=f32)
                tk_f = min(FLASH_TK, t)
                f_t = _cumsum_lanes(jnp.transpose(logf, (0, 2, 1)), min(512, t))
                fk = jnp.transpose(f_t.reshape(b, FOX_H // 2, 2, t // tk_f, tk_f), (0, 1, 3, 2, 4))
                fq = f_t.reshape(b, FOX_H // 2, 2, t, 1)
                o_f = _flash("fox", qf, lane_slab(O_KF, 4 * LANES), lane_slab(O_VF, 4 * LANES), [fk, fq],
                             tq=min(512, t), tk=tk_f, out_dtype=bf16)
                acts = [a.reshape(m, a.shape[-1]) for a in (o_cmp, o_s, o_w)] + [po[O_GATE], o_f.reshape(m, -1)]
            else:
                c_cmp, c_slc, c_win, c_fox, c_logf, _ = caches
                outs.setdefault("win", []).append(jnp.concatenate([c_win[e], kvw_new], axis=1)[:, t:])
                kt_view = lambda c: jnp.transpose(c, (0, 2, 3, 4, 1))
                qn = per_seq(po[O_QN])
                o_cmp, o_s, o_w = _dec_nsa(page_table, qn, kt_view(c_cmp[e]), kt_view(c_slc[e]),
                                           kt_view(c_win[e]), per_seq(po[O_KVS]), per_seq(po[O_KVW]),
                                           w_pool[e], qpos0=qpos0, n_new=t)
                qf = po[O_QF].reshape(b, t, 1, FOX_H * HD)
                head_of_lane = np.arange(FOX_H * HD) // HD
                own = jnp.asarray(head_of_lane[None, :] == np.arange(FOX_H)[:, None], dtype=bf16)
                qbd = (qf * own[None, None]).reshape(b, t * FOX_H, FOX_H * HD)
                lf_new = jnp.pad(jnp.transpose(logf, (0, 2, 1)), ((0, 0), (0, 0), (0, LANES - t)))
                o_f = _dec_fox(page_table, qbd, kt_view(c_fox[e]), jnp.transpose(c_logf[e], (0, 2, 1)),
                               per_seq(po[O_KVF]), lf_new, n_new=t)
                acts = [a[:, :t].reshape(m, a.shape[-1]) for a in (o_cmp, o_s, o_w)]
                acts += [po[O_GATE], o_f[:, :t].reshape(m, -1)]
            x2d = _outproj(_outproj_even_body, x2d, gt1, tiles_per_mod, acts, wts["out_even"][e], tm,
                           "outproj_even")
        else:
            o = li // 2
            po = _proj(x2d, norm_mix_g[li], sc1, sh1, tiles_per_mod, wts["in_odd"][o], tables,
                       jnp.zeros((1, LANES), f32), _odd_plan(), ODD_OUT_DEFS, tm, t)
            outs.setdefault("diff", []).append(po[O_KVD].reshape(b, t, 2, DIFF_H, 2 * HD))
            lam_init = 0.8 - 0.6 * math.exp(-0.3 * li)
            sub_g = subln_g[o].reshape(1, 2 * HD)
            width = 2 * DIFF_H * HD
            if not decode:
                o_d = _flash("diff", po[O_QD].reshape(b, t, width), po[O_KD].reshape(b, t, width),
                             po[O_VD].reshape(b, t, width), [lam_qk[o], sub_g], tq=min(512, t), tk=min(FLASH_TK, t),
                             out_dtype=bf16, lam_init=lam_init)
            else:
                kvd = po[O_KVD].reshape(b, t, 2, DIFF_H, 2 * HD)
                rows_new = lambda a: a.reshape(b, t * DIFF_H, 2 * HD)
                o_d = _dec_diff(page_table, per_seq(po[O_QD]), caches[5][o], rows_new(kvd[:, :, 0]),
                                rows_new(kvd[:, :, 1]), lam_qk[o], sub_g, n_new=t, lam_init=lam_init)[:, :t]
            x2d = _outproj(_outproj_odd_body, x2d, gt1, tiles_per_mod, [o_d.reshape(m, d)],
                           wts["out_odd"][o], tm, "outproj_odd")
        x2d = _mlp(x2d, norm_ffn_g[li], sc2, sh2, gt2, mlp_tiles_per_mod, wts["ff_in"][li], wts["ff_out"][li],
                   final_g, li == depth - 1, tm_mlp)
    stack = lambda name: jnp.stack(outs[name], 0)
    return (x2d.reshape(b, t, d), stack("cmp"), stack("slc"), stack("win"), stack("fox"), stack("logf"),
            stack("diff"))


def kernel(x_prompt, x_sample, cache_nsa_cmp_kv, cache_nsa_slc_kv, cache_nsa_win_kv, cache_fox_kv,
           cache_fox_logf, cache_diff_kv, page_table, c_prompt, c_sample, w_in_even, b_f, w_pool,
           w_out_even, w_in_odd, lam_qk, subln_g, w_out_odd, norm_mix_g, norm_ffn_g, w_ada, b_ada,
           w_ff_in, w_ff_out, final_g):
    depth = w_ada.shape[0]
    nb_p, nb_s = c_prompt.shape[0], c_sample.shape[0]
    past_len = page_table.shape[1] * PAGE_SIZE

    perm_rows = lambda w: jnp.concatenate(
        [w[:NSA_H * HD].reshape(NSA_H, HD, -1)[NSA_PERM, :, :].reshape(NSA_H * HD, -1), w[NSA_H * HD:]], axis=0)
    wts = dict(
        in_even=[_even_weight(w_in_even[e]) for e in range(w_in_even.shape[0])],
        out_even=[perm_rows(w_out_even[e]).astype(bf16) for e in range(w_out_even.shape[0])],
        in_odd=[w_in_odd[o].astype(bf16) for o in range(w_in_odd.shape[0])],
        out_odd=[w_out_odd[o].astype(bf16) for o in range(w_out_odd.shape[0])],
        ff_in=[w_ff_in[l].astype(bf16) for l in range(depth)],
        ff_out=[w_ff_out[l].astype(bf16) for l in range(depth)],
    )
    rows = -(-(nb_p + nb_s) // SUBLANES) * SUBLANES
    c_all = jnp.pad(jnp.concatenate([c_prompt, c_sample], axis=0), ((0, rows - nb_p - nb_s), (0, 0)))
    mod_all = _ada(c_all, w_ada, b_ada)
    mods_p = [mod_all[l, :nb_p] for l in range(depth)]
    mods_s = [mod_all[l, nb_p:nb_p + nb_s] for l in range(depth)]

    params = (b_f, w_pool, lam_qk, subln_g, norm_mix_g, norm_ffn_g, final_g)
    caches = (cache_nsa_cmp_kv, cache_nsa_slc_kv, cache_nsa_win_kv, cache_fox_kv, cache_fox_logf, cache_diff_kv)
    y_p, p_cmp, p_slc, p_win, p_fox, p_logf, p_diff = _trunk(x_prompt, mods_p, 0, None, None, wts, params)
    y_s, s_cmp, s_slc, s_win, s_fox, s_logf, s_diff = _trunk(x_sample, mods_s, past_len, page_table, caches,
                                                             wts, params)
    return (y_p, y_s, p_cmp, s_cmp, p_slc, s_slc, p_win, s_win, p_fox, s_fox, p_logf, s_logf, p_diff, s_diff)
```

```python
import functools
import math

import numpy as np
import jax
import jax.numpy as jnp
from jax import lax
from jax.experimental import pallas as pl
from jax.experimental.pallas import tpu as pltpu

HD = 64
ROPE_DIM = HD // 4
ROPE_HALF = ROPE_DIM // 2
ROPE_THETA = 500000.0
NSA_H = 8
NSA_KVH = 2
CMP_BLOCK = 64
CMP_SHIFT = 6
N_SEL = 16
WINDOW = 512
FOX_H = 8
DIFF_H = 8
EPS = 1e-6
PAGE_SIZE = 128

LANES = 128
SUBLANES = 8
VMEM_LIMIT_BYTES = 48 * 1024 * 1024

NEG_MASK = -1e30
NEG_BLOCK = -1e9
PAGES_PER_STEP = 4
FLASH_TK = 1024

NSA_PERM = (0, 4, 1, 5, 2, 6, 3, 7)

f32 = jnp.float32
bf16 = jnp.bfloat16


def _cparams(*sem):
    return pltpu.CompilerParams(dimension_semantics=sem, vmem_limit_bytes=VMEM_LIMIT_BYTES)


def _dot(a, b):
    return jnp.dot(a, b, preferred_element_type=f32)


def _dot_nt(a, b):
    return lax.dot_general(a, b, (((1,), (1,)), ((), ())), preferred_element_type=f32)


def _rms_mod(x, g, sc, sh):
    y = x * lax.rsqrt(jnp.mean(x * x, axis=-1, keepdims=True) + EPS)
    return (y * g) * (1.0 + sc) + sh


def _split_dot(x, w, parts):
    acc = None
    for _ in range(parts):
        piece = x.astype(bf16)
        term = _dot(piece, w)
        acc = term if acc is None else acc + term
        x = x - piece.astype(f32)
    return acc


def _split_dot_rhs(w, x, parts):
    acc = None
    for _ in range(parts):
        piece = x.astype(bf16)
        term = _dot(w, piece)
        acc = term if acc is None else acc + term
        x = x - piece.astype(f32)
    return acc


def _ada_body(c_ref, w_ref, b_ref, o_ref):
    c = c_ref[...]
    a = (c * jax.nn.sigmoid(c)).astype(bf16)
    o_ref[0] = _dot(a, w_ref[0].astype(bf16)) + b_ref[0]


def _ada(c_all, w_ada_b, b_ada):
    depth, d, n = w_ada_b.shape
    rows = c_all.shape[0]
    tn = n // 4
    return pl.pallas_call(
        _ada_body,
        out_shape=jax.ShapeDtypeStruct((depth, rows, n), f32),
        grid=(depth, n // tn),
        in_specs=[pl.BlockSpec((rows, d), lambda l, j: (0, 0)),
                  pl.BlockSpec((1, d, tn), lambda l, j: (l, 0, j)),
                  pl.BlockSpec((1, 1, tn), lambda l, j: (l, 0, j))],
        out_specs=pl.BlockSpec((1, rows, tn), lambda l, j: (l, 0, j)),
        compiler_params=_cparams("parallel", "parallel"),
        name="ada_mod",
    )(c_all, w_ada_b, b_ada.reshape(depth, 1, n))


def _rope_group(a, cos, sa, sb):
    return a * cos + pltpu.roll(a, LANES - ROPE_HALF, 1) * sa + pltpu.roll(a, ROPE_HALF, 1) * sb


def _proj_body(plan, x_ref, g_ref, sc_ref, sh_ref, w_ref, cos_ref, sa_ref, sb_ref, bias_ref, *out_refs):
    h = _rms_mod(x_ref[...], g_ref[...], sc_ref[0], sh_ref[0]).astype(bf16)
    cos, sa, sb = cos_ref[...], sa_ref[...], sb_ref[...]
    for src, groups in plan:
        width = LANES * len(groups)
        acc = _dot(h, w_ref[:, src:src + width])
        for gi, (kind, scale, dests) in enumerate(groups):
            a = acc[:, gi * LANES:(gi + 1) * LANES]
            if kind == "rope":
                a = _rope_group(a, cos, sa, sb)
            elif kind == "sigmoid":
                a = jax.nn.sigmoid(a)
            elif kind == "logsig":
                z = a + bias_ref[...]
                a = jnp.minimum(z, 0.0) - jnp.log1p(jnp.exp(-jnp.abs(z)))
            if scale != 1.0:
                a = a * scale
            for oi, col in dests:
                if len(out_refs[oi].shape) == 3:
                    out_refs[oi][0, col:col + LANES, :] = a.T.astype(out_refs[oi].dtype)
                else:
                    out_refs[oi][:, col:col + LANES] = a.astype(out_refs[oi].dtype)


def _proj(x2d, g, sc, sh, tiles_per_mod, w_b, tables, bias, plan, out_defs, tm, seq_len):
    m, d = x2d.shape
    n = w_b.shape[1]
    cos, sa, sb = tables
    mod_rows = sc.shape[1]
    row = lambda i: (i, 0)
    const = lambda i: (0, 0)
    mod = lambda i: (i // tiles_per_mod, 0, 0)
    tiles_per_seq = seq_len // tm
    out_shape, out_specs = [], []
    for w, dt, position_minor in out_defs:
        if position_minor:
            out_shape.append(jax.ShapeDtypeStruct((m // seq_len, w, seq_len), dt))
            out_specs.append(pl.BlockSpec((1, w, tm), lambda i: (i // tiles_per_seq, 0, i % tiles_per_seq)))
        else:
            out_shape.append(jax.ShapeDtypeStruct((m, w), dt))
            out_specs.append(pl.BlockSpec((tm, w), row))
    return pl.pallas_call(
        functools.partial(_proj_body, plan),
        out_shape=out_shape,
        grid=(m // tm,),
        in_specs=[pl.BlockSpec((tm, d), row),
                  pl.BlockSpec((1, d), const),
                  pl.BlockSpec((1, mod_rows, d), mod),
                  pl.BlockSpec((1, mod_rows, d), mod),
                  pl.BlockSpec((d, n), const, pipeline_mode=pl.Buffered(1)),
                  pl.BlockSpec((tm, LANES), row),
                  pl.BlockSpec((tm, LANES), row),
                  pl.BlockSpec((tm, LANES), row),
                  pl.BlockSpec((1, LANES), const)],
        out_specs=out_specs,
        compiler_params=_cparams("parallel"),
        name="norm_mod_proj",
    )(x2d, g.reshape(1, d), sc, sh, w_b, cos, sa, sb, bias)


def _rope_tables(pos):
    inv = np.float32(ROPE_THETA) ** (-np.arange(ROPE_HALF, dtype=np.float32) / np.float32(ROPE_HALF))
    ang = (pos.astype(np.float32)[:, None] * inv[None, :]).astype(np.float64)
    n = pos.shape[0]
    cos = np.ones((n, HD), np.float64)
    sa = np.zeros((n, HD), np.float64)
    sb = np.zeros((n, HD), np.float64)
    cos[:, :ROPE_HALF] = np.cos(ang)
    cos[:, ROPE_HALF:ROPE_DIM] = np.cos(ang)
    sa[:, :ROPE_HALF] = -np.sin(ang)
    sb[:, ROPE_HALF:ROPE_DIM] = np.sin(ang)
    tile = lambda t: jnp.asarray(np.tile(t, (1, LANES // HD)).astype(np.float32))
    return tile(cos), tile(sa), tile(sb)


EV_QN, EV_KVC, EV_KVS, EV_KVW, EV_QF, EV_KVF, EV_GATE, EV_FF = 0, 512, 768, 1024, 1280, 1792, 2816, 4352


def _even_weight(w):
    d = w.shape[0]
    sizes = [NSA_H * HD, 2 * NSA_KVH * HD, 2 * NSA_KVH * HD, 2 * NSA_KVH * HD, 3 * NSA_H, FOX_H * HD,
             2 * FOX_H * HD, FOX_H]
    offs = np.concatenate([[0], np.cumsum(sizes)])
    q_n, kv_c, kv_s, kv_w, gates, q_f, kv_f, f_f = [w[:, offs[i]:offs[i + 1]] for i in range(8)]
    q_n = q_n.reshape(d, NSA_H, HD)[:, NSA_PERM, :].reshape(d, NSA_H * HD)
    gates = gates.reshape(d, NSA_H, 3)[:, NSA_PERM, :]
    gates = jnp.repeat(jnp.transpose(gates, (0, 2, 1))[..., None], HD, axis=-1).reshape(d, 3 * NSA_H * HD)
    f_f = jnp.pad(f_f, ((0, 0), (0, LANES - FOX_H)))
    return jnp.concatenate([q_n, kv_c, kv_s, kv_w, q_f, kv_f, gates, f_f], axis=1).astype(bf16)


(O_QN, O_KVC, O_KVS, O_KS, O_VS, O_KVW, O_KW, O_VW, O_QF, O_KVF, O_KF, O_VF, O_GATE, O_LOGF, O_KVC_ROWS) = range(15)


def _even_out_defs(position_minor):
    pm = position_minor
    return [(512, bf16, False), (256, f32, pm), (256, f32, pm), (128, bf16, False), (128, bf16, False),
            (256, f32, pm), (128, bf16, False), (128, bf16, False), (512, bf16, False), (1024, f32, pm),
            (512, bf16, False), (512, bf16, False), (1536, f32, False), (128, f32, False), (256, f32, False)]


def _even_plan():
    scale = HD ** -0.5
    plan = [(EV_QN, [("rope", scale, [(O_QN, LANES * i)]) for i in range(4)]),
            (EV_KVC, [("rope", 1.0, [(O_KVC, 0), (O_KVC_ROWS, 0)]),
                      ("plain", 1.0, [(O_KVC, LANES), (O_KVC_ROWS, LANES)])]),
            (EV_KVS, [("rope", 1.0, [(O_KVS, 0), (O_KS, 0)]), ("plain", 1.0, [(O_KVS, LANES), (O_VS, 0)])]),
            (EV_KVW, [("rope", 1.0, [(O_KVW, 0), (O_KW, 0)]), ("plain", 1.0, [(O_KVW, LANES), (O_VW, 0)])]),
            (EV_QF, [("plain", scale, [(O_QF, LANES * i)]) for i in range(4)]),
            (EV_KVF, [("plain", 1.0, [(O_KVF, LANES * i), (O_KF, LANES * i)]) for i in range(4)]),
            (EV_KVF + 512, [("plain", 1.0, [(O_KVF, 512 + LANES * i), (O_VF, LANES * i)]) for i in range(4)])]
    for j in range(3):
        plan.append((EV_GATE + 512 * j, [("sigmoid", 1.0, [(O_GATE, 512 * j + LANES * i)]) for i in range(4)]))
    plan.append((EV_FF, [("logsig", 1.0, [(O_LOGF, 0)])]))
    return plan


(O_QD, O_KVD, O_KD, O_VD) = range(4)
ODD_OUT_DEFS = [(1024, bf16, False), (2048, f32, False), (1024, bf16, False), (1024, bf16, False)]


def _odd_plan():
    scale = HD ** -0.5
    plan = []
    for c in range(2):
        plan.append((512 * c, [("rope", scale, [(O_QD, 512 * c + LANES * i)]) for i in range(4)]))
    for c in range(2):
        plan.append((1024 + 512 * c,
                     [("rope", 1.0, [(O_KVD, 512 * c + LANES * i), (O_KD, 512 * c + LANES * i)]) for i in range(4)]))
    for c in range(2):
        plan.append((2048 + 512 * c,
                     [("plain", 1.0, [(O_KVD, 1024 + 512 * c + LANES * i), (O_VD, 512 * c + LANES * i)])
                      for i in range(4)]))
    return plan


def _outproj_even_body(x_ref, gt_ref, oc_ref, os_ref, ow_ref, gates_ref, of_ref, w_ref, o_ref):
    nw = NSA_H * HD
    o_nsa = (gates_ref[:, 0:nw] * oc_ref[...] + gates_ref[:, nw:2 * nw] * os_ref[...]
             + gates_ref[:, 2 * nw:3 * nw] * ow_ref[...])
    a = jnp.concatenate([o_nsa.astype(bf16), of_ref[...]], axis=1)
    o_ref[...] = x_ref[...] + gt_ref[0] * _dot(a, w_ref[...])


def _outproj_odd_body(x_ref, gt_ref, od_ref, w_ref, o_ref):
    o_ref[...] = x_ref[...] + gt_ref[0] * _dot(od_ref[...], w_ref[...])


def _outproj(body, x2d, gt, tiles_per_mod, acts, w_b, tm, name):
    m, d = x2d.shape
    row = lambda i: (i, 0)
    mod = lambda i: (i // tiles_per_mod, 0, 0)
    return pl.pallas_call(
        body,
        out_shape=jax.ShapeDtypeStruct((m, d), f32),
        grid=(m // tm,),
        in_specs=[pl.BlockSpec((tm, d), row), pl.BlockSpec((1, gt.shape[1], d), mod)]
                 + [pl.BlockSpec((tm, a.shape[1]), row) for a in acts]
                 + [pl.BlockSpec(w_b.shape, lambda i: (0, 0), pipeline_mode=pl.Buffered(1))],
        out_specs=pl.BlockSpec((tm, d), row),
        compiler_params=_cparams("parallel"),
        name=name,
    )(x2d, gt, *acts, w_b)


MLP_SLICE = 512


def _mlp_body(final, x_ref, g_ref, sc_ref, sh_ref, gt_ref, w1_ref, w2_ref, fg_ref, o_ref):
    x = x_ref[...]
    h = _rms_mod(x, g_ref[...], sc_ref[0], sh_ref[0]).astype(bf16)
    ff = None
    for lo in range(0, w1_ref.shape[1], MLP_SLICE):
        a = jnp.maximum(_dot(h, w1_ref[:, lo:lo + MLP_SLICE]), 0.0)
        part = _dot((a * a).astype(bf16), w2_ref[lo:lo + MLP_SLICE, :])
        ff = part if ff is None else ff + part
    y = x + gt_ref[0] * ff
    if final:
        y = y * lax.rsqrt(jnp.mean(y * y, axis=-1, keepdims=True) + EPS) * fg_ref[...]
    o_ref[...] = y


def _mlp(x2d, g, sc, sh, gt, tiles_per_mod, w1_b, w2_b, final_g, final, tm):
    m, d = x2d.shape
    dff = w1_b.shape[1]
    mod_rows = sc.shape[1]
    row = lambda i: (i, 0)
    const = lambda i: (0, 0)
    mod = lambda i: (i // tiles_per_mod, 0, 0)
    return pl.pallas_call(
        functools.partial(_mlp_body, final),
        out_shape=jax.ShapeDtypeStruct((m, d), f32),
        grid=(m // tm,),
        in_specs=[pl.BlockSpec((tm, d), row), pl.BlockSpec((1, d), const),
                  pl.BlockSpec((1, mod_rows, d), mod), pl.BlockSpec((1, mod_rows, d), mod),
                  pl.BlockSpec((1, mod_rows, d), mod),
                  pl.BlockSpec((d, dff), const, pipeline_mode=pl.Buffered(1)),
                  pl.BlockSpec((dff, d), const, pipeline_mode=pl.Buffered(1)),
                  pl.BlockSpec((1, d), const)],
        out_specs=pl.BlockSpec((tm, d), row),
        compiler_params=_cparams("parallel"),
        name="relu2_mlp",
    )(x2d, g.reshape(1, d), sc, sh, gt, w1_b, w2_b, final_g.reshape(1, d))


def _compress_body(x_ref, wp_ref, o_ref):
    wp = wp_ref[...]
    rows = []
    for blk in range(x_ref.shape[1] // CMP_BLOCK):
        rows.append(jnp.sum(x_ref[0, blk * CMP_BLOCK:(blk + 1) * CMP_BLOCK, :] * wp, axis=0, keepdims=True))
    o_ref[0] = jnp.concatenate(rows, axis=0)


def _compress(kv, w_pool_e):
    b, t, c = kv.shape
    wp = jnp.repeat(jnp.transpose(w_pool_e, (1, 0, 2))[..., None], HD, axis=-1).reshape(CMP_BLOCK, c)
    rows_in = PAGES_PER_STEP * PAGE_SIZE
    rows_out = rows_in // CMP_BLOCK
    return pl.pallas_call(
        _compress_body,
        out_shape=jax.ShapeDtypeStruct((b, t // CMP_BLOCK, c), f32),
        grid=(b, t // rows_in),
        in_specs=[pl.BlockSpec((1, rows_in, c), lambda bi, g: (bi, g, 0)),
                  pl.BlockSpec((CMP_BLOCK, c), lambda bi, g: (0, 0))],
        out_specs=pl.BlockSpec((1, rows_out, c), lambda bi, g: (bi, g, 0)),
        compiler_params=_cparams("parallel", "parallel"),
        name="nsa_compress",
    )(kv, wp)


def _cumsum_body(x_ref, u_ref, o_ref, carry_ref):
    @pl.when(pl.program_id(1) == 0)
    def _():
        carry_ref[...] = jnp.zeros_like(carry_ref)

    cs = _split_dot(x_ref[0], u_ref[...], 3) + carry_ref[...]
    o_ref[0] = cs
    carry_ref[...] = cs[:, cs.shape[1] - 1:]


def _upper_ones(n):
    return jnp.asarray(np.triu(np.ones((n, n), np.float32)), dtype=bf16)


def _cumsum_lanes(x_t, tc):
    b, h, l = x_t.shape
    return pl.pallas_call(
        _cumsum_body,
        out_shape=jax.ShapeDtypeStruct((b, h, l), f32),
        grid=(b, l // tc),
        in_specs=[pl.BlockSpec((1, h, tc), lambda bi, c: (bi, 0, c)),
                  pl.BlockSpec((tc, tc), lambda bi, c: (0, 0))],
        out_specs=pl.BlockSpec((1, h, tc), lambda bi, c: (bi, 0, c)),
        scratch_shapes=[pltpu.VMEM((h, 1), f32)],
        compiler_params=_cparams("parallel", "arbitrary"),
        name="logf_cumsum",
    )(x_t, _upper_ones(tc))


def _stack_heads(q, qs_ref, n_slab, tq):
    lane = lax.broadcasted_iota(jnp.int32, (tq, LANES), 1)
    low = lane < HD
    for s in range(n_slab):
        slab = q[:, s * LANES:(s + 1) * LANES].astype(f32)
        qs_ref[(2 * s) * tq:(2 * s + 1) * tq, 0:LANES] = jnp.where(low, slab, 0.0).astype(qs_ref.dtype)
        qs_ref[(2 * s + 1) * tq:(2 * s + 2) * tq, 0:LANES] = jnp.where(low, 0.0, slab).astype(qs_ref.dtype)


def _unstack_heads(o, n_slab, tq):
    lane = lax.broadcasted_iota(jnp.int32, (tq, LANES), 1)
    low = lane < HD
    slabs = [jnp.where(low, o[(2 * s) * tq:(2 * s + 1) * tq], o[(2 * s + 1) * tq:(2 * s + 2) * tq])
             for s in range(n_slab)]
    return slabs[0] if n_slab == 1 else jnp.concatenate(slabs, axis=1)


def _masked_softmax(s, valid):
    sm = jnp.where(valid, s, NEG_MASK)
    m = jnp.max(sm, axis=1, keepdims=True)
    p = jnp.where(valid, jnp.exp(sm - m), 0.0)
    return p / jnp.maximum(jnp.sum(p, axis=1, keepdims=True), 1e-30)


def _select_blocks(p, tg, tq, nb_valid):
    n = lax.broadcasted_iota(jnp.int32, (1, LANES), 1)
    lane_f = n.astype(f32)
    cur = jnp.right_shift(tg, CMP_SHIFT)
    n_cand = jnp.minimum(cur, nb_valid)
    out = []
    for g in range(NSA_KVH):
        imp = p[g * tq:(g + 1) * tq]
        for s_i in range(1, NSA_H // 2):
            imp = imp + p[(2 * s_i + g) * tq:(2 * s_i + g + 1) * tq]
        work = jnp.where(n < n_cand, imp, -1.0)
        sel = n == cur
        for it in range(N_SEL - 1):
            mx = jnp.max(work, axis=1, keepdims=True)
            hit = lane_f == jnp.min(jnp.where(work == mx, lane_f, float(LANES)), axis=1, keepdims=True)
            sel = jnp.logical_or(sel, jnp.logical_and(hit, it < n_cand))
            work = jnp.where(hit, -2.0, work)
        out.append(jnp.where(sel, 0.0, NEG_BLOCK))
    return out


def _select_blocks_ranked(p, tg, tq, nb_valid):
    n = lax.broadcasted_iota(jnp.int32, (1, LANES), 1)
    n_col = lax.broadcasted_iota(jnp.int32, (LANES, 1), 0)
    cur = jnp.right_shift(tg, CMP_SHIFT)
    n_cand = jnp.minimum(cur, nb_valid)
    n_pick = jnp.minimum(n_cand, N_SEL - 1).astype(f32)
    out = []
    for g in range(NSA_KVH):
        imp = p[g * tq:(g + 1) * tq]
        for s_i in range(1, NSA_H // 2):
            imp = imp + p[(2 * s_i + g) * tq:(2 * s_i + g + 1) * tq]
        work = jnp.where(n < n_cand, imp, -1.0)
        work_t = jnp.concatenate([work, jnp.zeros((LANES - tq, LANES), f32)], axis=0).T
        ranks = []
        for r in range(tq):
            other = work_t[:, r:r + 1]
            mine = work[r:r + 1, :]
            ahead = jnp.logical_or(other > mine, jnp.logical_and(other == mine, n_col < n))
            ranks.append(jnp.sum(jnp.where(ahead, 1.0, 0.0), axis=0, keepdims=True))
        rank = jnp.concatenate(ranks, axis=0)
        sel = jnp.logical_or(n == cur, rank < n_pick)
        out.append(jnp.where(sel, 0.0, NEG_BLOCK))
    return out


def _flash_body(cfg, *refs):
    mode, n_slab, tq, tk, n_chunks = cfg["mode"], cfg["n_slab"], cfg["tq"], cfg["tk"], cfg["n_chunks"]
    window = cfg["window"]
    rows = 2 * n_slab * tq
    q_ref, k_ref, v_ref = refs[:3]
    extra = refs[3:-5]
    o_ref, qs_ref, s_ref, m_ref, acc_ref = refs[-5:]
    qi = pl.program_id(2)

    _stack_heads(q_ref[0], qs_ref, n_slab, tq)
    if mode == "slc":
        mb_ref = extra[0]
        for s in range(n_slab):
            for g in range(2):
                qs_ref[(2 * s + g) * tq:(2 * s + g + 1) * tq, LANES:2 * LANES] = mb_ref[0, g]
    m_ref[...] = jnp.full_like(m_ref, NEG_MASK)
    acc_ref[...] = jnp.zeros_like(acc_ref)

    q_lo = qi * tq
    qpos = q_lo + (lax.broadcasted_iota(jnp.int32, (rows, 1), 0) & (tq - 1))

    def scores(c):
        ks = pl.multiple_of(c * tk, tk)
        k = k_ref[0, pl.ds(ks, tk), :]
        if mode == "slc":
            blk = jnp.right_shift(ks + lax.broadcasted_iota(jnp.int32, (tk, LANES), 0), CMP_SHIFT)
            hot = jnp.where(blk == lax.broadcasted_iota(jnp.int32, (tk, LANES), 1), 1.0, 0.0).astype(bf16)
            k = jnp.concatenate([k, hot], axis=1)
        return _dot_nt(qs_ref[...], k)

    def update(slot, c, masked, width=tk):
        ks = pl.multiple_of(c * tk, tk)
        s = s_ref[slot, :, 0:width]
        row_bias = 0.0
        if mode == "fox":
            fk_ref, fq_ref = extra
            fk = fk_ref[0, 0, c][:, 0:width]
            s = jnp.concatenate([s[0:tq] - fk[0:1], s[tq:2 * tq] - fk[1:2]], axis=0)
            row_bias = jnp.concatenate([fq_ref[0, 0, 0], fq_ref[0, 0, 1]], axis=0)
        if masked:
            kpos = ks + lax.broadcasted_iota(jnp.int32, (1, width), 1)
            mask = kpos <= qpos if masked == "causal" else kpos > qpos - window
            s = jnp.where(mask, s, NEG_MASK)
        m_old = m_ref[...]
        m_new = jnp.maximum(m_old, jnp.max(s, axis=1, keepdims=True) + row_bias)
        alpha = jnp.exp(m_old - m_new)
        p = jnp.exp((s + (row_bias - m_new)).astype(bf16))
        v_ones = jnp.concatenate([v_ref[0, pl.ds(ks, width), :], jnp.ones((width, LANES), bf16)], axis=1)
        acc_ref[...] = alpha * acc_ref[...] + _dot(p, v_ones)
        m_ref[...] = m_new

    def issue(c, slot):
        s_ref[slot] = scores(c)

    if window is not None:
        c_last = q_lo // tk

        @pl.when(c_last > 0)
        def _():
            issue(c_last - 1, 0)
            issue(c_last, 1)
            update(0, c_last - 1, "window")
            update(1, c_last, "causal")

        @pl.when(c_last == 0)
        def _():
            issue(0, 1)
            update(1, 0, "causal")
    else:
        n_full = (q_lo + 1) // tk
        visible = q_lo + tq - n_full * tk
        widths = [w for w in (tk // 4, tk // 2, tk) if w >= max(tq, 2 * LANES)]

        def diagonal(slot):
            for idx, w in enumerate(widths):
                lower = widths[idx - 1] if idx else 0

                @pl.when(jnp.logical_and(visible > lower, visible <= w))
                def _():
                    update(slot, n_full, "causal", w)

        issue(0, 0)

        def pair(j, carry):
            c = 2 * j
            issue(c + 1, 1)
            update(0, c, False)
            issue(c + 2, 0)
            update(1, c + 1, False)
            return carry

        lax.fori_loop(0, n_full // 2, pair, 0)

        @pl.when((n_full & 1) == 1)
        def _():
            issue(n_full, 1)
            update(0, n_full - 1, False)
            diagonal(1)

        @pl.when((n_full & 1) == 0)
        def _():
            diagonal(0)

    acc = acc_ref[...]
    o = acc[:, :LANES] / jnp.maximum(acc[:, LANES:], 1e-30)
    if mode == "diff":
        lq_ref, sg_ref = extra
        o_ref[0] = _diff_combine(o[0:tq], o[tq:2 * tq], lq_ref[...], sg_ref[...], cfg["lam_init"]).astype(o_ref.dtype)
    else:
        o_ref[0] = _unstack_heads(o, n_slab, tq).astype(o_ref.dtype)


def _diff_combine(o1, o2, lp, sub_g, lam_init):
    lam = (jnp.exp(jnp.sum(lp[0:1] * lp[1:2], axis=1, keepdims=True))
           - jnp.exp(jnp.sum(lp[2:3] * lp[3:4], axis=1, keepdims=True)) + lam_init)
    d = o1 - lam * o2
    d = d * lax.rsqrt(jnp.mean(d * d, axis=-1, keepdims=True) + EPS) * sub_g
    return d * (1.0 - lam_init)


def _flash(mode, q, k, v, extra, *, tq, tk, out_dtype, lam_init=0.0):
    b, t_q, cq = q.shape
    t_k = k.shape[1]
    n_slab = 4 if mode in ("slc", "win") else 1
    n_grid_slab = cq // (LANES * n_slab)
    n_chunks = t_k // tk
    rows = 2 * n_slab * tq
    ck = 2 * LANES if mode == "slc" else LANES
    assert tk % tq == 0
    if mode == "win":
        assert tk == WINDOW
    score_slots = (2, rows, tk)
    cfg = dict(mode=mode, n_slab=n_slab, tq=tq, tk=tk, n_chunks=n_chunks,
               window=WINDOW if mode == "win" else None, lam_init=lam_init)
    qw = LANES * n_slab
    in_specs = [pl.BlockSpec((1, tq, qw), lambda bi, s, i: (bi, i, s)),
                pl.BlockSpec((1, t_k, LANES), lambda bi, s, i: (bi, 0, s)),
                pl.BlockSpec((1, t_k, LANES), lambda bi, s, i: (bi, 0, s))]
    if mode == "fox":
        in_specs += [pl.BlockSpec((1, 1, n_chunks, 2, tk), lambda bi, s, i: (bi, s, 0, 0, 0)),
                     pl.BlockSpec((1, 1, 2, tq, 1), lambda bi, s, i: (bi, s, 0, i, 0))]
    elif mode == "slc":
        in_specs += [pl.BlockSpec((1, 2, tq, LANES), lambda bi, s, i: (bi, 0, i, 0))]
    elif mode == "diff":
        in_specs += [pl.BlockSpec((4, HD), lambda bi, s, i: (0, 0)),
                     pl.BlockSpec((1, LANES), lambda bi, s, i: (0, 0))]
    return pl.pallas_call(
        functools.partial(_flash_body, cfg),
        out_shape=jax.ShapeDtypeStruct((b, t_q, cq), out_dtype),
        grid=(b, n_grid_slab, t_q // tq),
        in_specs=in_specs,
        out_specs=pl.BlockSpec((1, tq, qw), lambda bi, s, i: (bi, i, s)),
        scratch_shapes=[pltpu.VMEM((rows, ck), bf16), pltpu.VMEM(score_slots, f32), pltpu.VMEM((rows, 1), f32),
                        pltpu.VMEM((rows, 2 * LANES), f32)],
        compiler_params=_cparams("parallel", "parallel", "parallel"),
        name="flash_" + mode,
    )(q, k, v, *extra)


def _cmp_body(tq, nb_valid, q_ref, ck_ref, cv_ref, o_ref, mb_ref, qs_ref):
    n_slab = NSA_H // 2
    rows = 2 * n_slab * tq
    qi = pl.program_id(1)
    _stack_heads(q_ref[0], qs_ref, n_slab, tq)
    s = _dot_nt(qs_ref[...], ck_ref[0])
    t = qi * tq + (lax.broadcasted_iota(jnp.int32, (rows, 1), 0) & (tq - 1))
    n = lax.broadcasted_iota(jnp.int32, (1, LANES), 1)
    valid = jnp.logical_and((n + 1) * CMP_BLOCK - 1 <= t, n < nb_valid)
    p = _masked_softmax(s, valid)
    o_ref[0] = _unstack_heads(_dot(p.astype(bf16), cv_ref[0]), n_slab, tq)
    bias = _select_blocks(p, t[0:tq], tq, nb_valid)
    for g in range(NSA_KVH):
        mb_ref[0, g] = bias[g].astype(bf16)


def _cmp_topk(q, ck, cv, *, tq, nb_valid):
    b, t_q, cq = q.shape
    rows = NSA_H * tq
    return pl.pallas_call(
        functools.partial(_cmp_body, tq, nb_valid),
        out_shape=[jax.ShapeDtypeStruct((b, t_q, cq), f32), jax.ShapeDtypeStruct((b, NSA_KVH, t_q, LANES), bf16)],
        grid=(b, t_q // tq),
        in_specs=[pl.BlockSpec((1, tq, cq), lambda bi, i: (bi, i, 0)),
                  pl.BlockSpec((1, LANES, LANES), lambda bi, i: (bi, 0, 0)),
                  pl.BlockSpec((1, LANES, LANES), lambda bi, i: (bi, 0, 0))],
        out_specs=[pl.BlockSpec((1, tq, cq), lambda bi, i: (bi, i, 0)),
                   pl.BlockSpec((1, NSA_KVH, tq, LANES), lambda bi, i: (bi, 0, i, 0))],
        scratch_shapes=[pltpu.VMEM((rows, LANES), bf16)],
        compiler_params=_cparams("parallel", "parallel"),
        name="nsa_cmp_topk",
    )(q, ck, cv)


DEC_ROWS = SUBLANES
DEC_SEQS = 2


def _pad_keys(new, lo, width):
    x = new[:, lo:lo + width]
    return jnp.concatenate([x, jnp.zeros((LANES - x.shape[0], width), f32)], axis=0).astype(bf16)


def _new_key_mask(rows, n_new):
    qi = lax.broadcasted_iota(jnp.int32, (rows, 1), 0) & (DEC_ROWS - 1)
    kj = lax.broadcasted_iota(jnp.int32, (1, LANES), 1)
    return jnp.logical_and(kj <= qi, kj < n_new)


def _softmax2(s1, s2, mask2):
    s2 = jnp.where(mask2, s2, NEG_MASK)
    m = jnp.maximum(jnp.max(s1, axis=1, keepdims=True), jnp.max(s2, axis=1, keepdims=True))
    p1 = jnp.exp(s1 - m)
    p2 = jnp.where(mask2, jnp.exp(s2 - m), 0.0)
    den = jnp.sum(p1, axis=1, keepdims=True) + jnp.sum(p2, axis=1, keepdims=True)
    return p1, p2, jnp.maximum(den, 1e-30)


def _dec_nsa_body(cfg, pt_ref, *refs):
    n_pages, n_seq = cfg["n_pages"], cfg["n_seq"]
    (q_ref, win_ref, news_ref, neww_ref, wp_ref, e_ref, oht_ref,
     oc_ref, os_ref, ow_ref, qs_ref) = refs[2 * n_seq * n_pages:]
    for j in range(n_seq):
        _dec_nsa_one(cfg, j, refs[j * n_pages:(j + 1) * n_pages],
                     refs[(n_seq + j) * n_pages:(n_seq + j + 1) * n_pages], q_ref, win_ref, news_ref, neww_ref,
                     wp_ref, e_ref, oht_ref, oc_ref, os_ref, ow_ref, qs_ref.at[j])


def _dec_nsa_one(cfg, j, cmp_pages, slc_pages, q_ref, win_ref, news_ref, neww_ref, wp_ref, e_ref, oht_ref,
                 oc_ref, os_ref, ow_ref, qs_ref):
    n_pages, qpos0, n_new, w_buf = cfg["n_pages"], cfg["qpos0"], cfg["n_new"], cfg["w_buf"]
    tq, n_slab = DEC_ROWS, NSA_H // 2
    rows = NSA_H * tq
    gd = NSA_KVH * HD

    _stack_heads(q_ref[j], qs_ref, n_slab, tq)
    qs = qs_ref[...].astype(bf16)
    t = qpos0 + (lax.broadcasted_iota(jnp.int32, (rows, 1), 0) & (tq - 1))
    n = lax.broadcasted_iota(jnp.int32, (1, LANES), 1)
    mask_new = _new_key_mask(rows, n_new)

    def compressed(kv):
        xt = jnp.concatenate([p[0, kv].reshape(gd, PAGE_SIZE) for p in cmp_pages], axis=1)
        w = jnp.concatenate([wp_ref[kv]] * n_pages, axis=1)
        return _split_dot(xt * w, e_ref[...], 2).astype(bf16)

    ck_t, cv_t = compressed(0), compressed(1)
    nb_valid = n_pages * PAGE_SIZE // CMP_BLOCK
    valid = jnp.logical_and((n + 1) * CMP_BLOCK - 1 <= t, n < nb_valid)
    p = _masked_softmax(_dot(qs, ck_t), valid)
    oc_ref[j] = _unstack_heads(_dot_nt(p.astype(bf16), cv_t), n_slab, tq)

    bias = _select_blocks_ranked(p, t[0:tq], tq, nb_valid)
    qq = jnp.concatenate([qs, jnp.concatenate(bias * n_slab, axis=0).astype(bf16)], axis=1)
    ks_t = jnp.concatenate([p_[0, 0].reshape(gd, PAGE_SIZE) for p_ in slc_pages], axis=1).astype(bf16)
    vs_t = jnp.concatenate([p_[0, 1].reshape(gd, PAGE_SIZE) for p_ in slc_pages], axis=1).astype(bf16)
    s1 = _dot(qq, jnp.concatenate([ks_t, oht_ref[...]], axis=0))
    new_s = news_ref[j]
    s2 = _dot_nt(qs, _pad_keys(new_s, 0, gd))
    p1, p2, den = _softmax2(s1, s2, mask_new)
    o = _dot_nt(p1.astype(bf16), vs_t) + _dot(p2.astype(bf16), _pad_keys(new_s, gd, gd))
    os_ref[j] = _unstack_heads(o / den, n_slab, tq)

    kw_t = win_ref[j, 0].reshape(gd, w_buf).astype(bf16)
    vw_t = win_ref[j, 1].reshape(gd, w_buf).astype(bf16)
    kpos = qpos0 - w_buf + lax.broadcasted_iota(jnp.int32, (1, w_buf), 1)
    s1 = jnp.where((t - kpos) < WINDOW, _dot(qs, kw_t), NEG_MASK)
    new_w = neww_ref[j]
    s2 = _dot_nt(qs, _pad_keys(new_w, 0, gd))
    p1, p2, den = _softmax2(s1, s2, mask_new)
    o = _dot_nt(p1.astype(bf16), vw_t) + _dot(p2.astype(bf16), _pad_keys(new_w, gd, gd))
    ow_ref[j] = _unstack_heads(o / den, n_slab, tq)


def _dec_nsa(page_table, qn, cmp_pool, slc_pool, win_t, new_s, new_w, w_pool_e, *, qpos0, n_new):
    b, n_pages = page_table.shape
    w_buf = win_t.shape[-1]
    l_k = n_pages * PAGE_SIZE
    gd = NSA_KVH * HD
    n_seq = DEC_SEQS if b % DEC_SEQS == 0 else 1
    cfg = dict(n_pages=n_pages, n_seq=n_seq, qpos0=qpos0, n_new=n_new, w_buf=w_buf)
    wp = jnp.tile(jnp.repeat(jnp.transpose(w_pool_e, (0, 2, 1)), HD, axis=1), (1, 1, PAGE_SIZE // CMP_BLOCK))
    blk_of = np.arange(l_k) // CMP_BLOCK
    e_all = jnp.asarray(blk_of[:, None] == np.arange(LANES)[None, :], dtype=bf16)
    oh_t = jnp.asarray(np.arange(LANES)[:, None] == blk_of[None, :], dtype=bf16)
    page = lambda j, i: pl.BlockSpec((1, 2, NSA_KVH, HD, PAGE_SIZE),
                                     lambda bi, pt: (pt[n_seq * bi + j, i], 0, 0, 0, 0))
    pages = [page(j, i) for j in range(n_seq) for i in range(n_pages)]
    per_b = lambda shape: pl.BlockSpec((n_seq,) + shape, lambda bi, pt: (bi,) + (0,) * len(shape))
    const = lambda shape: pl.BlockSpec(shape, lambda bi, pt: (0,) * len(shape))
    grid_spec = pltpu.PrefetchScalarGridSpec(
        num_scalar_prefetch=1, grid=(b // n_seq,),
        in_specs=pages * 2
                 + [per_b((DEC_ROWS, NSA_H * HD)), per_b((2, NSA_KVH, HD, w_buf)), per_b((DEC_ROWS, 2 * gd)),
                    per_b((DEC_ROWS, 2 * gd)), const((2, gd, PAGE_SIZE)), const((l_k, LANES)), const((LANES, l_k))],
        out_specs=[per_b((DEC_ROWS, NSA_H * HD))] * 3,
        scratch_shapes=[pltpu.VMEM((n_seq, NSA_H * DEC_ROWS, LANES), f32)])
    return pl.pallas_call(
        functools.partial(_dec_nsa_body, cfg), grid_spec=grid_spec,
        out_shape=[jax.ShapeDtypeStruct((b, DEC_ROWS, NSA_H * HD), f32)] * 3,
        compiler_params=_cparams("parallel"),
        name="decode_nsa",
    )(page_table, *([cmp_pool] * (n_seq * n_pages)), *([slc_pool] * (n_seq * n_pages)), qn, win_t, new_s, new_w,
      wp, e_all, oh_t)


def _dec_fox_body(cfg, pt_ref, *refs):
    n_pages, n_seq = cfg["n_pages"], cfg["n_seq"]
    tail = refs[2 * n_seq * n_pages:]
    for j in range(n_seq):
        _dec_fox_one(cfg, j, refs[j * n_pages:(j + 1) * n_pages],
                     refs[(n_seq + j) * n_pages:(n_seq + j + 1) * n_pages], *tail)


def _dec_fox_one(cfg, j, kv_pages, lf_pages, qbd_ref, new_ref, lfn_ref, u_ref, prev_ref, o_ref):
    n_pages, n_new = cfg["n_pages"], cfg["n_new"]
    hd_all = FOX_H * HD
    qbd = qbd_ref[j]
    u = u_ref[...]

    local = _split_dot(jnp.concatenate([p[0] for p in lf_pages], axis=0), u, 3)
    totals = jnp.broadcast_to(local[:, PAGE_SIZE - 1:], local.shape)
    f_all = local + _split_dot_rhs(prev_ref[...], totals, 3)
    f_k = jnp.concatenate([f_all[i * FOX_H:(i + 1) * FOX_H] for i in range(n_pages)], axis=1)
    carry = f_all[(n_pages - 1) * FOX_H:, PAGE_SIZE - 1:]
    f_new = _split_dot(lfn_ref[j], u, 3)

    k_t = jnp.concatenate([p[0, 0].reshape(hd_all, PAGE_SIZE) for p in kv_pages], axis=1).astype(bf16)
    v_t = jnp.concatenate([p[0, 1].reshape(hd_all, PAGE_SIZE) for p in kv_pages], axis=1).astype(bf16)
    new = new_ref[j]
    s1 = _dot(qbd, k_t)
    s2 = _dot_nt(qbd, _pad_keys(new, 0, hd_all))
    kj = lax.broadcasted_iota(jnp.int32, (1, LANES), 1)
    s1_q, s2_q, m2_q = [], [], []
    for qi in range(n_new):
        f_q = carry + f_new[:, qi:qi + 1]
        s1_q.append(s1[qi * FOX_H:(qi + 1) * FOX_H] + f_q - f_k)
        s2_q.append(s2[qi * FOX_H:(qi + 1) * FOX_H] + f_q - (carry + f_new))
        m2_q.append(jnp.broadcast_to(kj <= qi, (FOX_H, LANES)))
    p1, p2, den = _softmax2(jnp.concatenate(s1_q, axis=0), jnp.concatenate(s2_q, axis=0),
                            jnp.concatenate(m2_q, axis=0))
    o = (_dot_nt(p1.astype(bf16), v_t) + _dot(p2.astype(bf16), _pad_keys(new, hd_all, hd_all))) / den
    head = lax.broadcasted_iota(jnp.int32, (FOX_H, hd_all), 0)
    own = jnp.right_shift(lax.broadcasted_iota(jnp.int32, (FOX_H, hd_all), 1), CMP_SHIFT) == head
    out = [jnp.sum(jnp.where(own, o[qi * FOX_H:(qi + 1) * FOX_H], 0.0), axis=0, keepdims=True)
           for qi in range(n_new)]
    out.append(jnp.zeros((DEC_ROWS - n_new, hd_all), f32))
    o_ref[j] = jnp.concatenate(out, axis=0).astype(o_ref.dtype)


def _dec_fox(page_table, qbd, kv_pool, lf_pool, new_kv, lf_new, *, n_new):
    b, n_pages = page_table.shape
    hd_all = FOX_H * HD
    n_seq = DEC_SEQS if b % DEC_SEQS == 0 else 1
    cfg = dict(n_pages=n_pages, n_seq=n_seq, n_new=n_new)
    kv_page = lambda j, i: pl.BlockSpec((1, 2, FOX_H, HD, PAGE_SIZE),
                                        lambda bi, pt: (pt[n_seq * bi + j, i], 0, 0, 0, 0))
    lf_page = lambda j, i: pl.BlockSpec((1, FOX_H, PAGE_SIZE), lambda bi, pt: (pt[n_seq * bi + j, i], 0, 0))
    per_b = lambda shape: pl.BlockSpec((n_seq,) + shape, lambda bi, pt: (bi,) + (0,) * len(shape))
    seq_pages = [(j, i) for j in range(n_seq) for i in range(n_pages)]
    grid_spec = pltpu.PrefetchScalarGridSpec(
        num_scalar_prefetch=1, grid=(b // n_seq,),
        in_specs=[kv_page(j, i) for j, i in seq_pages] + [lf_page(j, i) for j, i in seq_pages]
                 + [per_b((FOX_H * n_new, hd_all)), per_b((DEC_ROWS, 2 * hd_all)), per_b((FOX_H, LANES)),
                    pl.BlockSpec((PAGE_SIZE, PAGE_SIZE), lambda bi, pt: (0, 0)),
                    pl.BlockSpec((FOX_H * n_pages, FOX_H * n_pages), lambda bi, pt: (0, 0))],
        out_specs=per_b((DEC_ROWS, hd_all)))
    r = np.arange(FOX_H * n_pages)
    prev = jnp.asarray((r[:, None] % FOX_H == r[None, :] % FOX_H) & (r[None, :] // FOX_H < r[:, None] // FOX_H),
                       dtype=bf16)
    return pl.pallas_call(
        functools.partial(_dec_fox_body, cfg), grid_spec=grid_spec,
        out_shape=jax.ShapeDtypeStruct((b, DEC_ROWS, hd_all), bf16),
        compiler_params=_cparams("parallel"),
        name="decode_fox",
    )(page_table, *([kv_pool] * (n_seq * n_pages)), *([lf_pool] * (n_seq * n_pages)), qbd, new_kv, lf_new,
      _upper_ones(PAGE_SIZE), prev)


def _dec_diff_body(cfg, pt_ref, *refs):
    n_pages, n_new, lam_init = cfg["n_pages"], cfg["n_new"], cfg["lam_init"]
    pages = refs[:n_pages]
    q_ref, knew_ref, vnew_ref, lq_ref, sg_ref, o_ref, qs_ref = refs[n_pages:]
    tq = DEC_ROWS
    rows = 2 * DIFF_H * tq
    cols = PAGE_SIZE * DIFF_H
    _stack_heads(q_ref[0], qs_ref, DIFF_H, tq)
    row = lax.broadcasted_iota(jnp.int32, (rows, 1), 0)
    row_head = jnp.right_shift(row, 4)
    qs = qs_ref[...].astype(bf16)
    col = lax.broadcasted_iota(jnp.int32, (1, cols), 1)
    own = (col & (DIFF_H - 1)) == row_head

    def part(s, mask, v_rows):
        s = jnp.where(mask, s, NEG_MASK)
        m = jnp.max(s, axis=1, keepdims=True)
        p = jnp.exp(s - m)
        return m, jnp.sum(p, axis=1, keepdims=True), _dot(p.astype(bf16), v_rows)

    parts = []
    for p in pages:
        k_rows = p[0, :, 0].reshape(cols, 2 * HD).astype(bf16)
        v_rows = p[0, :, 1].reshape(cols, 2 * HD).astype(bf16)
        parts.append(part(_dot_nt(qs, k_rows), own, v_rows))
    pad = jnp.zeros((LANES - knew_ref.shape[1], 2 * HD), f32)
    k_new = jnp.concatenate([knew_ref[0], pad], axis=0).astype(bf16)
    v_new = jnp.concatenate([vnew_ref[0], pad], axis=0).astype(bf16)
    lane = lax.broadcasted_iota(jnp.int32, (1, LANES), 1)
    mask_new = jnp.logical_and(jnp.logical_and((lane & (DIFF_H - 1)) == row_head,
                                               jnp.right_shift(lane, 3) <= (row & (tq - 1))),
                               lane < n_new * DIFF_H)
    parts.append(part(_dot_nt(qs, k_new), mask_new, v_new))
    m_all = functools.reduce(jnp.maximum, [m for m, _, _ in parts])
    scale = [jnp.exp(m - m_all) for m, _, _ in parts]
    l_fin = functools.reduce(lambda a, b_: a + b_, [w * l for w, (_, l, _) in zip(scale, parts)])
    acc = functools.reduce(lambda a, b_: a + b_, [w * a for w, (_, _, a) in zip(scale, parts)])
    o = acc / jnp.maximum(l_fin, 1e-30)
    heads = [_diff_combine(o[(2 * h) * tq:(2 * h + 1) * tq], o[(2 * h + 1) * tq:(2 * h + 2) * tq],
                           lq_ref[...], sg_ref[...], lam_init) for h in range(DIFF_H)]
    o_ref[0] = jnp.concatenate(heads, axis=1).astype(o_ref.dtype)


def _dec_diff(page_table, qd, pool, k_new, v_new, lam_qk_o, sub_g, *, n_new, lam_init):
    b, n_pages = page_table.shape
    cfg = dict(n_pages=n_pages, n_new=n_new, lam_init=lam_init)
    page = lambda i: pl.BlockSpec((1, PAGE_SIZE, 2, DIFF_H, 2 * HD), lambda bi, pt: (pt[bi, i], 0, 0, 0, 0))
    per_b = lambda shape: pl.BlockSpec((1,) + shape, lambda bi, pt: (bi,) + (0,) * len(shape))
    const = lambda shape: pl.BlockSpec(shape, lambda bi, pt: (0,) * len(shape))
    width = 2 * DIFF_H * HD
    grid_spec = pltpu.PrefetchScalarGridSpec(
        num_scalar_prefetch=1, grid=(b,),
        in_specs=[page(i) for i in range(n_pages)]
                 + [per_b((DEC_ROWS, width)), per_b((DIFF_H * n_new, 2 * HD)), per_b((DIFF_H * n_new, 2 * HD)),
                    const((4, HD)), const((1, 2 * HD))],
        out_specs=per_b((DEC_ROWS, width)),
        scratch_shapes=[pltpu.VMEM((2 * DIFF_H * DEC_ROWS, LANES), f32)])
    return pl.pallas_call(
        functools.partial(_dec_diff_body, cfg), grid_spec=grid_spec,
        out_shape=jax.ShapeDtypeStruct((b, DEC_ROWS, width), bf16),
        compiler_params=_cparams("parallel"),
        name="decode_diff",
    )(page_table, *([pool] * n_pages), qd, k_new, v_new, lam_qk_o, sub_g)


def _pad_rows(a, n):
    return jnp.pad(a, ((0, 0), (0, n - a.shape[1])) + ((0, 0),) * (a.ndim - 2))


def _trunk(x, mods, qpos0, page_table, caches, wts, params):
    (b_f, w_pool, lam_qk, subln_g, norm_mix_g, norm_ffn_g, final_g) = params
    b, t, d = x.shape
    m = b * t
    decode = caches is not None
    depth = len(mods)
    tm = min(256, m)
    x2d = x.reshape(m, d)
    pos = qpos0 + np.tile(np.arange(t), b)
    tables = _rope_tables(pos)

    if decode:
        tiles_per_mod = 1
        tm_mlp, mlp_tiles_per_mod = tm, 1
        expand = lambda v: jnp.repeat(v, t, axis=0).reshape(m // tm, tm, d)
    else:
        tiles_per_mod = t // tm
        tm_mlp = min(1024, t)
        mlp_tiles_per_mod = t // tm_mlp
        expand = lambda v: v.reshape(b, 1, d)

    per_seq = lambda a, rows=DEC_ROWS: _pad_rows(a.reshape(b, t, a.shape[-1]), rows)
    outs = {}
    for li in range(depth):
        sh1, sc1, gt1, sh2, sc2, gt2 = [expand(v) for v in jnp.split(mods[li], 6, axis=-1)]
        if li % 2 == 0:
            e = li // 2
            bias = jnp.pad(b_f[e].reshape(1, FOX_H), ((0, 0), (0, LANES - FOX_H)))
            po = _proj(x2d, norm_mix_g[li], sc1, sh1, tiles_per_mod, wts["in_even"][e], tables, bias,
                       _even_plan(), _even_out_defs(not decode), tm, t)
            logf = po[O_LOGF][:, :FOX_H].reshape(b, t, FOX_H)
            if decode:
                kv_leaf = lambda oi, heads: po[oi].reshape(b, t, 2, heads, HD)
            else:
                kv_leaf = lambda oi, heads: jnp.transpose(po[oi].reshape(b, 2, heads, HD, t), (0, 4, 1, 2, 3))
            outs.setdefault("cmp", []).append(kv_leaf(O_KVC, NSA_KVH))
            outs.setdefault("slc", []).append(kv_leaf(O_KVS, NSA_KVH))
            outs.setdefault("fox", []).append(kv_leaf(O_KVF, FOX_H))
            outs.setdefault("logf", []).append(logf)
            kvw_new = kv_leaf(O_KVW, NSA_KVH)
            if not decode:
                outs.setdefault("win", []).append(kvw_new[:, t - min(WINDOW, t):])
                qn, qf = po[O_QN].reshape(b, t, NSA_H * HD), po[O_QF].reshape(b, t, FOX_H * HD)
                cmp = _pad_rows(_compress(po[O_KVC_ROWS].reshape(b, t, 4 * HD), w_pool[e]), LANES)
                ck, cv = cmp[:, :, :LANES].astype(bf16), cmp[:, :, LANES:].astype(bf16)
                o_cmp, mb = _cmp_topk(qn, ck, cv, tq=min(256, t), nb_valid=t // CMP_BLOCK)
                lane_slab = lambda oi, w: po[oi].reshape(b, t, w)
                o_s = _flash("slc", qn, lane_slab(O_KS, LANES), lane_slab(O_VS, LANES), [mb],
                             tq=min(128, t), tk=min(FLASH_TK, t), out_dtype=f32)
                o_w = _flash("win", qn, lane_slab(O_KW, LANES), lane_slab(O_VW, LANES), [],
                             tq=min(256, t), tk=min(512, t), out_dtype=f32)
                tk_f = min(FLASH_TK, t)
                f_t = _cumsum_lanes(jnp.transpose(logf, (0, 2, 1)), min(512, t))
                fk = jnp.transpose(f_t.reshape(b, FOX_H // 2, 2, t // tk_f, tk_f), (0, 1, 3, 2, 4))
                fq = f_t.reshape(b, FOX_H // 2, 2, t, 1)
                o_f = _flash("fox", qf, lane_slab(O_KF, 4 * LANES), lane_slab(O_VF, 4 * LANES), [fk, fq],
                             tq=min(512, t), tk=tk_f, out_dtype=bf16)
                acts = [a.reshape(m, a.shape[-1]) for a in (o_cmp, o_s, o_w)] + [po[O_GATE], o_f.reshape(m, -1)]
            else:
                c_cmp, c_slc, c_win, c_fox, c_logf, _ = caches
                outs.setdefault("win", []).append(jnp.concatenate([c_win[e], kvw_new], axis=1)[:, t:])
                kt_view = lambda c: jnp.transpose(c, (0, 2, 3, 4, 1))
                qn = per_seq(po[O_QN])
                o_cmp, o_s, o_w = _dec_nsa(page_table, qn, kt_view(c_cmp[e]), kt_view(c_slc[e]),
                                           kt_view(c_win[e]), per_seq(po[O_KVS]), per_seq(po[O_KVW]),
                                           w_pool[e], qpos0=qpos0, n_new=t)
                qf = po[O_QF].reshape(b, t, 1, FOX_H * HD)
                head_of_lane = np.arange(FOX_H * HD) // HD
                own = jnp.asarray(head_of_lane[None, :] == np.arange(FOX_H)[:, None], dtype=bf16)
                qbd = (qf * own[None, None]).reshape(b, t * FOX_H, FOX_H * HD)
                lf_new = jnp.pad(jnp.transpose(logf, (0, 2, 1)), ((0, 0), (0, 0), (0, LANES - t)))
                o_f = _dec_fox(page_table, qbd, kt_view(c_fox[e]), jnp.transpose(c_logf[e], (0, 2, 1)),
                               per_seq(po[O_KVF]), lf_new, n_new=t)
                acts = [a[:, :t].reshape(m, a.shape[-1]) for a in (o_cmp, o_s, o_w)]
                acts += [po[O_GATE], o_f[:, :t].reshape(m, -1)]
            x2d = _outproj(_outproj_even_body, x2d, gt1, tiles_per_mod, acts, wts["out_even"][e], tm,
                           "outproj_even")
        else:
            o = li // 2
            po = _proj(x2d, norm_mix_g[li], sc1, sh1, tiles_per_mod, wts["in_odd"][o], tables,
                       jnp.zeros((1, LANES), f32), _odd_plan(), ODD_OUT_DEFS, tm, t)
            outs.setdefault("diff", []).append(po[O_KVD].reshape(b, t, 2, DIFF_H, 2 * HD))
            lam_init = 0.8 - 0.6 * math.exp(-0.3 * li)
            sub_g = subln_g[o].reshape(1, 2 * HD)
            width = 2 * DIFF_H * HD
            if not decode:
                o_d = _flash("diff", po[O_QD].reshape(b, t, width), po[O_KD].reshape(b, t, width),
                             po[O_VD].reshape(b, t, width), [lam_qk[o], sub_g], tq=min(512, t), tk=min(FLASH_TK, t),
                             out_dtype=bf16, lam_init=lam_init)
            else:
                kvd = po[O_KVD].reshape(b, t, 2, DIFF_H, 2 * HD)
                rows_new = lambda a: a.reshape(b, t * DIFF_H, 2 * HD)
                o_d = _dec_diff(page_table, per_seq(po[O_QD]), caches[5][o], rows_new(kvd[:, :, 0]),
                                rows_new(kvd[:, :, 1]), lam_qk[o], sub_g, n_new=t, lam_init=lam_init)[:, :t]
            x2d = _outproj(_outproj_odd_body, x2d, gt1, tiles_per_mod, [o_d.reshape(m, d)],
                           wts["out_odd"][o], tm, "outproj_odd")
        x2d = _mlp(x2d, norm_ffn_g[li], sc2, sh2, gt2, mlp_tiles_per_mod, wts["ff_in"][li], wts["ff_out"][li],
                   final_g, li == depth - 1, tm_mlp)
    stack = lambda name: jnp.stack(outs[name], 0)
    return (x2d.reshape(b, t, d), stack("cmp"), stack("slc"), stack("win"), stack("fox"), stack("logf"),
            stack("diff"))


def kernel(x_prompt, x_sample, cache_nsa_cmp_kv, cache_nsa_slc_kv, cache_nsa_win_kv, cache_fox_kv,
           cache_fox_logf, cache_diff_kv, page_table, c_prompt, c_sample, w_in_even, b_f, w_pool,
           w_out_even, w_in_odd, lam_qk, subln_g, w_out_odd, norm_mix_g, norm_ffn_g, w_ada, b_ada,
           w_ff_in, w_ff_out, final_g):
    depth = w_ada.shape[0]
    nb_p, nb_s = c_prompt.shape[0], c_sample.shape[0]
    past_len = page_table.shape[1] * PAGE_SIZE

    perm_rows = lambda w: jnp.concatenate(
        [w[:NSA_H * HD].reshape(NSA_H, HD, -1)[NSA_PERM, :, :].reshape(NSA_H * HD, -1), w[NSA_H * HD:]], axis=0)
    wts = dict(
        in_even=[_even_weight(w_in_even[e]) for e in range(w_in_even.shape[0])],
        out_even=[perm_rows(w_out_even[e]).astype(bf16) for e in range(w_out_even.shape[0])],
        in_odd=[w_in_odd[o].astype(bf16) for o in range(w_in_odd.shape[0])],
        out_odd=[w_out_odd[o].astype(bf16) for o in range(w_out_odd.shape[0])],
        ff_in=[w_ff_in[l].astype(bf16) for l in range(depth)],
        ff_out=[w_ff_out[l].astype(bf16) for l in range(depth)],
    )
    rows = -(-(nb_p + nb_s) // SUBLANES) * SUBLANES
    c_all = jnp.pad(jnp.concatenate([c_prompt, c_sample], axis=0), ((0, rows - nb_p - nb_s), (0, 0)))
    mod_all = _ada(c_all, w_ada, b_ada)
    mods_p = [mod_all[l, :nb_p] for l in range(depth)]
    mods_s = [mod_all[l, nb_p:nb_p + nb_s] for l in range(depth)]

    params = (b_f, w_pool, lam_qk, subln_g, norm_mix_g, norm_ffn_g, final_g)
    caches = (cache_nsa_cmp_kv, cache_nsa_slc_kv, cache_nsa_win_kv, cache_fox_kv, cache_fox_logf, cache_diff_kv)
    y_p, p_cmp, p_slc, p_win, p_fox, p_logf, p_diff = _trunk(x_prompt, mods_p, 0, None, None, wts, params)
    y_s, s_cmp, s_slc, s_win, s_fox, s_logf, s_diff = _trunk(x_sample, mods_s, past_len, page_table, caches,
                                                             wts, params)
    return (y_p, y_s, p_cmp, s_cmp, p_slc, s_slc, p_win, s_win, p_fox, s_fox, p_logf, s_logf, p_diff, s_diff)
```
